```python
import math
import jax, jax.numpy as jnp
from jax import lax
import numpy as np

D_MODEL = 1024
BATCH = 16
SEQ = 2048
DEPTH = 1
DEC_BATCH = 8
DEC_SEQ = 32
PAST_LEN = 1024

CHUNK = 64
N_HEADS = 8
N_KV = 2
HEAD_DIM = 128
ATT_W = N_HEADS * HEAD_DIM
KV_W = N_KV * HEAD_DIM
ROT_FRAC = 4
ROPE_THETA = 500000.0
IDX_HEADS = 8
IDX_DIM = 64
TOPK_MAX = 256
CONV_W = 3
CONV_DIM = D_MODEL
Q_BLOCK = 128
EPS = 1e-6
SPLIT_SIZES = (ATT_W, KV_W, KV_W, ATT_W, IDX_HEADS * IDX_DIM, IDX_DIM, IDX_HEADS,
               CONV_DIM, CONV_DIM, CONV_DIM, CONV_DIM, D_MODEL, D_MODEL)
PROJ_W = 4 * ATT_W // 2 + 2 * KV_W + IDX_HEADS * IDX_DIM + IDX_DIM + IDX_HEADS + 4 * CONV_DIM + 2 * D_MODEL

kernel_name = "hybrid_streaming_dsa_shortconv_step"


def rms_norm(x, g):
    xf = x.astype(jnp.float32)
    y = xf * lax.rsqrt(jnp.mean(xf * xf, axis=-1, keepdims=True) + EPS)
    return (y * g.astype(jnp.float32)).astype(x.dtype)


def partial_rope(x, pos):
    dh = x.shape[-1]
    rot = dh // ROT_FRAC
    half = rot // 2
    inv = ROPE_THETA ** (-2.0 * jnp.arange(half, dtype=jnp.float32) / rot)
    ang = pos.astype(jnp.float32)[:, None] * inv[None, :]
    cos = jnp.cos(ang)[:, None, :]
    sin = jnp.sin(ang)[:, None, :]
    xr = x[..., :rot].astype(jnp.float32)
    x1, x2 = xr[..., :half], xr[..., half:]
    out = jnp.concatenate([x1 * cos - x2 * sin, x2 * cos + x1 * sin], axis=-1)
    return jnp.concatenate([out.astype(x.dtype), x[..., rot:]], axis=-1)


def sparse_attention(q, qi, wi, q_pos, k, v, ki, k_pos, topk):
    B, Q = q.shape[0], q.shape[1]
    logits = jnp.einsum('bqhd,bsd->bqhs', qi, ki).astype(jnp.float32) * (IDX_DIM ** -0.5)
    score = jnp.einsum('bqh,bqhs->bqs', wi.astype(jnp.float32) * (IDX_HEADS ** -0.5),
                       jax.nn.relu(logits))
    admissible = (k_pos[None, :] // CHUNK) <= (q_pos[:, None] // CHUNK)
    score = jnp.where(admissible[None], score, -jnp.inf)
    _, idx = lax.top_k(score, topk)
    valid = admissible[jnp.arange(Q)[None, :, None], idx]
    bidx = jnp.arange(B)[:, None, None]
    kg = k[bidx, idx]
    vg = v[bidx, idx]
    qg = q.reshape(B, Q, N_KV, N_HEADS // N_KV, HEAD_DIM)
    s = jnp.einsum('bqhgd,bqkhd->bqhgk', qg, kg).astype(jnp.float32) * (HEAD_DIM ** -0.5)
    s = jnp.where(valid[:, :, None, None, :], s, -jnp.inf)
    p = jax.nn.softmax(s, axis=-1).astype(v.dtype)
    o = jnp.einsum('bqhgk,bqkhd->bqhgd', p, vg)
    return o.reshape(B, Q, ATT_W)


def encoder_layer(x, c, pos, past_k, past_v, past_ki, conv_state,
                  w_ada, b_ada, norm_g, w_in, q_norm_g, k_norm_g, conv_w, w_pa, w_pb, w_out):
    B, T, _ = x.shape
    P = past_k.shape[1]
    mod = jax.nn.silu(c) @ w_ada + b_ada
    shift, scale, gate = jnp.split(mod, 3, axis=-1)
    h = rms_norm(x, norm_g) * (1.0 + scale[:, None, :]) + shift[:, None, :]
    proj = h @ w_in
    (q, k, v, za, qi, ki, wi, u, bg, cg, zb, ga, gb) = jnp.split(
        proj, list(np.cumsum(SPLIT_SIZES)[:-1]), axis=-1)
    q = partial_rope(rms_norm(q.reshape(B, T, N_HEADS, HEAD_DIM), q_norm_g), pos)
    k = partial_rope(rms_norm(k.reshape(B, T, N_KV, HEAD_DIM), k_norm_g), pos)
    v = v.reshape(B, T, N_KV, HEAD_DIM)
    qi = partial_rope(qi.reshape(B, T, IDX_HEADS, IDX_DIM), pos)
    ki = partial_rope(ki[:, :, None, :], pos)[:, :, 0, :]

    k_all = jnp.concatenate([past_k, k], axis=1)
    v_all = jnp.concatenate([past_v, v], axis=1)
    ki_all = jnp.concatenate([past_ki, ki], axis=1)
    L = P + T
    k_pos = jnp.arange(L, dtype=jnp.int32)
    topk = min(TOPK_MAX, L // 4)
    qb = min(Q_BLOCK, T)
    nb = T // qb

    def to_blocks(a):
        return jnp.moveaxis(a.reshape((B, nb, qb) + a.shape[2:]), 1, 0)

    o = lax.map(lambda args: sparse_attention(args[0], args[1], args[2], args[3],
                                              k_all, v_all, ki_all, k_pos, topk),
                (to_blocks(q), to_blocks(qi), to_blocks(wi), pos.reshape(nb, qb)))
    o = jnp.moveaxis(o, 0, 1).reshape(B, T, ATT_W)
    a_out = (o * jax.nn.silu(za)) @ w_pa

    cv = cg * u
    cp = jnp.concatenate([conv_state, cv], axis=1)
    y_conv = conv_w[0] * cp[:, 0:T]
    for j in range(1, CONV_W):
        y_conv = y_conv + conv_w[j] * cp[:, j:j + T]
    b_out = (bg * y_conv * jax.nn.silu(zb)) @ w_pb
    new_conv = cp[:, -(CONV_W - 1):]

    merged = jax.nn.sigmoid(ga) * a_out + jax.nn.sigmoid(gb) * b_out
    y = x + gate[:, None, :] * (merged @ w_out)
    return y, k, v, ki, new_conv


def setup_inputs(seed: int = 0) -> dict:
    key = jax.random.key(seed)
    ks = jax.random.split(key, 20)
    f = jnp.float32
    nrm = lambda k, shape, s: jax.random.normal(k, shape, f) * s
    return {
        "x_prompt": nrm(ks[0], (BATCH, SEQ, D_MODEL), 1.0),
        "x_sample": nrm(ks[1], (DEC_BATCH, DEC_SEQ, D_MODEL), 1.0),
        "cache_k": nrm(ks[2], (DEC_BATCH, PAST_LEN, N_KV, HEAD_DIM), 1.0),
        "cache_v": nrm(ks[3], (DEC_BATCH, PAST_LEN, N_KV, HEAD_DIM), 1.0),
        "cache_idx_k": nrm(ks[4], (DEC_BATCH, PAST_LEN, IDX_DIM), 1.0),
        "state_conv": nrm(ks[5], (DEC_BATCH, CONV_W - 1, CONV_DIM), 1.0),
        "c_prompt": nrm(ks[6], (BATCH, D_MODEL), 1.0),
        "c_sample": nrm(ks[7], (DEC_BATCH, D_MODEL), 1.0),
        "w_ada": nrm(ks[8], (D_MODEL, 3 * D_MODEL), 0.5 * D_MODEL ** -0.5),
        "b_ada": nrm(ks[9], (3 * D_MODEL,), 0.01),
        "norm_g": 1.0 + nrm(ks[10], (D_MODEL,), 0.02),
        "w_in": nrm(ks[11], (D_MODEL, PROJ_W), D_MODEL ** -0.5),
        "q_norm_g": 1.0 + nrm(ks[12], (HEAD_DIM,), 0.02),
        "k_norm_g": 1.0 + nrm(ks[13], (HEAD_DIM,), 0.02),
        "conv_w": nrm(ks[14], (CONV_W, CONV_DIM), CONV_W ** -0.5),
        "w_pa": nrm(ks[15], (ATT_W, D_MODEL), ATT_W ** -0.5),
        "w_pb": nrm(ks[16], (CONV_DIM, D_MODEL), CONV_DIM ** -0.5),
        "w_out": nrm(ks[17], (D_MODEL, D_MODEL), D_MODEL ** -0.5),
    }


def reference(x_prompt, x_sample, cache_k, cache_v, cache_idx_k, state_conv, c_prompt, c_sample,
              w_ada, b_ada, norm_g, w_in, q_norm_g, k_norm_g, conv_w, w_pa, w_pb, w_out):
    dt = x_prompt.dtype
    yp = x_prompt
    for _ in range(DEPTH):
        yp, k_p, v_p, ki_p, conv_p = encoder_layer(
            yp, c_prompt, jnp.arange(SEQ, dtype=jnp.int32),
            jnp.zeros((BATCH, 0, N_KV, HEAD_DIM), dt), jnp.zeros((BATCH, 0, N_KV, HEAD_DIM), dt),
            jnp.zeros((BATCH, 0, IDX_DIM), dt), jnp.zeros((BATCH, CONV_W - 1, CONV_DIM), dt),
            w_ada, b_ada, norm_g, w_in, q_norm_g, k_norm_g, conv_w, w_pa, w_pb, w_out)
    ys = x_sample
    for _ in range(DEPTH):
        ys, k_s, v_s, ki_s, conv_s = encoder_layer(
            ys, c_sample, PAST_LEN + jnp.arange(DEC_SEQ, dtype=jnp.int32),
            cache_k, cache_v, cache_idx_k, state_conv,
            w_ada, b_ada, norm_g, w_in, q_norm_g, k_norm_g, conv_w, w_pa, w_pb, w_out)
    return (yp, ys, k_p, v_p, ki_p, conv_p, k_s, v_s, ki_s, conv_s)
```

```python
import functools

import jax
import jax.numpy as jnp
import numpy as np
from jax import lax
from jax.experimental import pallas as pl
from jax.experimental.pallas import tpu as pltpu

D_MODEL = 1024
CHUNK = 64
CHUNK_SHIFT = 6
N_HEADS = 8
N_KV = 2
HEAD_DIM = 128
ATT_W = N_HEADS * HEAD_DIM
KV_W = N_KV * HEAD_DIM
ROT_FRAC = 4
ROPE_THETA = 500000.0
IDX_HEADS = 8
IDX_DIM = 64
TOPK_MAX = 256
CONV_W = 3
EPS = 1e-6

LANES = 128
SUBLANES = 8
QB = 128
VMEM_LIMIT = 56 * 1024 * 1024

C_Q, C_K, C_V, C_ZA, C_QI = 0, 1024, 1280, 1536, 2560
C_U, C_BG, C_CG, C_ZB, C_GA, C_GB = 3072, 4096, 5120, 6144, 7168, 8192
C_KI, C_WI, C_END = 9216, 9344, 9472

NEG_BIG = -1e30
NEG_INF_KEY = -2139095041
INT_MIN = -2147483648

_NT = (((1,), (1,)), ((), ()))


def _bf(x):
    return x.astype(jnp.bfloat16)


def _dot(a, b):
    return jnp.dot(a, b, preferred_element_type=jnp.float32)


def _dot_nt(a, b):
    return lax.dot_general(a, b, _NT, preferred_element_type=jnp.float32)


def _silu(x):
    return x * jax.nn.sigmoid(x)


def _mod_kernel(c_ref, w_ref, b_ref, o_ref):
    c = c_ref[...]
    s = _silu(c)
    w = w_ref[...]
    s_hi = _bf(s)
    s_lo = _bf(s - s_hi.astype(jnp.float32))
    w_hi = _bf(w)
    w_lo = _bf(w - w_hi.astype(jnp.float32))
    acc = _dot(s_hi, w_hi) + (_dot(s_lo, w_hi) + _dot(s_hi, w_lo))
    o_ref[...] = acc + b_ref[...]


def _mod_call(c_all, w_ada, b_ada):
    nb, d = c_all.shape
    n = w_ada.shape[1]
    bn = 1024
    return pl.pallas_call(
        _mod_kernel,
        grid=(n // bn,),
        in_specs=[
            pl.BlockSpec((nb, d), lambda i: (0, 0)),
            pl.BlockSpec((d, bn), lambda i: (0, i)),
            pl.BlockSpec((1, bn), lambda i: (0, i)),
        ],
        out_specs=pl.BlockSpec((nb, bn), lambda i: (0, i)),
        out_shape=jax.ShapeDtypeStruct((nb, n), jnp.float32),
        compiler_params=pltpu.CompilerParams(dimension_semantics=("arbitrary",)),
        name="adaln_mod",
    )(c_all, w_ada, b_ada.reshape(1, n))


def _rope(x, cos, sin_up, sin_dn, half):
    up = pltpu.roll(x, LANES - half, 1)
    dn = pltpu.roll(x, half, 1)
    return x * cos + up * sin_up + dn * sin_dn


def _proj_kernel(x_ref, shift_ref, scale_ref, ng_ref, qg_ref, kg_ref,
                 rc_ref, rs1_ref, rs2_ref, ic_ref, is1_ref, is2_ref,
                 w_ref, wpb_ref, cw_ref, cs_ref,
                 q_ref, k_ref, v_ref, vt_ref, sza_ref, qi_ref, ki_ref, ki2_ref, wi_ref,
                 sga_ref, gbb_ref, conv_ref, carry_ref, *, tm, conv_row):
    t = pl.program_id(1)
    x = x_ref[0]
    ms = jnp.mean(x * x, axis=-1, keepdims=True)
    xn = x * lax.rsqrt(ms + EPS) * ng_ref[...]
    h = xn * (1.0 + scale_ref[0]) + shift_ref[0]
    hb = _bf(h)

    def proj(c0, c1):
        return _dot(hb, w_ref[:, c0:c1])

    rc, rs1, rs2 = rc_ref[...], rs1_ref[...], rs2_ref[...]
    ic, is1, is2 = ic_ref[...], is1_ref[...], is2_ref[...]
    rot_half = HEAD_DIM // ROT_FRAC // 2
    idx_half = IDX_DIM // ROT_FRAC // 2

    def head_norm_rope(xh, g):
        r = lax.rsqrt(jnp.mean(xh * xh, axis=-1, keepdims=True) + EPS)
        return _rope(xh * r * g, rc, rs1, rs2, rot_half)

    q = proj(C_Q, C_K)
    qg = qg_ref[...]
    for hh in range(N_HEADS):
        sl = slice(hh * HEAD_DIM, (hh + 1) * HEAD_DIM)
        q_ref[0, :, sl] = _bf(head_norm_rope(q[:, sl], qg) * (HEAD_DIM ** -0.5))

    k = proj(C_K, C_V)
    kg = kg_ref[...]
    for hh in range(N_KV):
        sl = slice(hh * HEAD_DIM, (hh + 1) * HEAD_DIM)
        k_ref[0, :, sl] = head_norm_rope(k[:, sl], kg)

    v = proj(C_V, C_ZA)
    v_ref[0] = v
    vt_ref[0, 0] = _bf(v.T)

    sza_ref[0] = _silu(proj(C_ZA, C_QI))

    qi = proj(C_QI, C_U)
    for p in range(IDX_HEADS * IDX_DIM // LANES):
        sl = slice(p * LANES, (p + 1) * LANES)
        qi_ref[0, :, sl] = _bf(_rope(qi[:, sl], ic, is1, is2, idx_half) * (IDX_DIM ** -0.5))

    ki2 = _rope(proj(C_KI, C_WI), ic, is1, is2, idx_half)
    ki2_ref[0] = ki2
    ki_ref[0] = ki2[:, :IDX_DIM]
    wi_ref[0] = proj(C_WI, C_END) * (IDX_HEADS ** -0.5)

    cv = proj(C_CG, C_ZB) * proj(C_U, C_BG)

    @pl.when(t == 0)
    def _():
        carry_ref[0:2, :] = cs_ref[0]

    c0 = carry_ref[0:1, :]
    c1 = carry_ref[1:2, :]
    row = lax.broadcasted_iota(jnp.int32, (tm, 1), 0)
    r1 = jnp.where(row == 0, c1, pltpu.roll(cv, 1, 0))
    r2 = jnp.where(row == 0, c0, jnp.where(row == 1, c1, pltpu.roll(cv, 2, 0)))
    cw = cw_ref[...]
    y_conv = cw[0:1, :] * r2 + cw[1:2, :] * r1 + cw[2:3, :] * cv
    carry_ref[0:2, :] = cv[tm - 2:tm, :]
    conv_ref[0] = cv[conv_row - 2:conv_row, :]

    tb = proj(C_BG, C_CG) * y_conv * _silu(proj(C_ZB, C_GA))
    b_out = _dot(_bf(tb), wpb_ref[...])
    gbb_ref[0] = jax.nn.sigmoid(proj(C_GB, C_KI)) * b_out
    sga_ref[0] = jax.nn.sigmoid(proj(C_GA, C_GB))


def _proj_call(x, shift, scale, norm_g, qg, kg, rope_tabs, w_r, w_pb, conv_w, conv_state,
               *, tm, conv_row):
    b, t, d = x.shape
    nt = t // tm
    tok = lambda w: pl.BlockSpec((1, tm, w), lambda i, j: (i, j, 0))
    per_b = lambda r, w: pl.BlockSpec((1, r, w), lambda i, j: (i, 0, 0))
    const = lambda r, w: pl.BlockSpec((r, w), lambda i, j: (0, 0))
    tab = pl.BlockSpec((tm, LANES), lambda i, j: (j, 0))
    whole = pl.BlockSpec(memory_space=pltpu.VMEM)
    f32, bf16 = jnp.float32, jnp.bfloat16
    out_shape = (
        jax.ShapeDtypeStruct((b, t, ATT_W), bf16),
        jax.ShapeDtypeStruct((b, t, KV_W), f32),
        jax.ShapeDtypeStruct((b, t, KV_W), f32),
        jax.ShapeDtypeStruct((b, nt, KV_W, tm), bf16),
        jax.ShapeDtypeStruct((b, t, ATT_W), f32),
        jax.ShapeDtypeStruct((b, t, IDX_HEADS * IDX_DIM), bf16),
        jax.ShapeDtypeStruct((b, t, IDX_DIM), f32),
        jax.ShapeDtypeStruct((b, t, LANES), f32),
        jax.ShapeDtypeStruct((b, t, LANES), f32),
        jax.ShapeDtypeStruct((b, t, D_MODEL), f32),
        jax.ShapeDtypeStruct((b, t, D_MODEL), f32),
        jax.ShapeDtypeStruct((b, CONV_W - 1, D_MODEL), f32),
    )
    out_specs = (
        tok(ATT_W), tok(KV_W), tok(KV_W),
        pl.BlockSpec((1, 1, KV_W, tm), lambda i, j: (i, j, 0, 0)),
        tok(ATT_W), tok(IDX_HEADS * IDX_DIM), tok(IDX_DIM), tok(LANES), tok(LANES),
        tok(D_MODEL), tok(D_MODEL), per_b(CONV_W - 1, D_MODEL),
    )
    in_specs = [
        tok(d), per_b(1, d), per_b(1, d), const(1, d), const(1, LANES), const(1, LANES),
        tab, tab, tab, tab, tab, tab,
        whole, whole, const(CONV_W, d), per_b(CONV_W - 1, d),
    ]
    return pl.pallas_call(
        functools.partial(_proj_kernel, tm=tm, conv_row=conv_row),
        grid=(b, nt),
        in_specs=in_specs,
        out_specs=out_specs,
        out_shape=out_shape,
        scratch_shapes=[pltpu.VMEM((SUBLANES, d), jnp.float32)],
        compiler_params=pltpu.CompilerParams(
            dimension_semantics=("arbitrary", "arbitrary"), vmem_limit_bytes=VMEM_LIMIT),
        name="in_proj",
    )(x, shift, scale, norm_g, qg, kg, *rope_tabs, w_r, w_pb, conv_w, conv_state)


def _sort_key(x):
    bits = lax.bitcast_convert_type(x, jnp.int32)
    return bits ^ ((bits >> 31) & 0x7FFFFFFF)


def _attn_kernel(q_ref, qi_ref, wi_ref, k_ref, vt_ref, ki2_ref,
                 sza_ref, sga_ref, gbb_ref, x_ref, gate_ref, wpa_ref, wout_ref,
                 y_ref, keys_ref, bias_ref, o_ref, *, pos0, l_real, kt_size, topk):
    KT = kt_size
    j = pl.program_id(1)
    q0 = pos0 + j * QB
    kmax = jnp.minimum(l_real, (((q0 + QB - 1) >> CHUNK_SHIFT) + 1) * CHUNK)
    nkt = (kmax + KT - 1) // KT

    lane = lax.broadcasted_iota(jnp.int32, (1, QB), 1)
    q_chunk = (q0 + lane) >> CHUNK_SHIFT
    w_t = wi_ref[0].T
    qi = qi_ref[0]
    lane_kt = lax.broadcasted_iota(jnp.int32, (KT, LANES), 1)
    sub_kt = lax.broadcasted_iota(jnp.int32, (KT, 1), 0)

    def score_body(kt, carry):
        k0 = pl.multiple_of(kt * KT, KT)
        ki2 = ki2_ref[0, pl.ds(k0, KT), :]
        ki_lo = _bf(jnp.where(lane_kt < IDX_DIM, ki2, 0.0))
        ki_hi = _bf(jnp.where(lane_kt >= IDX_DIM, ki2, 0.0))
        acc = jnp.zeros((KT, QB), jnp.float32)
        for p in range(IDX_HEADS // 2):
            slab = qi[:, p * LANES:(p + 1) * LANES]
            acc = acc + w_t[2 * p:2 * p + 1, :] * jnp.maximum(_dot_nt(ki_lo, slab), 0.0)
            acc = acc + w_t[2 * p + 1:2 * p + 2, :] * jnp.maximum(_dot_nt(ki_hi, slab), 0.0)
        kpos = k0 + sub_kt
        adm = ((kpos >> CHUNK_SHIFT) <= q_chunk) & (kpos < l_real)
        keys_ref[pl.ds(k0, KT), :] = _sort_key(jnp.where(adm, acc, -jnp.inf))
        return carry

    lax.fori_loop(0, nkt, score_body, 0)

    def count(pred):
        def body(kt, acc):
            k0 = pl.multiple_of(kt * KT, KT)
            m = pred(keys_ref[pl.ds(k0, KT), :]).astype(jnp.int32)
            return acc + jnp.sum(m.reshape(KT // SUBLANES, SUBLANES, QB), axis=0)
        acc = lax.fori_loop(0, nkt, body, jnp.zeros((SUBLANES, QB), jnp.int32))
        return jnp.sum(acc, axis=0, keepdims=True)

    def bisect_body(i, thr):
        cand = thr + (jnp.int32(1) << (31 - i))
        cnt = count(lambda kk: kk >= cand)
        return jnp.where(cnt >= topk, cand, thr)

    thr = lax.fori_loop(0, 32, bisect_body, jnp.full((1, QB), INT_MIN, jnp.int32))
    n_gt = count(lambda kk: kk > thr)
    n_eq = count(lambda kk: kk == thr)
    need = topk - n_gt
    tie_rows = (n_eq != need) & (thr > NEG_INF_KEY)
    any_tie = jnp.max(tie_rows.astype(jnp.int32)) > 0
    thr_adm = jnp.maximum(thr, NEG_INF_KEY + 1)

    def bias_body(kt, carry):
        k0 = pl.multiple_of(kt * KT, KT)
        kk = keys_ref[pl.ds(k0, KT), :]
        bias_ref[pl.ds(k0, KT), :] = jnp.where(kk >= thr_adm, 0.0, NEG_BIG)
        return carry

    lax.fori_loop(0, nkt, bias_body, 0)

    @pl.when(any_tie)
    def _():
        tri = _bf((lax.broadcasted_iota(jnp.int32, (KT, KT), 1)
                   <= lax.broadcasted_iota(jnp.int32, (KT, KT), 0)).astype(jnp.float32))
        need_f = need.astype(jnp.float32)

        def tie_body(kt, seen):
            k0 = pl.multiple_of(kt * KT, KT)
            kk = keys_ref[pl.ds(k0, KT), :]
            eq = kk == thr
            eq_f = eq.astype(jnp.float32)
            rank = _dot(tri, _bf(eq_f)) + seen
            sel = ((kk > thr) | (eq & (rank <= need_f))) & (kk > NEG_INF_KEY)
            bias_ref[pl.ds(k0, KT), :] = jnp.where(sel, 0.0, NEG_BIG)
            return seen + jnp.sum(eq_f, axis=0, keepdims=True)

        lax.fori_loop(0, nkt, tie_body, jnp.zeros((1, QB), jnp.float32))

    grp = N_HEADS // N_KV
    for g in range(N_KV):
        q_heads = [q_ref[0, :, (g * grp + hh) * HEAD_DIM:(g * grp + hh + 1) * HEAD_DIM]
                   for hh in range(grp)]
        gsl = slice(g * HEAD_DIM, (g + 1) * HEAD_DIM)

        def att_body(kt, carry):
            k0 = pl.multiple_of(kt * KT, KT)
            kb = _bf(k_ref[0, pl.ds(k0, KT), gsl])
            vt = vt_ref[0, kt, gsl, :]
            bias = bias_ref[pl.ds(k0, KT), :]
            out = []
            for hh in range(grp):
                m, l, acc = carry[hh]
                s = _dot_nt(kb, q_heads[hh]) + bias
                m_new = jnp.maximum(m, jnp.max(s, axis=0, keepdims=True))
                alpha = jnp.exp(m - m_new)
                p = jnp.exp(s - m_new)
                l = l * alpha + jnp.sum(p, axis=0, keepdims=True)
                acc = acc * alpha + _dot(vt, _bf(p))
                out.append((m_new, l, acc))
            return tuple(out)

        init = tuple((jnp.full((1, QB), NEG_BIG, jnp.float32),
                      jnp.zeros((1, QB), jnp.float32),
                      jnp.zeros((HEAD_DIM, QB), jnp.float32)) for _ in range(grp))
        res = lax.fori_loop(0, nkt, att_body, init)
        for hh in range(grp):
            _, l, acc = res[hh]
            h_abs = g * grp + hh
            o_ref[:, h_abs * HEAD_DIM:(h_abs + 1) * HEAD_DIM] = (acc / l).T

    a_out = _dot(_bf(o_ref[...] * sza_ref[0]), wpa_ref[...])
    merged = sga_ref[0] * a_out + gbb_ref[0]
    y_ref[0] = x_ref[0] + gate_ref[0] * _dot(_bf(merged), wout_ref[...])


def _attn_call(q, qi, wi, k_all, vt_all, ki2_all, sza, sga, gbb, x, gate, w_pa, w_out,
               *, pos0, l_real, kt_size, topk):
    b, t, d = x.shape
    lp = k_all.shape[1]
    tok = lambda w: pl.BlockSpec((1, QB, w), lambda i, j: (i, j, 0))
    per_b = lambda r, w: pl.BlockSpec((1, r, w), lambda i, j: (i, 0, 0))
    whole = pl.BlockSpec(memory_space=pltpu.VMEM)
    in_specs = [
        tok(ATT_W), tok(IDX_HEADS * IDX_DIM), tok(LANES),
        per_b(lp, KV_W),
        pl.BlockSpec((1, lp // kt_size, KV_W, kt_size), lambda i, j: (i, 0, 0, 0)),
        per_b(lp, LANES),
        tok(ATT_W), tok(D_MODEL), tok(D_MODEL), tok(d), per_b(1, d),
        whole, whole,
    ]
    return pl.pallas_call(
        functools.partial(_attn_kernel, pos0=pos0, l_real=l_real, kt_size=kt_size, topk=topk),
        grid=(b, t // QB),
        in_specs=in_specs,
        out_specs=tok(d),
        out_shape=jax.ShapeDtypeStruct((b, t, d), jnp.float32),
        scratch_shapes=[
            pltpu.VMEM((lp, QB), jnp.int32),
            pltpu.VMEM((lp, QB), jnp.float32),
            pltpu.VMEM((QB, ATT_W), jnp.float32),
        ],
        compiler_params=pltpu.CompilerParams(
            dimension_semantics=("arbitrary", "arbitrary"), vmem_limit_bytes=VMEM_LIMIT),
        name="dsa_attn",
    )(q, qi, wi, k_all, vt_all, ki2_all, sza, sga, gbb, x, gate, w_pa, w_out)


def _rope_tables(pos, width, period):
    rot = period // ROT_FRAC
    half = rot // 2
    inv = ROPE_THETA ** (-2.0 * jnp.arange(half, dtype=jnp.float32) / rot)
    ang = pos.astype(jnp.float32)[:, None] * inv[None, :]
    cos, sin = jnp.cos(ang), jnp.sin(ang)
    n = pos.shape[0]
    rest = period - rot
    one = jnp.ones((n, rest), jnp.float32)
    zero_h = jnp.zeros((n, half), jnp.float32)
    zero_r = jnp.zeros((n, rest), jnp.float32)
    c = jnp.concatenate([cos, cos, one], axis=1)
    s_up = jnp.concatenate([-sin, zero_h, zero_r], axis=1)
    s_dn = jnp.concatenate([zero_h, sin, zero_r], axis=1)
    rep = width // period
    return tuple(jnp.tile(a, (1, rep)) for a in (c, s_up, s_dn))


def _prep_w_in(w_in):
    o_ki = ATT_W + 2 * KV_W + ATT_W + IDX_HEADS * IDX_DIM
    o_wi = o_ki + IDX_DIM
    o_u = o_wi + IDX_HEADS
    ki = w_in[:, o_ki:o_wi]
    pad = jnp.zeros((w_in.shape[0], C_END - C_WI - IDX_HEADS), w_in.dtype)
    w_r = jnp.concatenate([w_in[:, :o_ki], w_in[:, o_u:], ki, ki, w_in[:, o_wi:o_u], pad], axis=1)
    return _bf(w_r)


def _layer(x, shift, scale, gate, pos, conv_state, past, weights, *, tm, t_real, kt_size):
    norm_g, qg, kg, w_r, w_pb, conv_w, w_pa, w_out = weights
    b, t, _ = x.shape
    tabs = _rope_tables(pos, LANES, HEAD_DIM) + _rope_tables(pos, LANES, IDX_DIM)
    conv_row = t_real - (t // tm - 1) * tm
    (q, k, v, vt, sza, qi, ki, ki2, wi, sga, gbb, conv_new) = _proj_call(
        x, shift, scale, norm_g, qg, kg, tabs, w_r, w_pb, conv_w, conv_state,
        tm=tm, conv_row=conv_row)
    if past is None:
        p_len = 0
        k_all, vt_all, ki2_all = k, vt, ki2
    else:
        past_k, past_v, past_ki = past
        p_len = past_k.shape[1]
        lp = -(-(p_len + t) // kt_size) * kt_size
        extra = lp - p_len - t
        k_all = jnp.concatenate(
            [past_k, k, jnp.zeros((b, extra, KV_W), jnp.float32)], axis=1)
        vt_all = jnp.concatenate(
            [_bf(jnp.swapaxes(past_v, 1, 2)), vt[:, 0], jnp.zeros((b, KV_W, extra), jnp.bfloat16)],
            axis=2)
        vt_all = jnp.swapaxes(vt_all.reshape(b, KV_W, lp // kt_size, kt_size), 1, 2)
        ki2_all = jnp.concatenate(
            [jnp.concatenate([past_ki, past_ki], axis=-1), ki2,
             jnp.zeros((b, extra, LANES), jnp.float32)], axis=1)
    l_real = p_len + t_real
    topk = min(TOPK_MAX, l_real // 4)
    y = _attn_call(q, qi, wi, k_all, vt_all, ki2_all, sza, sga, gbb, x, gate, w_pa, w_out,
                   pos0=p_len, l_real=l_real, kt_size=kt_size, topk=topk)
    return y, k, v, ki, conv_new


def kernel(x_prompt, x_sample, cache_k, cache_v, cache_idx_k, state_conv, c_prompt, c_sample,
           w_ada, b_ada, norm_g, w_in, q_norm_g, k_norm_g, conv_w, w_pa, w_pb, w_out):
    bp, seq, d = x_prompt.shape
    bs, dec_seq, _ = x_sample.shape
    past_len = cache_k.shape[1]

    mod = _mod_call(jnp.concatenate([c_prompt, c_sample], axis=0), w_ada, b_ada)
    shift, scale, gate = (m.reshape(bp + bs, 1, d) for m in jnp.split(mod, 3, axis=-1))

    weights = (norm_g.reshape(1, d), q_norm_g.reshape(1, HEAD_DIM), k_norm_g.reshape(1, HEAD_DIM),
               _prep_w_in(w_in), _bf(w_pb), conv_w, _bf(w_pa), _bf(w_out))

    yp, k_p, v_p, ki_p, conv_p = _layer(
        x_prompt, shift[:bp], scale[:bp], gate[:bp], jnp.arange(seq, dtype=jnp.int32),
        jnp.zeros((bp, CONV_W - 1, d), x_prompt.dtype), None, weights,
        tm=256, t_real=seq, kt_size=256)

    x_pad = jnp.pad(x_sample, ((0, 0), (0, QB - dec_seq), (0, 0)))
    past = (cache_k.reshape(bs, past_len, KV_W), cache_v.reshape(bs, past_len, KV_W), cache_idx_k)
    ys, k_s, v_s, ki_s, conv_s = _layer(
        x_pad, shift[bp:], scale[bp:], gate[bp:],
        past_len + jnp.arange(QB, dtype=jnp.int32), state_conv, past, weights,
        tm=QB, t_real=dec_seq, kt_size=256)

    kv4 = lambda a, n: a[:, :n].reshape(a.shape[0], n, N_KV, HEAD_DIM)
    return (yp, ys[:, :dec_seq],
            kv4(k_p, seq), kv4(v_p, seq), ki_p, conv_p,
            kv4(k_s, dec_seq), kv4(v_s, dec_seq), ki_s[:, :dec_seq], conv_s)
```

```python
import functools

import jax
import jax.numpy as jnp
from jax import lax
from jax.experimental import pallas as pl
from jax.experimental.pallas import tpu as pltpu

D_MODEL = 1024
CHUNK = 64
CHUNK_SHIFT = 6
N_HEADS = 8
N_KV = 2
HEAD_DIM = 128
ATT_W = N_HEADS * HEAD_DIM
KV_W = N_KV * HEAD_DIM
ROT_FRAC = 4
ROPE_THETA = 500000.0
IDX_HEADS = 8
IDX_DIM = 64
TOPK_MAX = 256
CONV_W = 3
EPS = 1e-6

LANES = 128
SUBLANES = 8
QB = 128
VMEM_LIMIT = 56 * 1024 * 1024

C_Q, C_K, C_V, C_ZA, C_QI = 0, 1024, 1280, 1536, 2560
C_U, C_BG, C_CG, C_ZB, C_GA, C_GB = 3072, 4096, 5120, 6144, 7168, 8192
C_KI, C_WI, C_END = 9216, 9344, 9472

NEG_BIG = -1e30
NEG_INF_KEY = -2139095041
INT_MIN = -2147483648

_NT = (((1,), (1,)), ((), ()))


def _bf(x):
    return x.astype(jnp.bfloat16)


def _dot(a, b):
    return jnp.dot(a, b, preferred_element_type=jnp.float32)


def _dot_nt(a, b):
    return lax.dot_general(a, b, _NT, preferred_element_type=jnp.float32)


def _silu(x):
    return x * jax.nn.sigmoid(x)


def _mod_kernel(c_ref, w_ref, b_ref, o_ref):
    c = c_ref[...]
    s = _silu(c)
    w = w_ref[...]
    s_hi = _bf(s)
    s_lo = _bf(s - s_hi.astype(jnp.float32))
    w_hi = _bf(w)
    w_lo = _bf(w - w_hi.astype(jnp.float32))
    acc = _dot(s_hi, w_hi) + (_dot(s_lo, w_hi) + _dot(s_hi, w_lo))
    o_ref[...] = acc + b_ref[...]


def _mod_call(c_all, w_ada, b_ada):
    nb, d = c_all.shape
    n = w_ada.shape[1]
    bn = 1024
    return pl.pallas_call(
        _mod_kernel,
        grid=(n // bn,),
        in_specs=[
            pl.BlockSpec((nb, d), lambda i: (0, 0)),
            pl.BlockSpec((d, bn), lambda i: (0, i)),
            pl.BlockSpec((1, bn), lambda i: (0, i)),
        ],
        out_specs=pl.BlockSpec((nb, bn), lambda i: (0, i)),
        out_shape=jax.ShapeDtypeStruct((nb, n), jnp.float32),
        compiler_params=pltpu.CompilerParams(dimension_semantics=("arbitrary",)),
        name="adaln_mod",
    )(c_all, w_ada, b_ada.reshape(1, n))


def _rope(x, cos, sin_up, sin_dn, half):
    up = pltpu.roll(x, LANES - half, 1)
    dn = pltpu.roll(x, half, 1)
    return x * cos + up * sin_up + dn * sin_dn


def _proj_kernel(x_ref, shift_ref, scale_ref, ng_ref, qg_ref, kg_ref,
                 rc_ref, rs1_ref, rs2_ref, ic_ref, is1_ref, is2_ref,
                 w_ref, wpb_ref, cw_ref, cs_ref,
                 q_ref, k_ref, v_ref, vt_ref, sza_ref, qi_ref, ki_ref, ki2_ref, wi_ref,
                 sga_ref, gbb_ref, conv_ref, carry_ref, *, tm, conv_row):
    t = pl.program_id(1)
    x = x_ref[0]
    ms = jnp.mean(x * x, axis=-1, keepdims=True)
    xn = x * lax.rsqrt(ms + EPS) * ng_ref[...]
    h = xn * (1.0 + scale_ref[0]) + shift_ref[0]
    hb = _bf(h)

    def proj(c0, c1):
        return _dot(hb, w_ref[:, c0:c1])

    rc, rs1, rs2 = rc_ref[...], rs1_ref[...], rs2_ref[...]
    ic, is1, is2 = ic_ref[...], is1_ref[...], is2_ref[...]
    rot_half = HEAD_DIM // ROT_FRAC // 2
    idx_half = IDX_DIM // ROT_FRAC // 2

    def head_norm_rope(xh, g):
        r = lax.rsqrt(jnp.mean(xh * xh, axis=-1, keepdims=True) + EPS)
        return _rope(xh * r * g, rc, rs1, rs2, rot_half)

    q = proj(C_Q, C_K)
    qg = qg_ref[...]
    for hh in range(N_HEADS):
        sl = slice(hh * HEAD_DIM, (hh + 1) * HEAD_DIM)
        q_ref[0, :, sl] = _bf(head_norm_rope(q[:, sl], qg) * (HEAD_DIM ** -0.5))

    k = proj(C_K, C_V)
    kg = kg_ref[...]
    for hh in range(N_KV):
        sl = slice(hh * HEAD_DIM, (hh + 1) * HEAD_DIM)
        k_ref[0, :, sl] = head_norm_rope(k[:, sl], kg)

    v = proj(C_V, C_ZA)
    v_ref[0] = v
    vt_ref[0, 0] = _bf(v.T)

    sza_ref[0] = _silu(proj(C_ZA, C_QI))

    qi = proj(C_QI, C_U)
    for p in range(IDX_HEADS * IDX_DIM // LANES):
        sl = slice(p * LANES, (p + 1) * LANES)
        qi_ref[0, :, sl] = _bf(_rope(qi[:, sl], ic, is1, is2, idx_half) * (IDX_DIM ** -0.5))

    ki2 = _rope(proj(C_KI, C_WI), ic, is1, is2, idx_half)
    ki2_ref[0] = ki2
    ki_ref[0] = ki2[:, :IDX_DIM]
    wi_ref[0] = proj(C_WI, C_END) * (IDX_HEADS ** -0.5)

    cv = proj(C_CG, C_ZB) * proj(C_U, C_BG)

    @pl.when(t == 0)
    def _():
        carry_ref[0:2, :] = cs_ref[0]

    c0 = carry_ref[0:1, :]
    c1 = carry_ref[1:2, :]
    row = lax.broadcasted_iota(jnp.int32, (tm, 1), 0)
    r1 = jnp.where(row == 0, c1, pltpu.roll(cv, 1, 0))
    r2 = jnp.where(row == 0, c0, jnp.where(row == 1, c1, pltpu.roll(cv, 2, 0)))
    cw = cw_ref[...]
    y_conv = cw[0:1, :] * r2 + cw[1:2, :] * r1 + cw[2:3, :] * cv
    carry_ref[0:2, :] = cv[tm - 2:tm, :]
    conv_ref[0] = cv[conv_row - 2:conv_row, :]

    tb = proj(C_BG, C_CG) * y_conv * _silu(proj(C_ZB, C_GA))
    b_out = _dot(_bf(tb), wpb_ref[...])
    gbb_ref[0] = jax.nn.sigmoid(proj(C_GB, C_KI)) * b_out
    sga_ref[0] = jax.nn.sigmoid(proj(C_GA, C_GB))


def _proj_call(x, shift, scale, norm_g, qg, kg, rope_tabs, w_r, w_pb, conv_w, conv_state,
               *, tm, conv_row):
    b, t, d = x.shape
    nt = t // tm
    tok = lambda w: pl.BlockSpec((1, tm, w), lambda i, j: (i, j, 0))
    per_b = lambda r, w: pl.BlockSpec((1, r, w), lambda i, j: (i, 0, 0))
    const = lambda r, w: pl.BlockSpec((r, w), lambda i, j: (0, 0))
    tab = pl.BlockSpec((tm, LANES), lambda i, j: (j, 0))
    whole = pl.BlockSpec(memory_space=pltpu.VMEM)
    f32, bf16 = jnp.float32, jnp.bfloat16
    out_shape = (
        jax.ShapeDtypeStruct((b, t, ATT_W), bf16),
        jax.ShapeDtypeStruct((b, t, KV_W), f32),
        jax.ShapeDtypeStruct((b, t, KV_W), f32),
        jax.ShapeDtypeStruct((b, nt, KV_W, tm), bf16),
        jax.ShapeDtypeStruct((b, t, ATT_W), f32),
        jax.ShapeDtypeStruct((b, t, IDX_HEADS * IDX_DIM), bf16),
        jax.ShapeDtypeStruct((b, t, IDX_DIM), f32),
        jax.ShapeDtypeStruct((b, t, LANES), f32),
        jax.ShapeDtypeStruct((b, t, LANES), f32),
        jax.ShapeDtypeStruct((b, t, D_MODEL), f32),
        jax.ShapeDtypeStruct((b, t, D_MODEL), f32),
        jax.ShapeDtypeStruct((b, CONV_W - 1, D_MODEL), f32),
    )
    out_specs = (
        tok(ATT_W), tok(KV_W), tok(KV_W),
        pl.BlockSpec((1, 1, KV_W, tm), lambda i, j: (i, j, 0, 0)),
        tok(ATT_W), tok(IDX_HEADS * IDX_DIM), tok(IDX_DIM), tok(LANES), tok(LANES),
        tok(D_MODEL), tok(D_MODEL), per_b(CONV_W - 1, D_MODEL),
    )
    in_specs = [
        tok(d), per_b(1, d), per_b(1, d), const(1, d), const(1, LANES), const(1, LANES),
        tab, tab, tab, tab, tab, tab,
        whole, whole, const(CONV_W, d), per_b(CONV_W - 1, d),
    ]
    return pl.pallas_call(
        functools.partial(_proj_kernel, tm=tm, conv_row=conv_row),
        grid=(b, nt),
        in_specs=in_specs,
        out_specs=out_specs,
        out_shape=out_shape,
        scratch_shapes=[pltpu.VMEM((SUBLANES, d), jnp.float32)],
        compiler_params=pltpu.CompilerParams(
            dimension_semantics=("arbitrary", "arbitrary"), vmem_limit_bytes=VMEM_LIMIT),
        name="in_proj",
    )(x, shift, scale, norm_g, qg, kg, *rope_tabs, w_r, w_pb, conv_w, conv_state)


def _sort_key(x):
    bits = lax.bitcast_convert_type(x, jnp.int32)
    return bits ^ ((bits >> 31) & 0x7FFFFFFF)


def _attn_kernel(q_ref, qi_ref, wi_ref, k_ref, vt_ref, ki2_ref,
                 sza_ref, sga_ref, gbb_ref, x_ref, gate_ref, wpa_ref, wout_ref,
                 y_ref, keys_ref, bias_ref, s_ref, acc_ref, o_ref, *, pos0, l_real, kt_size, topk):
    KT = kt_size
    vt_w = vt_ref.shape[3]
    vt_per_kt = KT // vt_w
    j = pl.program_id(1)
    q0 = pos0 + j * QB
    kmax = jnp.minimum(l_real, (((q0 + QB - 1) >> CHUNK_SHIFT) + 1) * CHUNK)
    nkt = (kmax + KT - 1) // KT

    lane = lax.broadcasted_iota(jnp.int32, (1, QB), 1)
    q_chunk = (q0 + lane) >> CHUNK_SHIFT
    w_t = wi_ref[0].T
    qi = qi_ref[0]
    lane_kt = lax.broadcasted_iota(jnp.int32, (KT, LANES), 1)
    sub_kt = lax.broadcasted_iota(jnp.int32, (KT, 1), 0)

    def score_body(kt, carry):
        k0 = pl.multiple_of(kt * KT, KT)
        ki2 = ki2_ref[0, pl.ds(k0, KT), :]
        ki_lo = _bf(jnp.where(lane_kt < IDX_DIM, ki2, 0.0))
        ki_hi = _bf(jnp.where(lane_kt >= IDX_DIM, ki2, 0.0))
        acc = jnp.zeros((KT, QB), jnp.float32)
        for p in range(IDX_HEADS // 2):
            slab = qi[:, p * LANES:(p + 1) * LANES]
            acc = acc + w_t[2 * p:2 * p + 1, :] * jnp.maximum(_dot_nt(ki_lo, slab), 0.0)
            acc = acc + w_t[2 * p + 1:2 * p + 2, :] * jnp.maximum(_dot_nt(ki_hi, slab), 0.0)
        kpos = k0 + sub_kt
        adm = ((kpos >> CHUNK_SHIFT) <= q_chunk) & (kpos < l_real)
        keys_ref[pl.ds(k0, KT), :] = _sort_key(jnp.where(adm, acc, -jnp.inf))
        return carry

    lax.fori_loop(0, nkt, score_body, 0)

    def count(pred):
        def body(kt, acc):
            k0 = pl.multiple_of(kt * KT, KT)
            m = pred(keys_ref[pl.ds(k0, KT), :]).astype(jnp.int32)
            return acc + jnp.sum(m.reshape(KT // SUBLANES, SUBLANES, QB), axis=0)
        acc = lax.fori_loop(0, nkt, body, jnp.zeros((SUBLANES, QB), jnp.int32))
        return jnp.sum(acc, axis=0, keepdims=True)

    def bisect_body(i, thr):
        cand = thr + (jnp.int32(1) << (31 - i))
        cnt = count(lambda kk: kk >= cand)
        return jnp.where(cnt >= topk, cand, thr)

    thr = lax.fori_loop(0, 32, bisect_body, jnp.full((1, QB), INT_MIN, jnp.int32))
    n_gt = count(lambda kk: kk > thr)
    n_eq = count(lambda kk: kk == thr)
    need = topk - n_gt
    tie_rows = (n_eq != need) & (thr > NEG_INF_KEY)
    any_tie = jnp.max(tie_rows.astype(jnp.int32)) > 0
    thr_adm = jnp.maximum(thr, NEG_INF_KEY + 1)

    def bias_body(kt, carry):
        k0 = pl.multiple_of(kt * KT, KT)
        kk = keys_ref[pl.ds(k0, KT), :]
        bias_ref[pl.ds(k0, KT), :] = jnp.where(kk >= thr_adm, 0.0, NEG_BIG)
        return carry

    lax.fori_loop(0, nkt, bias_body, 0)

    @pl.when(any_tie)
    def _():
        tri = _bf((lax.broadcasted_iota(jnp.int32, (KT, KT), 1)
                   <= lax.broadcasted_iota(jnp.int32, (KT, KT), 0)).astype(jnp.float32))
        need_f = need.astype(jnp.float32)

        def tie_body(kt, seen):
            k0 = pl.multiple_of(kt * KT, KT)
            kk = keys_ref[pl.ds(k0, KT), :]
            eq = kk == thr
            eq_f = eq.astype(jnp.float32)
            rank = _dot(tri, _bf(eq_f)) + seen
            sel = ((kk > thr) | (eq & (rank <= need_f))) & (kk > NEG_INF_KEY)
            bias_ref[pl.ds(k0, KT), :] = jnp.where(sel, 0.0, NEG_BIG)
            return seen + jnp.sum(eq_f, axis=0, keepdims=True)

        lax.fori_loop(0, nkt, tie_body, jnp.zeros((1, QB), jnp.float32))

    grp = N_HEADS // N_KV
    gw = grp * QB
    for g in range(N_KV):
        q_g = jnp.concatenate(
            [q_ref[0, :, (g * grp + hh) * HEAD_DIM:(g * grp + hh + 1) * HEAD_DIM]
             for hh in range(grp)], axis=0)
        gsl = slice(g * HEAD_DIM, (g + 1) * HEAD_DIM)

        def s_body(kt, m8):
            k0 = pl.multiple_of(kt * KT, KT)
            kb = _bf(k_ref[0, pl.ds(k0, KT), gsl])
            bias = bias_ref[pl.ds(k0, KT), :]
            s = _dot_nt(kb, q_g)
            out = []
            for hh in range(grp):
                hsl = slice(hh * QB, (hh + 1) * QB)
                sh = s[:, hsl] + bias
                s_ref[pl.ds(k0, KT), hsl] = sh
                out.append(jnp.maximum(
                    m8[hh], jnp.max(sh.reshape(KT // SUBLANES, SUBLANES, QB), axis=0)))
            return tuple(out)

        m8 = lax.fori_loop(0, nkt, s_body,
                           tuple(jnp.full((SUBLANES, QB), NEG_BIG, jnp.float32)
                                 for _ in range(grp)))
        m_all = jnp.concatenate([jnp.max(m, axis=0, keepdims=True) for m in m8], axis=1)

        acc_ref[...] = jnp.zeros((HEAD_DIM, gw), jnp.float32)

        def pv_body(kt, l8):
            k0 = pl.multiple_of(kt * KT, KT)
            p = jnp.exp(s_ref[pl.ds(k0, KT), :] - m_all)
            pb = _bf(p)
            acc_ref[...] += sum(
                _dot(vt_ref[0, kt * vt_per_kt + r, gsl, :], pb[r * vt_w:(r + 1) * vt_w, :])
                for r in range(vt_per_kt))
            return l8 + jnp.sum(p.reshape(KT // SUBLANES, SUBLANES, gw), axis=0)

        l8 = lax.fori_loop(0, nkt, pv_body, jnp.zeros((SUBLANES, gw), jnp.float32))
        o_t = acc_ref[...] * (1.0 / jnp.sum(l8, axis=0, keepdims=True))
        for hh in range(grp):
            h_abs = g * grp + hh
            o_ref[:, h_abs * HEAD_DIM:(h_abs + 1) * HEAD_DIM] = o_t[:, hh * QB:(hh + 1) * QB].T

    a_out = _dot(_bf(o_ref[...] * sza_ref[0]), wpa_ref[...])
    merged = sga_ref[0] * a_out + gbb_ref[0]
    y_ref[0] = x_ref[0] + gate_ref[0] * _dot(_bf(merged), wout_ref[...])


def _attn_call(q, qi, wi, k_all, vt_all, ki2_all, sza, sga, gbb, x, gate, w_pa, w_out,
               *, pos0, l_real, kt_size, topk):
    b, t, d = x.shape
    lp = k_all.shape[1]
    tok = lambda w: pl.BlockSpec((1, QB, w), lambda i, j: (i, j, 0))
    per_b = lambda r, w: pl.BlockSpec((1, r, w), lambda i, j: (i, 0, 0))
    whole = pl.BlockSpec(memory_space=pltpu.VMEM)
    in_specs = [
        tok(ATT_W), tok(IDX_HEADS * IDX_DIM), tok(LANES),
        per_b(lp, KV_W),
        pl.BlockSpec((1,) + vt_all.shape[1:], lambda i, j: (i, 0, 0, 0)),
        per_b(lp, LANES),
        tok(ATT_W), tok(D_MODEL), tok(D_MODEL), tok(d), per_b(1, d),
        whole, whole,
    ]
    return pl.pallas_call(
        functools.partial(_attn_kernel, pos0=pos0, l_real=l_real, kt_size=kt_size, topk=topk),
        grid=(b, t // QB),
        in_specs=in_specs,
        out_specs=tok(d),
        out_shape=jax.ShapeDtypeStruct((b, t, d), jnp.float32),
        scratch_shapes=[
            pltpu.VMEM((lp, QB), jnp.int32),
            pltpu.VMEM((lp, QB), jnp.float32),
            pltpu.VMEM((lp, ATT_W // N_KV), jnp.float32),
            pltpu.VMEM((HEAD_DIM, ATT_W // N_KV), jnp.float32),
            pltpu.VMEM((QB, ATT_W), jnp.float32),
        ],
        compiler_params=pltpu.CompilerParams(
            dimension_semantics=("arbitrary", "arbitrary"), vmem_limit_bytes=VMEM_LIMIT),
        name="dsa_attn",
    )(q, qi, wi, k_all, vt_all, ki2_all, sza, sga, gbb, x, gate, w_pa, w_out)


def _rope_tables(pos, width, period):
    rot = period // ROT_FRAC
    half = rot // 2
    inv = ROPE_THETA ** (-2.0 * jnp.arange(half, dtype=jnp.float32) / rot)
    ang = pos.astype(jnp.float32)[:, None] * inv[None, :]
    cos, sin = jnp.cos(ang), jnp.sin(ang)
    n = pos.shape[0]
    rest = period - rot
    one = jnp.ones((n, rest), jnp.float32)
    zero_h = jnp.zeros((n, half), jnp.float32)
    zero_r = jnp.zeros((n, rest), jnp.float32)
    c = jnp.concatenate([cos, cos, one], axis=1)
    s_up = jnp.concatenate([-sin, zero_h, zero_r], axis=1)
    s_dn = jnp.concatenate([zero_h, sin, zero_r], axis=1)
    rep = width // period
    return tuple(jnp.tile(a, (1, rep)) for a in (c, s_up, s_dn))


def _prep_w_in(w_in):
    o_ki = ATT_W + 2 * KV_W + ATT_W + IDX_HEADS * IDX_DIM
    o_wi = o_ki + IDX_DIM
    o_u = o_wi + IDX_HEADS
    ki = w_in[:, o_ki:o_wi]
    pad = jnp.zeros((w_in.shape[0], C_END - C_WI - IDX_HEADS), w_in.dtype)
    w_r = jnp.concatenate([w_in[:, :o_ki], w_in[:, o_u:], ki, ki, w_in[:, o_wi:o_u], pad], axis=1)
    return _bf(w_r)


def _layer(x, shift, scale, gate, pos, conv_state, past, weights, *, tm, t_real, kt_size):
    norm_g, qg, kg, w_r, w_pb, conv_w, w_pa, w_out = weights
    b, t, _ = x.shape
    tabs = _rope_tables(pos, LANES, HEAD_DIM) + _rope_tables(pos, LANES, IDX_DIM)
    conv_row = t_real - (t // tm - 1) * tm
    (q, k, v, vt, sza, qi, ki, ki2, wi, sga, gbb, conv_new) = _proj_call(
        x, shift, scale, norm_g, qg, kg, tabs, w_r, w_pb, conv_w, conv_state,
        tm=tm, conv_row=conv_row)
    if past is None:
        p_len = 0
        k_all, vt_all, ki2_all = k, vt, ki2
    else:
        past_k, past_v, past_ki = past
        p_len = past_k.shape[1]
        lp = -(-(p_len + t) // kt_size) * kt_size
        extra = lp - p_len - t
        k_all = jnp.concatenate(
            [past_k, k, jnp.zeros((b, extra, KV_W), jnp.float32)], axis=1)
        vt_all = jnp.concatenate(
            [_bf(jnp.swapaxes(past_v, 1, 2)), vt[:, 0], jnp.zeros((b, KV_W, extra), jnp.bfloat16)],
            axis=2)
        vt_all = jnp.swapaxes(vt_all.reshape(b, KV_W, lp // kt_size, kt_size), 1, 2)
        ki2_all = jnp.concatenate(
            [jnp.concatenate([past_ki, past_ki], axis=-1), ki2,
             jnp.zeros((b, extra, LANES), jnp.float32)], axis=1)
    l_real = p_len + t_real
    topk = min(TOPK_MAX, l_real // 4)
    y = _attn_call(q, qi, wi, k_all, vt_all, ki2_all, sza, sga, gbb, x, gate, w_pa, w_out,
                   pos0=p_len, l_real=l_real, kt_size=kt_size, topk=topk)
    return y, k, v, ki, conv_new


def kernel(x_prompt, x_sample, cache_k, cache_v, cache_idx_k, state_conv, c_prompt, c_sample,
           w_ada, b_ada, norm_g, w_in, q_norm_g, k_norm_g, conv_w, w_pa, w_pb, w_out):
    bp, seq, d = x_prompt.shape
    bs, dec_seq, _ = x_sample.shape
    past_len = cache_k.shape[1]

    mod = _mod_call(jnp.concatenate([c_prompt, c_sample], axis=0), w_ada, b_ada)
    shift, scale, gate = (m.reshape(bp + bs, 1, d) for m in jnp.split(mod, 3, axis=-1))

    weights = (norm_g.reshape(1, d), q_norm_g.reshape(1, HEAD_DIM), k_norm_g.reshape(1, HEAD_DIM),
               _prep_w_in(w_in), _bf(w_pb), conv_w, _bf(w_pa), _bf(w_out))

    yp, k_p, v_p, ki_p, conv_p = _layer(
        x_prompt, shift[:bp], scale[:bp], gate[:bp], jnp.arange(seq, dtype=jnp.int32),
        jnp.zeros((bp, CONV_W - 1, d), x_prompt.dtype), None, weights,
        tm=256, t_real=seq, kt_size=512)

    x_pad = jnp.pad(x_sample, ((0, 0), (0, QB - dec_seq), (0, 0)))
    past = (cache_k.reshape(bs, past_len, KV_W), cache_v.reshape(bs, past_len, KV_W), cache_idx_k)
    ys, k_s, v_s, ki_s, conv_s = _layer(
        x_pad, shift[bp:], scale[bp:], gate[bp:],
        past_len + jnp.arange(QB, dtype=jnp.int32), state_conv, past, weights,
        tm=QB, t_real=dec_seq, kt_size=384)

    kv4 = lambda a, n: a[:, :n].reshape(a.shape[0], n, N_KV, HEAD_DIM)
    return (yp, ys[:, :dec_seq],
            kv4(k_p, seq), kv4(v_p, seq), ki_p, conv_p,
            kv4(k_s, dec_seq), kv4(v_s, dec_seq), ki_s[:, :dec_seq], conv_s)
```

```python
import functools

import jax
import jax.numpy as jnp
from jax import lax
from jax.experimental import pallas as pl
from jax.experimental.pallas import tpu as pltpu

D_MODEL = 1024
CHUNK = 64
CHUNK_SHIFT = 6
N_HEADS = 8
N_KV = 2
HEAD_DIM = 128
ATT_W = N_HEADS * HEAD_DIM
KV_W = N_KV * HEAD_DIM
ROT_FRAC = 4
ROPE_THETA = 500000.0
IDX_HEADS = 8
IDX_DIM = 64
TOPK_MAX = 256
CONV_W = 3
EPS = 1e-6

LANES = 128
SUBLANES = 8
QB = 256
VMEM_LIMIT = 56 * 1024 * 1024

C_Q, C_K, C_V, C_ZA, C_QI = 0, 1024, 1280, 1536, 2560
C_U, C_BG, C_CG, C_ZB, C_GA, C_GB = 3072, 4096, 5120, 6144, 7168, 8192
C_KI, C_WI, C_END = 9216, 9344, 9472

NEG_BIG = -1e30
NEG_INF_KEY = -2139095041
SEARCH_VALUE_IT = 16
SEARCH_FIRST_IT = 18
SEARCH_MAX_IT = SEARCH_VALUE_IT + 34

_NT = (((1,), (1,)), ((), ()))


def _bf(x):
    return x.astype(jnp.bfloat16)


def _dot(a, b):
    return jnp.dot(a, b, preferred_element_type=jnp.float32)


def _dot_nt(a, b):
    return lax.dot_general(a, b, _NT, preferred_element_type=jnp.float32)


def _silu(x):
    return x * jax.nn.sigmoid(x)


def _mod_kernel(c_ref, w_ref, b_ref, o_ref):
    c = c_ref[...]
    s = _silu(c)
    w = w_ref[...]
    s_hi = _bf(s)
    s_lo = _bf(s - s_hi.astype(jnp.float32))
    w_hi = _bf(w)
    w_lo = _bf(w - w_hi.astype(jnp.float32))
    acc = _dot(s_hi, w_hi) + (_dot(s_lo, w_hi) + _dot(s_hi, w_lo))
    o_ref[...] = acc + b_ref[...]


def _mod_call(c_all, w_ada, b_ada):
    nb, d = c_all.shape
    n = w_ada.shape[1]
    bn = 1024
    return pl.pallas_call(
        _mod_kernel,
        grid=(n // bn,),
        in_specs=[
            pl.BlockSpec((nb, d), lambda i: (0, 0)),
            pl.BlockSpec((d, bn), lambda i: (0, i)),
            pl.BlockSpec((1, bn), lambda i: (0, i)),
        ],
        out_specs=pl.BlockSpec((nb, bn), lambda i: (0, i)),
        out_shape=jax.ShapeDtypeStruct((nb, n), jnp.float32),
        compiler_params=pltpu.CompilerParams(dimension_semantics=("arbitrary",)),
        name="adaln_mod",
    )(c_all, w_ada, b_ada.reshape(1, n))


def _rope(x, cos, sin_up, sin_dn, half):
    up = pltpu.roll(x, LANES - half, 1)
    dn = pltpu.roll(x, half, 1)
    return x * cos + up * sin_up + dn * sin_dn


def _proj_kernel(x_ref, shift_ref, scale_ref, ng_ref, qg_ref, kg_ref,
                 rc_ref, rs1_ref, rs2_ref, ic_ref, is1_ref, is2_ref,
                 w_ref, wpb_ref, cw_ref, cs_ref,
                 q_ref, k_ref, v_ref, vt_ref, sza_ref, qi_ref, ki_ref, ki2_ref, wi_ref,
                 sga_ref, gbb_ref, conv_ref, carry_ref, *, tm, conv_row):
    t = pl.program_id(1)
    x = x_ref[0]
    ms = jnp.mean(x * x, axis=-1, keepdims=True)
    xn = x * lax.rsqrt(ms + EPS) * ng_ref[...]
    h = xn * (1.0 + scale_ref[0]) + shift_ref[0]
    hb = _bf(h)

    def proj(c0, c1):
        return _dot(hb, w_ref[:, c0:c1])

    rc, rs1, rs2 = rc_ref[...], rs1_ref[...], rs2_ref[...]
    ic, is1, is2 = ic_ref[...], is1_ref[...], is2_ref[...]
    rot_half = HEAD_DIM // ROT_FRAC // 2
    idx_half = IDX_DIM // ROT_FRAC // 2

    def head_norm_rope(xh, g):
        r = lax.rsqrt(jnp.mean(xh * xh, axis=-1, keepdims=True) + EPS)
        return _rope(xh * r * g, rc, rs1, rs2, rot_half)

    q = proj(C_Q, C_K)
    qg = qg_ref[...]
    for hh in range(N_HEADS):
        sl = slice(hh * HEAD_DIM, (hh + 1) * HEAD_DIM)
        q_ref[0, :, sl] = _bf(head_norm_rope(q[:, sl], qg) * (HEAD_DIM ** -0.5))

    k = proj(C_K, C_V)
    kg = kg_ref[...]
    for hh in range(N_KV):
        sl = slice(hh * HEAD_DIM, (hh + 1) * HEAD_DIM)
        k_ref[0, :, sl] = head_norm_rope(k[:, sl], kg)

    v = proj(C_V, C_ZA)
    v_ref[0] = v
    vt_ref[0, 0] = _bf(v.T)

    sza_ref[0] = _silu(proj(C_ZA, C_QI))

    qi = proj(C_QI, C_U)
    for p in range(IDX_HEADS * IDX_DIM // LANES):
        sl = slice(p * LANES, (p + 1) * LANES)
        qi_ref[0, :, sl] = _bf(_rope(qi[:, sl], ic, is1, is2, idx_half) * (IDX_DIM ** -0.5))

    ki2 = _rope(proj(C_KI, C_WI), ic, is1, is2, idx_half)
    ki2_ref[0] = ki2
    ki_ref[0] = ki2[:, :IDX_DIM]
    wi_ref[0] = proj(C_WI, C_END) * (IDX_HEADS ** -0.5)

    cv = proj(C_CG, C_ZB) * proj(C_U, C_BG)

    @pl.when(t == 0)
    def _():
        carry_ref[0:2, :] = cs_ref[0]

    c0 = carry_ref[0:1, :]
    c1 = carry_ref[1:2, :]
    row = lax.broadcasted_iota(jnp.int32, (tm, 1), 0)
    r1 = jnp.where(row == 0, c1, pltpu.roll(cv, 1, 0))
    r2 = jnp.where(row == 0, c0, jnp.where(row == 1, c1, pltpu.roll(cv, 2, 0)))
    cw = cw_ref[...]
    y_conv = cw[0:1, :] * r2 + cw[1:2, :] * r1 + cw[2:3, :] * cv
    carry_ref[0:2, :] = cv[tm - 2:tm, :]
    conv_ref[0] = cv[conv_row - 2:conv_row, :]

    tb = proj(C_BG, C_CG) * y_conv * _silu(proj(C_ZB, C_GA))
    b_out = _dot(_bf(tb), wpb_ref[...])
    gbb_ref[0] = jax.nn.sigmoid(proj(C_GB, C_KI)) * b_out
    sga_ref[0] = jax.nn.sigmoid(proj(C_GA, C_GB))


def _proj_call(x, shift, scale, norm_g, qg, kg, rope_tabs, w_r, w_pb, conv_w, conv_state,
               *, tm, conv_row):
    b, t, d = x.shape
    nt = t // tm
    tok = lambda w: pl.BlockSpec((1, tm, w), lambda i, j: (i, j, 0))
    per_b = lambda r, w: pl.BlockSpec((1, r, w), lambda i, j: (i, 0, 0))
    const = lambda r, w: pl.BlockSpec((r, w), lambda i, j: (0, 0))
    tab = pl.BlockSpec((tm, LANES), lambda i, j: (j, 0))
    whole = pl.BlockSpec(memory_space=pltpu.VMEM)
    f32, bf16 = jnp.float32, jnp.bfloat16
    out_shape = (
        jax.ShapeDtypeStruct((b, t, ATT_W), bf16),
        jax.ShapeDtypeStruct((b, t, KV_W), f32),
        jax.ShapeDtypeStruct((b, t, KV_W), f32),
        jax.ShapeDtypeStruct((b, nt, KV_W, tm), bf16),
        jax.ShapeDtypeStruct((b, t, ATT_W), f32),
        jax.ShapeDtypeStruct((b, t, IDX_HEADS * IDX_DIM), bf16),
        jax.ShapeDtypeStruct((b, t, IDX_DIM), f32),
        jax.ShapeDtypeStruct((b, t, LANES), f32),
        jax.ShapeDtypeStruct((b, t, LANES), f32),
        jax.ShapeDtypeStruct((b, t, D_MODEL), f32),
        jax.ShapeDtypeStruct((b, t, D_MODEL), f32),
        jax.ShapeDtypeStruct((b, CONV_W - 1, D_MODEL), f32),
    )
    out_specs = (
        tok(ATT_W), tok(KV_W), tok(KV_W),
        pl.BlockSpec((1, 1, KV_W, tm), lambda i, j: (i, j, 0, 0)),
        tok(ATT_W), tok(IDX_HEADS * IDX_DIM), tok(IDX_DIM), tok(LANES), tok(LANES),
        tok(D_MODEL), tok(D_MODEL), per_b(CONV_W - 1, D_MODEL),
    )
    in_specs = [
        tok(d), per_b(1, d), per_b(1, d), const(1, d), const(1, LANES), const(1, LANES),
        tab, tab, tab, tab, tab, tab,
        whole, whole, const(CONV_W, d), per_b(CONV_W - 1, d),
    ]
    return pl.pallas_call(
        functools.partial(_proj_kernel, tm=tm, conv_row=conv_row),
        grid=(b, nt),
        in_specs=in_specs,
        out_specs=out_specs,
        out_shape=out_shape,
        scratch_shapes=[pltpu.VMEM((SUBLANES, d), jnp.float32)],
        compiler_params=pltpu.CompilerParams(
            dimension_semantics=("arbitrary", "arbitrary"), vmem_limit_bytes=VMEM_LIMIT),
        name="in_proj",
    )(x, shift, scale, norm_g, qg, kg, *rope_tabs, w_r, w_pb, conv_w, conv_state)


def _sort_key(x):
    bits = lax.bitcast_convert_type(x, jnp.int32)
    return bits ^ ((bits >> 31) & 0x7FFFFFFF)


def _sort_key_inv(key):
    return lax.bitcast_convert_type(key ^ ((key >> 31) & 0x7FFFFFFF), jnp.float32)


def _attn_kernel(q_ref, qi_ref, wi_ref, k_ref, vt_ref, ki2_ref,
                 sza_ref, sga_ref, gbb_ref, x_ref, gate_ref, wpa_ref, wout_ref,
                 y_ref, keys_ref, bias_ref, s_ref, acc_ref, o_ref, *, pos0, l_real, kt_size, topk):
    KT = kt_size
    vt_w = vt_ref.shape[3]
    vt_per_kt = KT // vt_w
    j = pl.program_id(1)
    q0 = pos0 + j * QB
    kmax = jnp.minimum(l_real, (((q0 + QB - 1) >> CHUNK_SHIFT) + 1) * CHUNK)
    nkt = (kmax + KT - 1) // KT

    lane = lax.broadcasted_iota(jnp.int32, (1, QB), 1)
    q_chunk = (q0 + lane) >> CHUNK_SHIFT
    w_t = wi_ref[0].T
    qi = qi_ref[0]
    lane_kt = lax.broadcasted_iota(jnp.int32, (KT, LANES), 1)
    sub_kt = lax.broadcasted_iota(jnp.int32, (KT, 1), 0)

    def score_body(kt, carry):
        smax8, smin8 = carry
        k0 = pl.multiple_of(kt * KT, KT)
        ki2 = ki2_ref[0, pl.ds(k0, KT), :]
        ki_lo = _bf(jnp.where(lane_kt < IDX_DIM, ki2, 0.0))
        ki_hi = _bf(jnp.where(lane_kt >= IDX_DIM, ki2, 0.0))
        acc = jnp.zeros((KT, QB), jnp.float32)
        for p in range(IDX_HEADS // 2):
            slab = qi[:, p * LANES:(p + 1) * LANES]
            acc = acc + w_t[2 * p:2 * p + 1, :] * jnp.maximum(_dot_nt(ki_lo, slab), 0.0)
            acc = acc + w_t[2 * p + 1:2 * p + 2, :] * jnp.maximum(_dot_nt(ki_hi, slab), 0.0)
        kpos = k0 + sub_kt
        adm = ((kpos >> CHUNK_SHIFT) <= q_chunk) & (kpos < l_real)
        keys_ref[pl.ds(k0, KT), :] = _sort_key(jnp.where(adm, acc, -jnp.inf))
        acc3 = acc.reshape(KT // SUBLANES, SUBLANES, QB)
        return (jnp.maximum(smax8, jnp.max(acc3, axis=0)),
                jnp.minimum(smin8, jnp.min(acc3, axis=0)))

    smax8, smin8 = lax.fori_loop(
        0, nkt, score_body,
        (jnp.full((SUBLANES, QB), -jnp.inf, jnp.float32),
         jnp.full((SUBLANES, QB), jnp.inf, jnp.float32)))

    def count_ge(cand):
        def body(kt, acc):
            k0 = pl.multiple_of(kt * KT, KT)
            m = (keys_ref[pl.ds(k0, KT), :] >= cand).astype(jnp.int32)
            return acc + jnp.sum(m.reshape(KT // SUBLANES, SUBLANES, QB), axis=0)
        acc = lax.fori_loop(0, nkt, body, jnp.zeros((SUBLANES, QB), jnp.int32))
        return jnp.sum(acc, axis=0, keepdims=True)

    def active_rows(lo, hi, clo):
        return (clo > topk) & (hi != lo + 1)

    def any_active(st):
        lo, hi, clo, _ = st
        return jnp.max(active_rows(lo, hi, clo).astype(jnp.float32)) > 0.0

    def search_step(it, st):
        lo, hi, clo, chi = st
        act = active_rows(lo, hi, clo)
        key_mid = lo + lax.shift_right_logical(hi - lo, 1)
        val_mid = _sort_key(_sort_key_inv(lo) * 0.5 + _sort_key_inv(hi) * 0.5)
        guess = jnp.where(it == 0, 0, jnp.where(it == 1, 1,
                          jnp.where(it < SEARCH_VALUE_IT, val_mid, key_mid)))
        cand = jnp.where((guess > lo) & (guess < hi), guess, key_mid)
        cnt = count_ge(cand)
        ge = cnt >= topk
        up_lo = act & ge
        up_hi = act & jnp.logical_not(ge)
        return (jnp.where(up_lo, cand, lo), jnp.where(up_hi, cand, hi),
                jnp.where(up_lo, cnt, clo), jnp.where(up_hi, cnt, chi))

    n_adm = jnp.minimum(l_real, (q_chunk + 1) * CHUNK)
    lo0 = _sort_key(jnp.min(smin8, axis=0, keepdims=True))
    hi0 = _sort_key(jnp.max(smax8, axis=0, keepdims=True)) + 1
    st = (lo0, hi0, n_adm, jnp.zeros((1, QB), jnp.int32))
    n_first = jnp.where(any_active(st), SEARCH_FIRST_IT, 0)
    st = lax.fori_loop(0, n_first, search_step, st)

    def search_cond(c):
        return (c[0] < SEARCH_MAX_IT) & any_active(c[1])

    def search_body(c):
        it, st = c
        return it + 2, search_step(it + 1, search_step(it, st))

    _, (thr, _, n_ge, n_gt) = lax.while_loop(search_cond, search_body, (n_first, st))
    need = topk - n_gt
    any_tie = jnp.max((n_ge > topk).astype(jnp.float32)) > 0.0
    thr_adm = jnp.maximum(thr, NEG_INF_KEY + 1)

    def bias_body(kt, carry):
        k0 = pl.multiple_of(kt * KT, KT)
        kk = keys_ref[pl.ds(k0, KT), :]
        bias_ref[pl.ds(k0, KT), :] = jnp.where(kk >= thr_adm, 0.0, NEG_BIG)
        return carry

    lax.fori_loop(0, nkt, bias_body, 0)

    @pl.when(any_tie)
    def _():
        tri = _bf((lax.broadcasted_iota(jnp.int32, (KT, KT), 1)
                   <= lax.broadcasted_iota(jnp.int32, (KT, KT), 0)).astype(jnp.float32))
        need_f = need.astype(jnp.float32)

        def tie_body(kt, seen):
            k0 = pl.multiple_of(kt * KT, KT)
            kk = keys_ref[pl.ds(k0, KT), :]
            eq = kk == thr
            eq_f = eq.astype(jnp.float32)
            rank = _dot(tri, _bf(eq_f)) + seen
            take = (rank <= need_f) | (n_ge <= topk)
            sel = ((kk > thr) | (eq & take)) & (kk > NEG_INF_KEY)
            bias_ref[pl.ds(k0, KT), :] = jnp.where(sel, 0.0, NEG_BIG)
            return seen + jnp.sum(eq_f, axis=0, keepdims=True)

        lax.fori_loop(0, nkt, tie_body, jnp.zeros((1, QB), jnp.float32))

    grp = N_HEADS // N_KV
    gw = grp * QB
    for g in range(N_KV):
        q_g = jnp.concatenate(
            [q_ref[0, :, (g * grp + hh) * HEAD_DIM:(g * grp + hh + 1) * HEAD_DIM]
             for hh in range(grp)], axis=0)
        gsl = slice(g * HEAD_DIM, (g + 1) * HEAD_DIM)

        def s_body(kt, m8):
            k0 = pl.multiple_of(kt * KT, KT)
            kb = _bf(k_ref[0, pl.ds(k0, KT), gsl])
            bias = bias_ref[pl.ds(k0, KT), :]
            s = _dot_nt(kb, q_g)
            out = []
            for hh in range(grp):
                hsl = slice(hh * QB, (hh + 1) * QB)
                sh = s[:, hsl] + bias
                s_ref[pl.ds(k0, KT), hsl] = sh
                out.append(jnp.maximum(
                    m8[hh], jnp.max(sh.reshape(KT // SUBLANES, SUBLANES, QB), axis=0)))
            return tuple(out)

        m8 = lax.fori_loop(0, nkt, s_body,
                           tuple(jnp.full((SUBLANES, QB), NEG_BIG, jnp.float32)
                                 for _ in range(grp)))
        m_all = jnp.concatenate([jnp.max(m, axis=0, keepdims=True) for m in m8], axis=1)

        acc_ref[...] = jnp.zeros((HEAD_DIM, gw), jnp.float32)

        def pv_body(kt, l8):
            k0 = pl.multiple_of(kt * KT, KT)
            p = jnp.exp(s_ref[pl.ds(k0, KT), :] - m_all)
            pb = _bf(p)
            acc_ref[...] += sum(
                _dot(vt_ref[0, kt * vt_per_kt + r, gsl, :], pb[r * vt_w:(r + 1) * vt_w, :])
                for r in range(vt_per_kt))
            return l8 + jnp.sum(p.reshape(KT // SUBLANES, SUBLANES, gw), axis=0)

        l8 = lax.fori_loop(0, nkt, pv_body, jnp.zeros((SUBLANES, gw), jnp.float32))
        o_t = acc_ref[...] * (1.0 / jnp.sum(l8, axis=0, keepdims=True))
        for hh in range(grp):
            h_abs = g * grp + hh
            o_ref[:, h_abs * HEAD_DIM:(h_abs + 1) * HEAD_DIM] = o_t[:, hh * QB:(hh + 1) * QB].T

    a_out = _dot(_bf(o_ref[...] * sza_ref[0]), wpa_ref[...])
    merged = sga_ref[0] * a_out + gbb_ref[0]
    y_ref[0] = x_ref[0] + gate_ref[0] * _dot(_bf(merged), wout_ref[...])


def _attn_call(q, qi, wi, k_all, vt_all, ki2_all, sza, sga, gbb, x, gate, w_pa, w_out,
               *, pos0, l_real, kt_size, topk):
    b, t, d = x.shape
    lp = k_all.shape[1]
    tok = lambda w: pl.BlockSpec((1, QB, w), lambda i, j: (i, j, 0))
    per_b = lambda r, w: pl.BlockSpec((1, r, w), lambda i, j: (i, 0, 0))
    whole = pl.BlockSpec(memory_space=pltpu.VMEM)
    in_specs = [
        tok(ATT_W), tok(IDX_HEADS * IDX_DIM), tok(LANES),
        per_b(lp, KV_W),
        pl.BlockSpec((1,) + vt_all.shape[1:], lambda i, j: (i, 0, 0, 0)),
        per_b(lp, LANES),
        tok(ATT_W), tok(D_MODEL), tok(D_MODEL), tok(d), per_b(1, d),
        whole, whole,
    ]
    return pl.pallas_call(
        functools.partial(_attn_kernel, pos0=pos0, l_real=l_real, kt_size=kt_size, topk=topk),
        grid=(b, t // QB),
        in_specs=in_specs,
        out_specs=tok(d),
        out_shape=jax.ShapeDtypeStruct((b, t, d), jnp.float32),
        scratch_shapes=[
            pltpu.VMEM((lp, QB), jnp.int32),
            pltpu.VMEM((lp, QB), jnp.float32),
            pltpu.VMEM((lp, N_HEADS // N_KV * QB), jnp.float32),
            pltpu.VMEM((HEAD_DIM, N_HEADS // N_KV * QB), jnp.float32),
            pltpu.VMEM((QB, ATT_W), jnp.float32),
        ],
        compiler_params=pltpu.CompilerParams(
            dimension_semantics=("arbitrary", "arbitrary"), vmem_limit_bytes=VMEM_LIMIT),
        name="dsa_attn",
    )(q, qi, wi, k_all, vt_all, ki2_all, sza, sga, gbb, x, gate, w_pa, w_out)


def _rope_tables(pos, width, period):
    rot = period // ROT_FRAC
    half = rot // 2
    inv = ROPE_THETA ** (-2.0 * jnp.arange(half, dtype=jnp.float32) / rot)
    ang = pos.astype(jnp.float32)[:, None] * inv[None, :]
    cos, sin = jnp.cos(ang), jnp.sin(ang)
    n = pos.shape[0]
    rest = period - rot
    one = jnp.ones((n, rest), jnp.float32)
    zero_h = jnp.zeros((n, half), jnp.float32)
    zero_r = jnp.zeros((n, rest), jnp.float32)
    c = jnp.concatenate([cos, cos, one], axis=1)
    s_up = jnp.concatenate([-sin, zero_h, zero_r], axis=1)
    s_dn = jnp.concatenate([zero_h, sin, zero_r], axis=1)
    rep = width // period
    return tuple(jnp.tile(a, (1, rep)) for a in (c, s_up, s_dn))


def _prep_w_in(w_in):
    o_ki = ATT_W + 2 * KV_W + ATT_W + IDX_HEADS * IDX_DIM
    o_wi = o_ki + IDX_DIM
    o_u = o_wi + IDX_HEADS
    ki = w_in[:, o_ki:o_wi]
    pad = jnp.zeros((w_in.shape[0], C_END - C_WI - IDX_HEADS), w_in.dtype)
    w_r = jnp.concatenate([w_in[:, :o_ki], w_in[:, o_u:], ki, ki, w_in[:, o_wi:o_u], pad], axis=1)
    return _bf(w_r)


def _layer(x, shift, scale, gate, pos, conv_state, past, weights, *, tm, t_real, kt_size):
    norm_g, qg, kg, w_r, w_pb, conv_w, w_pa, w_out = weights
    b, t, _ = x.shape
    tabs = _rope_tables(pos, LANES, HEAD_DIM) + _rope_tables(pos, LANES, IDX_DIM)
    conv_row = t_real - (t // tm - 1) * tm
    (q, k, v, vt, sza, qi, ki, ki2, wi, sga, gbb, conv_new) = _proj_call(
        x, shift, scale, norm_g, qg, kg, tabs, w_r, w_pb, conv_w, conv_state,
        tm=tm, conv_row=conv_row)
    if past is None:
        p_len = 0
        k_all, vt_all, ki2_all = k, vt, ki2
    else:
        past_k, past_v, past_ki = past
        p_len = past_k.shape[1]
        lp = -(-(p_len + t) // kt_size) * kt_size
        extra = lp - p_len - t
        k_all = jnp.concatenate(
            [past_k, k, jnp.zeros((b, extra, KV_W), jnp.float32)], axis=1)
        vt_all = jnp.concatenate(
            [_bf(jnp.swapaxes(past_v, 1, 2)), vt[:, 0], jnp.zeros((b, KV_W, extra), jnp.bfloat16)],
            axis=2)
        vt_all = jnp.swapaxes(vt_all.reshape(b, KV_W, lp // kt_size, kt_size), 1, 2)
        ki2_all = jnp.concatenate(
            [jnp.concatenate([past_ki, past_ki], axis=-1), ki2,
             jnp.zeros((b, extra, LANES), jnp.float32)], axis=1)
    l_real = p_len + t_real
    topk = min(TOPK_MAX, l_real // 4)
    y = _attn_call(q, qi, wi, k_all, vt_all, ki2_all, sza, sga, gbb, x, gate, w_pa, w_out,
                   pos0=p_len, l_real=l_real, kt_size=kt_size, topk=topk)
    return y, k, v, ki, conv_new


def kernel(x_prompt, x_sample, cache_k, cache_v, cache_idx_k, state_conv, c_prompt, c_sample,
           w_ada, b_ada, norm_g, w_in, q_norm_g, k_norm_g, conv_w, w_pa, w_pb, w_out):
    bp, seq, d = x_prompt.shape
    bs, dec_seq, _ = x_sample.shape
    past_len = cache_k.shape[1]

    mod = _mod_call(jnp.concatenate([c_prompt, c_sample], axis=0), w_ada, b_ada)
    shift, scale, gate = (m.reshape(bp + bs, 1, d) for m in jnp.split(mod, 3, axis=-1))

    weights = (norm_g.reshape(1, d), q_norm_g.reshape(1, HEAD_DIM), k_norm_g.reshape(1, HEAD_DIM),
               _prep_w_in(w_in), _bf(w_pb), conv_w, _bf(w_pa), _bf(w_out))

    yp, k_p, v_p, ki_p, conv_p = _layer(
        x_prompt, shift[:bp], scale[:bp], gate[:bp], jnp.arange(seq, dtype=jnp.int32),
        jnp.zeros((bp, CONV_W - 1, d), x_prompt.dtype), None, weights,
        tm=256, t_real=seq, kt_size=512)

    x_pad = jnp.pad(x_sample, ((0, 0), (0, QB - dec_seq), (0, 0)))
    past = (cache_k.reshape(bs, past_len, KV_W), cache_v.reshape(bs, past_len, KV_W), cache_idx_k)
    ys, k_s, v_s, ki_s, conv_s = _layer(
        x_pad, shift[bp:], scale[bp:], gate[bp:],
        past_len + jnp.arange(QB, dtype=jnp.int32), state_conv, past, weights,
        tm=QB, t_real=dec_seq, kt_size=640)

    kv4 = lambda a, n: a[:, :n].reshape(a.shape[0], n, N_KV, HEAD_DIM)
    return (yp, ys[:, :dec_seq],
            kv4(k_p, seq), kv4(v_p, seq), ki_p, conv_p,
            kv4(k_s, dec_seq), kv4(v_s, dec_seq), ki_s[:, :dec_seq], conv_s)
```

```python
import functools
import math

import jax
import jax.numpy as jnp
from jax import lax
from jax.experimental import pallas as pl
from jax.experimental.pallas import tpu as pltpu

D_MODEL = 1024
CHUNK = 64
CHUNK_SHIFT = 6
N_HEADS = 8
N_KV = 2
HEAD_DIM = 128
ATT_W = N_HEADS * HEAD_DIM
KV_W = N_KV * HEAD_DIM
ROT_FRAC = 4
ROPE_THETA = 500000.0
IDX_HEADS = 8
IDX_DIM = 64
TOPK_MAX = 256
CONV_W = 3
EPS = 1e-6

LANES = 128
SUBLANES = 8
QB_PROMPT = 256
QB_SAMPLE = 128
Q_SCALE = HEAD_DIM ** -0.5 * math.log2(math.e)
VMEM_LIMIT = 56 * 1024 * 1024

C_Q, C_K, C_V, C_ZA, C_QI = 0, 1024, 1280, 1536, 2560
C_U, C_BG, C_CG, C_ZB, C_GA, C_GB = 3072, 4096, 5120, 6144, 7168, 8192
C_KI, C_WI, C_END = 9216, 9344, 9472

NEG_BIG = -1e30
NEG_INF_KEY = -2139095041
SEARCH_VALUE_IT = 16
SEARCH_FIRST_IT = 18
SEARCH_MAX_IT = SEARCH_VALUE_IT + 34

_NT = (((1,), (1,)), ((), ()))


def _bf(x):
    return x.astype(jnp.bfloat16)


def _dot(a, b):
    return jnp.dot(a, b, preferred_element_type=jnp.float32)


def _dot_nt(a, b):
    return lax.dot_general(a, b, _NT, preferred_element_type=jnp.float32)


def _silu(x):
    return x * jax.nn.sigmoid(x)


def _mod_kernel(c_ref, w_ref, b_ref, o_ref):
    c = c_ref[...]
    s = _silu(c)
    w = w_ref[...]
    s_hi = _bf(s)
    s_lo = _bf(s - s_hi.astype(jnp.float32))
    w_hi = _bf(w)
    w_lo = _bf(w - w_hi.astype(jnp.float32))
    acc = _dot(s_hi, w_hi) + (_dot(s_lo, w_hi) + _dot(s_hi, w_lo))
    o_ref[...] = acc + b_ref[...]


def _mod_call(c_all, w_ada, b_ada):
    nb, d = c_all.shape
    n = w_ada.shape[1]
    bn = 1024
    return pl.pallas_call(
        _mod_kernel,
        grid=(n // bn,),
        in_specs=[
            pl.BlockSpec((nb, d), lambda i: (0, 0)),
            pl.BlockSpec((d, bn), lambda i: (0, i)),
            pl.BlockSpec((1, bn), lambda i: (0, i)),
        ],
        out_specs=pl.BlockSpec((nb, bn), lambda i: (0, i)),
        out_shape=jax.ShapeDtypeStruct((nb, n), jnp.float32),
        compiler_params=pltpu.CompilerParams(dimension_semantics=("arbitrary",)),
        name="adaln_mod",
    )(c_all, w_ada, b_ada.reshape(1, n))


def _rope(x, cos, sin_up, sin_dn, half):
    up = pltpu.roll(x, LANES - half, 1)
    dn = pltpu.roll(x, half, 1)
    return x * cos + up * sin_up + dn * sin_dn


def _proj_kernel(x_ref, shift_ref, scale_ref, ng_ref, qg_ref, kg_ref,
                 rc_ref, rs1_ref, rs2_ref, ic_ref, is1_ref, is2_ref,
                 w_ref, wpb_ref, cw_ref, cs_ref,
                 q_ref, k_ref, v_ref, vt_ref, sza_ref, qi_ref, ki_ref, ki2_ref, wi_ref,
                 sga_ref, gbb_ref, conv_ref, carry_ref, *, tm, conv_row):
    t = pl.program_id(1)
    x = x_ref[0]
    ms = jnp.mean(x * x, axis=-1, keepdims=True)
    xn = x * lax.rsqrt(ms + EPS) * ng_ref[...]
    h = xn * (1.0 + scale_ref[0]) + shift_ref[0]
    hb = _bf(h)

    def proj(c0, c1):
        return _dot(hb, w_ref[:, c0:c1])

    rc, rs1, rs2 = rc_ref[...], rs1_ref[...], rs2_ref[...]
    ic, is1, is2 = ic_ref[...], is1_ref[...], is2_ref[...]
    rot_half = HEAD_DIM // ROT_FRAC // 2
    idx_half = IDX_DIM // ROT_FRAC // 2

    def head_norm_rope(xh, g):
        r = lax.rsqrt(jnp.mean(xh * xh, axis=-1, keepdims=True) + EPS)
        return _rope(xh * r * g, rc, rs1, rs2, rot_half)

    q = proj(C_Q, C_K)
    qg = qg_ref[...]
    for hh in range(N_HEADS):
        sl = slice(hh * HEAD_DIM, (hh + 1) * HEAD_DIM)
        q_ref[0, :, sl] = _bf(head_norm_rope(q[:, sl], qg) * Q_SCALE)

    k = proj(C_K, C_V)
    kg = kg_ref[...]
    for hh in range(N_KV):
        sl = slice(hh * HEAD_DIM, (hh + 1) * HEAD_DIM)
        k_ref[0, :, sl] = head_norm_rope(k[:, sl], kg)

    v = proj(C_V, C_ZA)
    v_ref[0] = v
    vt_ref[0, 0] = _bf(v.T)

    sza_ref[0] = _silu(proj(C_ZA, C_QI))

    qi = proj(C_QI, C_U)
    for p in range(IDX_HEADS * IDX_DIM // LANES):
        sl = slice(p * LANES, (p + 1) * LANES)
        qi_ref[0, :, sl] = _bf(_rope(qi[:, sl], ic, is1, is2, idx_half) * (IDX_DIM ** -0.5))

    ki2 = _rope(proj(C_KI, C_WI), ic, is1, is2, idx_half)
    ki2_ref[0] = ki2
    ki_ref[0] = ki2[:, :IDX_DIM]
    wi_ref[0] = proj(C_WI, C_END) * (IDX_HEADS ** -0.5)

    cv = proj(C_CG, C_ZB) * proj(C_U, C_BG)

    @pl.when(t == 0)
    def _():
        carry_ref[0:2, :] = cs_ref[0]

    c0 = carry_ref[0:1, :]
    c1 = carry_ref[1:2, :]
    row = lax.broadcasted_iota(jnp.int32, (tm, 1), 0)
    r1 = jnp.where(row == 0, c1, pltpu.roll(cv, 1, 0))
    r2 = jnp.where(row == 0, c0, jnp.where(row == 1, c1, pltpu.roll(cv, 2, 0)))
    cw = cw_ref[...]
    y_conv = cw[0:1, :] * r2 + cw[1:2, :] * r1 + cw[2:3, :] * cv
    carry_ref[0:2, :] = cv[tm - 2:tm, :]
    conv_ref[0] = cv[conv_row - 2:conv_row, :]

    tb = proj(C_BG, C_CG) * y_conv * _silu(proj(C_ZB, C_GA))
    b_out = _dot(_bf(tb), wpb_ref[...])
    gbb_ref[0] = jax.nn.sigmoid(proj(C_GB, C_KI)) * b_out
    sga_ref[0] = jax.nn.sigmoid(proj(C_GA, C_GB))


def _proj_call(x, shift, scale, norm_g, qg, kg, rope_tabs, w_r, w_pb, conv_w, conv_state,
               *, tm, conv_row):
    b, t, d = x.shape
    nt = t // tm
    tok = lambda w: pl.BlockSpec((1, tm, w), lambda i, j: (i, j, 0))
    per_b = lambda r, w: pl.BlockSpec((1, r, w), lambda i, j: (i, 0, 0))
    const = lambda r, w: pl.BlockSpec((r, w), lambda i, j: (0, 0))
    tab = pl.BlockSpec((tm, LANES), lambda i, j: (j, 0))
    whole = pl.BlockSpec(memory_space=pltpu.VMEM)
    f32, bf16 = jnp.float32, jnp.bfloat16
    out_shape = (
        jax.ShapeDtypeStruct((b, t, ATT_W), bf16),
        jax.ShapeDtypeStruct((b, t, KV_W), f32),
        jax.ShapeDtypeStruct((b, t, KV_W), f32),
        jax.ShapeDtypeStruct((b, nt, KV_W, tm), bf16),
        jax.ShapeDtypeStruct((b, t, ATT_W), f32),
        jax.ShapeDtypeStruct((b, t, IDX_HEADS * IDX_DIM), bf16),
        jax.ShapeDtypeStruct((b, t, IDX_DIM), f32),
        jax.ShapeDtypeStruct((b, t, LANES), f32),
        jax.ShapeDtypeStruct((b, t, LANES), f32),
        jax.ShapeDtypeStruct((b, t, D_MODEL), f32),
        jax.ShapeDtypeStruct((b, t, D_MODEL), f32),
        jax.ShapeDtypeStruct((b, CONV_W - 1, D_MODEL), f32),
    )
    out_specs = (
        tok(ATT_W), tok(KV_W), tok(KV_W),
        pl.BlockSpec((1, 1, KV_W, tm), lambda i, j: (i, j, 0, 0)),
        tok(ATT_W), tok(IDX_HEADS * IDX_DIM), tok(IDX_DIM), tok(LANES), tok(LANES),
        tok(D_MODEL), tok(D_MODEL), per_b(CONV_W - 1, D_MODEL),
    )
    in_specs = [
        tok(d), per_b(1, d), per_b(1, d), const(1, d), const(1, LANES), const(1, LANES),
        tab, tab, tab, tab, tab, tab,
        whole, whole, const(CONV_W, d), per_b(CONV_W - 1, d),
    ]
    return pl.pallas_call(
        functools.partial(_proj_kernel, tm=tm, conv_row=conv_row),
        grid=(b, nt),
        in_specs=in_specs,
        out_specs=out_specs,
        out_shape=out_shape,
        scratch_shapes=[pltpu.VMEM((SUBLANES, d), jnp.float32)],
        compiler_params=pltpu.CompilerParams(
            dimension_semantics=("arbitrary", "arbitrary"), vmem_limit_bytes=VMEM_LIMIT),
        name="in_proj",
    )(x, shift, scale, norm_g, qg, kg, *rope_tabs, w_r, w_pb, conv_w, conv_state)


def _sort_key(x):
    bits = lax.bitcast_convert_type(x, jnp.int32)
    return bits ^ ((bits >> 31) & 0x7FFFFFFF)


def _sort_key_inv(key):
    return lax.bitcast_convert_type(key ^ ((key >> 31) & 0x7FFFFFFF), jnp.float32)


def _attn_kernel(q_ref, qi_ref, wi_ref, k_ref, vt_ref, ki2_ref,
                 sza_ref, sga_ref, gbb_ref, x_ref, gate_ref, wpa_ref, wout_ref,
                 y_ref, keys_ref, bias_ref, s_ref, acc_ref, o_ref,
                 *, pos0, l_real, kt_size, topk, qb):
    KT = kt_size
    QB = qb
    grp = N_HEADS // N_KV
    vt_w = vt_ref.shape[3]
    vt_per_kt = KT // vt_w
    j = pl.program_id(1)
    q0 = pos0 + j * QB
    kmax = jnp.minimum(l_real, (((q0 + QB - 1) >> CHUNK_SHIFT) + 1) * CHUNK)
    nkt = (kmax + KT - 1) // KT

    lane = lax.broadcasted_iota(jnp.int32, (1, QB), 1)
    q_chunk = (q0 + lane) >> CHUNK_SHIFT
    w_t = wi_ref[0].T
    qi = qi_ref[0]
    lane_kt = lax.broadcasted_iota(jnp.int32, (KT, LANES), 1)
    sub_kt = lax.broadcasted_iota(jnp.int32, (KT, 1), 0)

    def score_body(kt, carry):
        smax8, smin8 = carry
        k0 = pl.multiple_of(kt * KT, KT)
        ki2 = ki2_ref[0, pl.ds(k0, KT), :]
        ki_lo = _bf(jnp.where(lane_kt < IDX_DIM, ki2, 0.0))
        ki_hi = _bf(jnp.where(lane_kt >= IDX_DIM, ki2, 0.0))
        acc = jnp.zeros((KT, QB), jnp.float32)
        for p in range(IDX_HEADS // 2):
            slab = qi[:, p * LANES:(p + 1) * LANES]
            acc = acc + w_t[2 * p:2 * p + 1, :] * jnp.maximum(_dot_nt(ki_lo, slab), 0.0)
            acc = acc + w_t[2 * p + 1:2 * p + 2, :] * jnp.maximum(_dot_nt(ki_hi, slab), 0.0)
        kpos = k0 + sub_kt
        adm = ((kpos >> CHUNK_SHIFT) <= q_chunk) & (kpos < l_real)
        keys_ref[pl.ds(k0, KT), :] = _sort_key(jnp.where(adm, acc, -jnp.inf))
        acc3 = acc.reshape(KT // SUBLANES, SUBLANES, QB)
        return (jnp.maximum(smax8, jnp.max(acc3, axis=0)),
                jnp.minimum(smin8, jnp.min(acc3, axis=0)))

    smax8, smin8 = lax.fori_loop(
        0, nkt, score_body,
        (jnp.full((SUBLANES, QB), -jnp.inf, jnp.float32),
         jnp.full((SUBLANES, QB), jnp.inf, jnp.float32)))

    def count_ge(cand):
        def body(kt, acc):
            k0 = pl.multiple_of(kt * KT, KT)
            m = (keys_ref[pl.ds(k0, KT), :] >= cand).astype(jnp.int32)
            return acc + jnp.sum(m.reshape(KT // SUBLANES, SUBLANES, QB), axis=0)
        acc = lax.fori_loop(0, nkt, body, jnp.zeros((SUBLANES, QB), jnp.int32))
        return jnp.sum(acc, axis=0, keepdims=True)

    def active_rows(lo, hi, clo):
        return (clo > topk) & (hi != lo + 1)

    def any_active(st):
        lo, hi, clo, _ = st
        return jnp.max(active_rows(lo, hi, clo).astype(jnp.float32)) > 0.0

    def search_step(it, st):
        lo, hi, clo, chi = st
        act = active_rows(lo, hi, clo)
        key_mid = lo + lax.shift_right_logical(hi - lo, 1)
        val_mid = _sort_key(_sort_key_inv(lo) * 0.5 + _sort_key_inv(hi) * 0.5)
        guess = jnp.where(it == 0, 0, jnp.where(it == 1, 1,
                          jnp.where(it < SEARCH_VALUE_IT, val_mid, key_mid)))
        cand = jnp.where((guess > lo) & (guess < hi), guess, key_mid)
        cnt = count_ge(cand)
        ge = cnt >= topk
        up_lo = act & ge
        up_hi = act & jnp.logical_not(ge)
        return (jnp.where(up_lo, cand, lo), jnp.where(up_hi, cand, hi),
                jnp.where(up_lo, cnt, clo), jnp.where(up_hi, cnt, chi))

    n_adm = jnp.minimum(l_real, (q_chunk + 1) * CHUNK)
    lo0 = _sort_key(jnp.min(smin8, axis=0, keepdims=True))
    hi0 = _sort_key(jnp.max(smax8, axis=0, keepdims=True)) + 1
    st = (lo0, hi0, n_adm, jnp.zeros((1, QB), jnp.int32))
    n_first = jnp.where(any_active(st), SEARCH_FIRST_IT, 0)
    st = lax.fori_loop(0, n_first, search_step, st)

    def search_cond(c):
        return (c[0] < SEARCH_MAX_IT) & any_active(c[1])

    def search_body(c):
        it, st = c
        return it + 2, search_step(it + 1, search_step(it, st))

    _, (thr, _, n_ge, n_gt) = lax.while_loop(search_cond, search_body, (n_first, st))
    need = topk - n_gt
    any_tie = jnp.max((n_ge > topk).astype(jnp.float32)) > 0.0
    thr_adm = jnp.maximum(thr, NEG_INF_KEY + 1)

    def bias_body(kt, carry):
        k0 = pl.multiple_of(kt * KT, KT)
        kk = keys_ref[pl.ds(k0, KT), :]
        bias_ref[pl.ds(k0, KT), :] = jnp.where(kk >= thr_adm, 0.0, NEG_BIG)
        return carry

    lax.fori_loop(0, nkt, bias_body, 0)

    @pl.when(any_tie)
    def _():
        tri = _bf((lax.broadcasted_iota(jnp.int32, (KT, KT), 1)
                   <= lax.broadcasted_iota(jnp.int32, (KT, KT), 0)).astype(jnp.float32))
        need_f = need.astype(jnp.float32)

        def tie_body(kt, seen):
            k0 = pl.multiple_of(kt * KT, KT)
            kk = keys_ref[pl.ds(k0, KT), :]
            eq = kk == thr
            eq_f = eq.astype(jnp.float32)
            rank = _dot(tri, _bf(eq_f)) + seen
            take = (rank <= need_f) | (n_ge <= topk)
            sel = ((kk > thr) | (eq & take)) & (kk > NEG_INF_KEY)
            bias_ref[pl.ds(k0, KT), :] = jnp.where(sel, 0.0, NEG_BIG)
            return seen + jnp.sum(eq_f, axis=0, keepdims=True)

        lax.fori_loop(0, nkt, tie_body, jnp.zeros((1, QB), jnp.float32))

    gw = grp * QB
    m8_init = tuple(jnp.full((SUBLANES, QB), NEG_BIG, jnp.float32) for _ in range(grp))
    l8_init = jnp.zeros((SUBLANES, gw), jnp.float32)
    q_groups = [jnp.concatenate(
        [q_ref[0, :, (g * grp + hh) * HEAD_DIM:(g * grp + hh + 1) * HEAD_DIM]
         for hh in range(grp)], axis=0) for g in range(N_KV)]

    def qk_tile(g, kt, m8):
        k0 = pl.multiple_of(kt * KT, KT)
        kb = _bf(k_ref[0, pl.ds(k0, KT), g * HEAD_DIM:(g + 1) * HEAD_DIM])
        bias = bias_ref[pl.ds(k0, KT), :]
        s = _dot_nt(kb, q_groups[g])
        out = []
        for hh in range(grp):
            sh = s[:, hh * QB:(hh + 1) * QB] + bias
            s_ref[pl.ds(k0, KT), g * gw + hh * QB:g * gw + (hh + 1) * QB] = sh
            out.append(jnp.maximum(
                m8[hh], jnp.max(sh.reshape(KT // SUBLANES, SUBLANES, QB), axis=0)))
        return tuple(out)

    def pv_tile(g, kt, l8, m_all):
        k0 = pl.multiple_of(kt * KT, KT)
        p = jnp.exp2(s_ref[pl.ds(k0, KT), g * gw:(g + 1) * gw] - m_all)
        pb = _bf(p)
        acc_ref[...] += sum(
            _dot(vt_ref[0, kt * vt_per_kt + r, g * HEAD_DIM:(g + 1) * HEAD_DIM, :],
                 pb[r * vt_w:(r + 1) * vt_w, :])
            for r in range(vt_per_kt))
        return l8 + jnp.sum(p.reshape(KT // SUBLANES, SUBLANES, gw), axis=0)

    def col_max(m8):
        return jnp.concatenate([jnp.max(m, axis=0, keepdims=True) for m in m8], axis=1)

    def finish_group(g, l8):
        o_t = acc_ref[...] * (1.0 / jnp.sum(l8, axis=0, keepdims=True))
        for hh in range(grp):
            h_abs = g * grp + hh
            o_ref[:, h_abs * HEAD_DIM:(h_abs + 1) * HEAD_DIM] = o_t[:, hh * QB:(hh + 1) * QB].T

    m8 = lax.fori_loop(0, nkt, functools.partial(qk_tile, 0), m8_init)
    for g in range(N_KV):
        m_all = col_max(m8)
        acc_ref[...] = jnp.zeros((HEAD_DIM, gw), jnp.float32)
        if g + 1 < N_KV:
            def both(kt, carry, g=g, m_all=m_all):
                return pv_tile(g, kt, carry[0], m_all), qk_tile(g + 1, kt, carry[1])
            l8, m8 = lax.fori_loop(0, nkt, both, (l8_init, m8_init))
        else:
            l8 = lax.fori_loop(0, nkt, lambda kt, l8, g=g, m_all=m_all: pv_tile(g, kt, l8, m_all),
                               l8_init)
        finish_group(g, l8)

    a_out = _dot(_bf(o_ref[...] * sza_ref[0]), wpa_ref[...])
    merged = sga_ref[0] * a_out + gbb_ref[0]
    y_ref[0] = x_ref[0] + gate_ref[0] * _dot(_bf(merged), wout_ref[...])


def _attn_call(q, qi, wi, k_all, vt_all, ki2_all, sza, sga, gbb, x, gate, w_pa, w_out,
               *, pos0, l_real, kt_size, topk, qb):
    b, t, d = x.shape
    lp = k_all.shape[1]
    tok = lambda w: pl.BlockSpec((1, qb, w), lambda i, j: (i, j, 0))
    per_b = lambda r, w: pl.BlockSpec((1, r, w), lambda i, j: (i, 0, 0))
    whole = pl.BlockSpec(memory_space=pltpu.VMEM)
    in_specs = [
        tok(ATT_W), tok(IDX_HEADS * IDX_DIM), tok(LANES),
        per_b(lp, KV_W),
        pl.BlockSpec((1,) + vt_all.shape[1:], lambda i, j: (i, 0, 0, 0)),
        per_b(lp, LANES),
        tok(ATT_W), tok(D_MODEL), tok(D_MODEL), tok(d), per_b(1, d),
        whole, whole,
    ]
    return pl.pallas_call(
        functools.partial(_attn_kernel, pos0=pos0, l_real=l_real, kt_size=kt_size, topk=topk,
                          qb=qb),
        grid=(b, t // qb),
        in_specs=in_specs,
        out_specs=tok(d),
        out_shape=jax.ShapeDtypeStruct((b, t, d), jnp.float32),
        scratch_shapes=[
            pltpu.VMEM((lp, qb), jnp.int32),
            pltpu.VMEM((lp, qb), jnp.float32),
            pltpu.VMEM((lp, N_HEADS * qb), jnp.float32),
            pltpu.VMEM((HEAD_DIM, N_HEADS // N_KV * qb), jnp.float32),
            pltpu.VMEM((qb, ATT_W), jnp.float32),
        ],
        compiler_params=pltpu.CompilerParams(
            dimension_semantics=("arbitrary", "arbitrary"), vmem_limit_bytes=VMEM_LIMIT),
        name="dsa_attn",
    )(q, qi, wi, k_all, vt_all, ki2_all, sza, sga, gbb, x, gate, w_pa, w_out)


def _rope_tables(pos, width, period):
    rot = period // ROT_FRAC
    half = rot // 2
    inv = ROPE_THETA ** (-2.0 * jnp.arange(half, dtype=jnp.float32) / rot)
    ang = pos.astype(jnp.float32)[:, None] * inv[None, :]
    cos, sin = jnp.cos(ang), jnp.sin(ang)
    n = pos.shape[0]
    rest = period - rot
    one = jnp.ones((n, rest), jnp.float32)
    zero_h = jnp.zeros((n, half), jnp.float32)
    zero_r = jnp.zeros((n, rest), jnp.float32)
    c = jnp.concatenate([cos, cos, one], axis=1)
    s_up = jnp.concatenate([-sin, zero_h, zero_r], axis=1)
    s_dn = jnp.concatenate([zero_h, sin, zero_r], axis=1)
    rep = width // period
    return tuple(jnp.tile(a, (1, rep)) for a in (c, s_up, s_dn))


def _prep_w_in(w_in):
    o_ki = ATT_W + 2 * KV_W + ATT_W + IDX_HEADS * IDX_DIM
    o_wi = o_ki + IDX_DIM
    o_u = o_wi + IDX_HEADS
    ki = w_in[:, o_ki:o_wi]
    pad = jnp.zeros((w_in.shape[0], C_END - C_WI - IDX_HEADS), w_in.dtype)
    w_r = jnp.concatenate([w_in[:, :o_ki], w_in[:, o_u:], ki, ki, w_in[:, o_wi:o_u], pad], axis=1)
    return _bf(w_r)


def _layer(x, shift, scale, gate, pos, conv_state, past, weights, *, tm, t_real, kt_size, qb):
    norm_g, qg, kg, w_r, w_pb, conv_w, w_pa, w_out = weights
    b, t, _ = x.shape
    tabs = _rope_tables(pos, LANES, HEAD_DIM) + _rope_tables(pos, LANES, IDX_DIM)
    conv_row = t_real - (t // tm - 1) * tm
    (q, k, v, vt, sza, qi, ki, ki2, wi, sga, gbb, conv_new) = _proj_call(
        x, shift, scale, norm_g, qg, kg, tabs, w_r, w_pb, conv_w, conv_state,
        tm=tm, conv_row=conv_row)
    if past is None:
        p_len = 0
        k_all, vt_all, ki2_all = k, vt, ki2
    else:
        past_k, past_v, past_ki = past
        p_len = past_k.shape[1]
        lp = -(-(p_len + t) // kt_size) * kt_size
        extra = lp - p_len - t
        k_all = jnp.concatenate(
            [past_k, k, jnp.zeros((b, extra, KV_W), jnp.float32)], axis=1)
        vt_all = jnp.concatenate(
            [_bf(jnp.swapaxes(past_v, 1, 2)), vt[:, 0], jnp.zeros((b, KV_W, extra), jnp.bfloat16)],
            axis=2)
        vt_all = jnp.swapaxes(vt_all.reshape(b, KV_W, lp // kt_size, kt_size), 1, 2)
        ki2_all = jnp.concatenate(
            [jnp.concatenate([past_ki, past_ki], axis=-1), ki2,
             jnp.zeros((b, extra, LANES), jnp.float32)], axis=1)
    l_real = p_len + t_real
    topk = min(TOPK_MAX, l_real // 4)
    y = _attn_call(q, qi, wi, k_all, vt_all, ki2_all, sza, sga, gbb, x, gate, w_pa, w_out,
                   pos0=p_len, l_real=l_real, kt_size=kt_size, topk=topk, qb=qb)
    return y, k, v, ki, conv_new


def kernel(x_prompt, x_sample, cache_k, cache_v, cache_idx_k, state_conv, c_prompt, c_sample,
           w_ada, b_ada, norm_g, w_in, q_norm_g, k_norm_g, conv_w, w_pa, w_pb, w_out):
    bp, seq, d = x_prompt.shape
    bs, dec_seq, _ = x_sample.shape
    past_len = cache_k.shape[1]

    mod = _mod_call(jnp.concatenate([c_prompt, c_sample], axis=0), w_ada, b_ada)
    shift, scale, gate = (m.reshape(bp + bs, 1, d) for m in jnp.split(mod, 3, axis=-1))

    weights = (norm_g.reshape(1, d), q_norm_g.reshape(1, HEAD_DIM), k_norm_g.reshape(1, HEAD_DIM),
               _prep_w_in(w_in), _bf(w_pb), conv_w, _bf(w_pa), _bf(w_out))

    yp, k_p, v_p, ki_p, conv_p = _layer(
        x_prompt, shift[:bp], scale[:bp], gate[:bp], jnp.arange(seq, dtype=jnp.int32),
        jnp.zeros((bp, CONV_W - 1, d), x_prompt.dtype), None, weights,
        tm=256, t_real=seq, kt_size=512, qb=QB_PROMPT)

    x_pad = jnp.pad(x_sample, ((0, 0), (0, QB_SAMPLE - dec_seq), (0, 0)))
    past = (cache_k.reshape(bs, past_len, KV_W), cache_v.reshape(bs, past_len, KV_W), cache_idx_k)
    ys, k_s, v_s, ki_s, conv_s = _layer(
        x_pad, shift[bp:], scale[bp:], gate[bp:],
        past_len + jnp.arange(QB_SAMPLE, dtype=jnp.int32), state_conv, past, weights,
        tm=QB_SAMPLE, t_real=dec_seq, kt_size=384, qb=QB_SAMPLE)

    kv4 = lambda a, n: a[:, :n].reshape(a.shape[0], n, N_KV, HEAD_DIM)
    return (yp, ys[:, :dec_seq],
            kv4(k_p, seq), kv4(v_p, seq), ki_p, conv_p,
            kv4(k_s, dec_seq), kv4(v_s, dec_seq), ki_s[:, :dec_seq], conv_s)
```

```python
import functools
import math

import jax
import jax.numpy as jnp
from jax import lax
from jax.experimental import pallas as pl
from jax.experimental.pallas import tpu as pltpu

D_MODEL = 1024
CHUNK = 64
CHUNK_SHIFT = 6
N_HEADS = 8
N_KV = 2
HEAD_DIM = 128
ATT_W = N_HEADS * HEAD_DIM
KV_W = N_KV * HEAD_DIM
ROT_FRAC = 4
ROPE_THETA = 500000.0
IDX_HEADS = 8
IDX_DIM = 64
TOPK_MAX = 256
CONV_W = 3
EPS = 1e-6

LANES = 128
SUBLANES = 8
QB_PROMPT = 256
QB_SAMPLE = 128
Q_SCALE = HEAD_DIM ** -0.5 * math.log2(math.e)
VMEM_LIMIT = 56 * 1024 * 1024

C_Q, C_K, C_V, C_ZA, C_QI = 0, 1024, 1280, 1536, 2560
C_U, C_BG, C_CG, C_ZB, C_GA, C_GB = 3072, 4096, 5120, 6144, 7168, 8192
C_KI, C_WI, C_END = 9216, 9344, 9472

NEG_BIG = -1e30
NEG_INF_KEY = -2139095041
SEARCH_VALUE_IT = 16
SEARCH_FIRST_IT = 18
SEARCH_MAX_IT = SEARCH_VALUE_IT + 34

_NT = (((1,), (1,)), ((), ()))


def _bf(x):
    return x.astype(jnp.bfloat16)


def _dot(a, b):
    return jnp.dot(a, b, preferred_element_type=jnp.float32)


def _dot_nt(a, b):
    return lax.dot_general(a, b, _NT, preferred_element_type=jnp.float32)


def _silu(x):
    return x * jax.nn.sigmoid(x)


def _mod_kernel(c_ref, w_ref, b_ref, o_ref):
    c = c_ref[...]
    s = _silu(c)
    w = w_ref[...]
    s_hi = _bf(s)
    s_lo = _bf(s - s_hi.astype(jnp.float32))
    w_hi = _bf(w)
    w_lo = _bf(w - w_hi.astype(jnp.float32))
    acc = _dot(s_hi, w_hi) + (_dot(s_lo, w_hi) + _dot(s_hi, w_lo))
    o_ref[...] = acc + b_ref[...]


def _mod_call(c_all, w_ada, b_ada):
    nb, d = c_all.shape
    n = w_ada.shape[1]
    bn = 1024
    return pl.pallas_call(
        _mod_kernel,
        grid=(n // bn,),
        in_specs=[
            pl.BlockSpec((nb, d), lambda i: (0, 0)),
            pl.BlockSpec((d, bn), lambda i: (0, i)),
            pl.BlockSpec((1, bn), lambda i: (0, i)),
        ],
        out_specs=pl.BlockSpec((nb, bn), lambda i: (0, i)),
        out_shape=jax.ShapeDtypeStruct((nb, n), jnp.float32),
        compiler_params=pltpu.CompilerParams(dimension_semantics=("arbitrary",)),
        name="adaln_mod",
    )(c_all, w_ada, b_ada.reshape(1, n))


def _rope(x, cos, sin_up, sin_dn, half):
    up = pltpu.roll(x, LANES - half, 1)
    dn = pltpu.roll(x, half, 1)
    return x * cos + up * sin_up + dn * sin_dn


def _proj_kernel(x_ref, shift_ref, scale_ref, ng_ref, qg_ref, kg_ref,
                 rc_ref, rs1_ref, rs2_ref, ic_ref, is1_ref, is2_ref,
                 wa_ref, wb_ref, wc_ref, wpb_ref, cw_ref, cs_ref,
                 q_ref, k_ref, v_ref, kb_ref, vt_ref, sza_ref, qi_ref, ki_ref, ki2_ref, wi_ref,
                 sga_ref, gbb_ref, conv_ref, carry_ref, *, tm, conv_row):
    t = pl.program_id(1)
    x = x_ref[0]
    ms = jnp.mean(x * x, axis=-1, keepdims=True)
    xn = x * lax.rsqrt(ms + EPS) * ng_ref[...]
    h = xn * (1.0 + scale_ref[0]) + shift_ref[0]
    hb = _bf(h)

    def proj(c0, c1):
        for w_ref, base in ((wc_ref, C_KI), (wb_ref, C_U), (wa_ref, C_Q)):
            if c0 >= base:
                return _dot(hb, w_ref[:, c0 - base:c1 - base])

    rc, rs1, rs2 = rc_ref[...], rs1_ref[...], rs2_ref[...]
    ic, is1, is2 = ic_ref[...], is1_ref[...], is2_ref[...]
    rot_half = HEAD_DIM // ROT_FRAC // 2
    idx_half = IDX_DIM // ROT_FRAC // 2

    def head_norm_rope(xh, g):
        r = lax.rsqrt(jnp.mean(xh * xh, axis=-1, keepdims=True) + EPS)
        return _rope(xh * r * g, rc, rs1, rs2, rot_half)

    q = proj(C_Q, C_K)
    qg = qg_ref[...]
    for hh in range(N_HEADS):
        sl = slice(hh * HEAD_DIM, (hh + 1) * HEAD_DIM)
        q_ref[0, :, sl] = _bf(head_norm_rope(q[:, sl], qg) * Q_SCALE)

    k = proj(C_K, C_V)
    kg = kg_ref[...]
    v = proj(C_V, C_ZA)
    for hh in range(N_KV):
        sl = slice(hh * HEAD_DIM, (hh + 1) * HEAD_DIM)
        kh = head_norm_rope(k[:, sl], kg)
        k_ref[0, :, hh, :] = kh
        kb_ref[0, :, sl] = _bf(kh)
        v_ref[0, :, hh, :] = v[:, sl]
    vt_ref[0, 0] = _bf(v.T)

    sza_ref[0] = _silu(proj(C_ZA, C_QI))

    qi = proj(C_QI, C_U)
    for p in range(IDX_HEADS * IDX_DIM // LANES):
        sl = slice(p * LANES, (p + 1) * LANES)
        qi_ref[0, :, sl] = _bf(_rope(qi[:, sl], ic, is1, is2, idx_half) * (IDX_DIM ** -0.5))

    ki2 = _rope(proj(C_KI, C_WI), ic, is1, is2, idx_half)
    ki2_ref[0] = ki2
    ki_ref[0] = ki2[:, :IDX_DIM]
    wi_ref[0] = proj(C_WI, C_END) * (IDX_HEADS ** -0.5)

    cv = proj(C_CG, C_ZB) * proj(C_U, C_BG)

    @pl.when(t == 0)
    def _():
        carry_ref[0:2, :] = cs_ref[0]

    c0 = carry_ref[0:1, :]
    c1 = carry_ref[1:2, :]
    row = lax.broadcasted_iota(jnp.int32, (tm, 1), 0)
    r1 = jnp.where(row == 0, c1, pltpu.roll(cv, 1, 0))
    r2 = jnp.where(row == 0, c0, jnp.where(row == 1, c1, pltpu.roll(cv, 2, 0)))
    cw = cw_ref[...]
    y_conv = cw[0:1, :] * r2 + cw[1:2, :] * r1 + cw[2:3, :] * cv
    carry_ref[0:2, :] = cv[tm - 2:tm, :]
    conv_ref[0] = cv[conv_row - 2:conv_row, :]

    tb = proj(C_BG, C_CG) * y_conv * _silu(proj(C_ZB, C_GA))
    b_out = _dot(_bf(tb), wpb_ref[...])
    gbb_ref[0] = jax.nn.sigmoid(proj(C_GB, C_KI)) * b_out
    sga_ref[0] = jax.nn.sigmoid(proj(C_GA, C_GB))


def _proj_call(x, shift, scale, norm_g, qg, kg, rope_tabs, w_r, w_pb, conv_w, conv_state,
               *, tm, conv_row):
    b, t, d = x.shape
    nt = t // tm
    tok = lambda w: pl.BlockSpec((1, tm, w), lambda i, j: (i, j, 0))
    per_b = lambda r, w: pl.BlockSpec((1, r, w), lambda i, j: (i, 0, 0))
    const = lambda r, w: pl.BlockSpec((r, w), lambda i, j: (0, 0))
    tab = pl.BlockSpec((tm, LANES), lambda i, j: (j, 0))
    kv4 = pl.BlockSpec((1, tm, N_KV, HEAD_DIM), lambda i, j: (i, j, 0, 0))
    whole = pl.BlockSpec(memory_space=pltpu.VMEM)
    f32, bf16 = jnp.float32, jnp.bfloat16
    out_shape = (
        jax.ShapeDtypeStruct((b, t, ATT_W), bf16),
        jax.ShapeDtypeStruct((b, t, N_KV, HEAD_DIM), f32),
        jax.ShapeDtypeStruct((b, t, N_KV, HEAD_DIM), f32),
        jax.ShapeDtypeStruct((b, t, KV_W), bf16),
        jax.ShapeDtypeStruct((b, nt, KV_W, tm), bf16),
        jax.ShapeDtypeStruct((b, t, ATT_W), f32),
        jax.ShapeDtypeStruct((b, t, IDX_HEADS * IDX_DIM), bf16),
        jax.ShapeDtypeStruct((b, t, IDX_DIM), f32),
        jax.ShapeDtypeStruct((b, t, LANES), f32),
        jax.ShapeDtypeStruct((b, t, LANES), f32),
        jax.ShapeDtypeStruct((b, t, D_MODEL), f32),
        jax.ShapeDtypeStruct((b, t, D_MODEL), f32),
        jax.ShapeDtypeStruct((b, CONV_W - 1, D_MODEL), f32),
    )
    out_specs = (
        tok(ATT_W), kv4, kv4, tok(KV_W),
        pl.BlockSpec((1, 1, KV_W, tm), lambda i, j: (i, j, 0, 0)),
        tok(ATT_W), tok(IDX_HEADS * IDX_DIM), tok(IDX_DIM), tok(LANES), tok(LANES),
        tok(D_MODEL), tok(D_MODEL), per_b(CONV_W - 1, D_MODEL),
    )
    in_specs = [
        tok(d), per_b(1, d), per_b(1, d), const(1, d), const(1, LANES), const(1, LANES),
        tab, tab, tab, tab, tab, tab,
        whole, whole, whole, whole, const(CONV_W, d), per_b(CONV_W - 1, d),
    ]
    return pl.pallas_call(
        functools.partial(_proj_kernel, tm=tm, conv_row=conv_row),
        grid=(b, nt),
        in_specs=in_specs,
        out_specs=out_specs,
        out_shape=out_shape,
        scratch_shapes=[pltpu.VMEM((SUBLANES, d), jnp.float32)],
        compiler_params=pltpu.CompilerParams(
            dimension_semantics=("arbitrary", "arbitrary"), vmem_limit_bytes=VMEM_LIMIT),
        name="in_proj",
    )(x, shift, scale, norm_g, qg, kg, *rope_tabs, *w_r, w_pb, conv_w, conv_state)


def _sort_key(x):
    bits = lax.bitcast_convert_type(x, jnp.int32)
    return bits ^ ((bits >> 31) & 0x7FFFFFFF)


def _sort_key_inv(key):
    return lax.bitcast_convert_type(key ^ ((key >> 31) & 0x7FFFFFFF), jnp.float32)


def _attn_kernel(q_ref, qi_ref, wi_ref, k_ref, vt_ref, ki2_ref,
                 sza_ref, sga_ref, gbb_ref, x_ref, gate_ref, wpa_ref, wout_ref,
                 y_ref, keys_ref, bias_ref, s_ref, acc_ref, o_ref,
                 *, pos0, l_real, kt_size, topk, qb):
    KT = kt_size
    QB = qb
    grp = N_HEADS // N_KV
    vt_w = vt_ref.shape[3]
    vt_per_kt = KT // vt_w
    j = pl.program_id(1)
    q0 = pos0 + j * QB
    kmax = jnp.minimum(l_real, (((q0 + QB - 1) >> CHUNK_SHIFT) + 1) * CHUNK)
    nkt = (kmax + KT - 1) // KT

    lane = lax.broadcasted_iota(jnp.int32, (1, QB), 1)
    q_chunk = (q0 + lane) >> CHUNK_SHIFT
    w_t = wi_ref[0].T
    qi = qi_ref[0]
    lane_kt = lax.broadcasted_iota(jnp.int32, (KT, LANES), 1)
    sub_kt = lax.broadcasted_iota(jnp.int32, (KT, 1), 0)

    def score_body(kt, carry):
        smax8, smin8 = carry
        k0 = pl.multiple_of(kt * KT, KT)
        ki2 = ki2_ref[0, pl.ds(k0, KT), :]
        ki_lo = _bf(jnp.where(lane_kt < IDX_DIM, ki2, 0.0))
        ki_hi = _bf(jnp.where(lane_kt >= IDX_DIM, ki2, 0.0))
        acc = jnp.zeros((KT, QB), jnp.float32)
        for p in range(IDX_HEADS // 2):
            slab = qi[:, p * LANES:(p + 1) * LANES]
            acc = acc + w_t[2 * p:2 * p + 1, :] * jnp.maximum(_dot_nt(ki_lo, slab), 0.0)
            acc = acc + w_t[2 * p + 1:2 * p + 2, :] * jnp.maximum(_dot_nt(ki_hi, slab), 0.0)
        kpos = k0 + sub_kt
        adm = ((kpos >> CHUNK_SHIFT) <= q_chunk) & (kpos < l_real)
        keys_ref[pl.ds(k0, KT), :] = _sort_key(jnp.where(adm, acc, -jnp.inf))
        acc3 = acc.reshape(KT // SUBLANES, SUBLANES, QB)
        return (jnp.maximum(smax8, jnp.max(acc3, axis=0)),
                jnp.minimum(smin8, jnp.min(acc3, axis=0)))

    smax8, smin8 = lax.fori_loop(
        0, nkt, score_body,
        (jnp.full((SUBLANES, QB), -jnp.inf, jnp.float32),
         jnp.full((SUBLANES, QB), jnp.inf, jnp.float32)))

    def count_ge(cand):
        def body(kt, acc):
            k0 = pl.multiple_of(kt * KT, KT)
            m = (keys_ref[pl.ds(k0, KT), :] >= cand).astype(jnp.int32)
            return acc + jnp.sum(m.reshape(KT // SUBLANES, SUBLANES, QB), axis=0)
        acc = lax.fori_loop(0, nkt, body, jnp.zeros((SUBLANES, QB), jnp.int32))
        return jnp.sum(acc, axis=0, keepdims=True)

    def active_rows(lo, hi, clo):
        return (clo > topk) & (hi != lo + 1)

    def any_active(st):
        lo, hi, clo, _ = st
        return jnp.max(active_rows(lo, hi, clo).astype(jnp.float32)) > 0.0

    def search_step(it, st):
        lo, hi, clo, chi = st
        act = active_rows(lo, hi, clo)
        key_mid = lo + lax.shift_right_logical(hi - lo, 1)
        val_mid = _sort_key(_sort_key_inv(lo) * 0.5 + _sort_key_inv(hi) * 0.5)
        guess = jnp.where(it == 0, 0, jnp.where(it == 1, 1,
                          jnp.where(it < SEARCH_VALUE_IT, val_mid, key_mid)))
        cand = jnp.where((guess > lo) & (guess < hi), guess, key_mid)
        cnt = count_ge(cand)
        ge = cnt >= topk
        up_lo = act & ge
        up_hi = act & jnp.logical_not(ge)
        return (jnp.where(up_lo, cand, lo), jnp.where(up_hi, cand, hi),
                jnp.where(up_lo, cnt, clo), jnp.where(up_hi, cnt, chi))

    n_adm = jnp.minimum(l_real, (q_chunk + 1) * CHUNK)
    lo0 = _sort_key(jnp.min(smin8, axis=0, keepdims=True))
    hi0 = _sort_key(jnp.max(smax8, axis=0, keepdims=True)) + 1
    st = (lo0, hi0, n_adm, jnp.zeros((1, QB), jnp.int32))
    n_first = jnp.where(any_active(st), SEARCH_FIRST_IT, 0)
    st = lax.fori_loop(0, n_first, search_step, st)

    def search_cond(c):
        return (c[0] < SEARCH_MAX_IT) & any_active(c[1])

    def search_body(c):
        it, st = c
        return it + 2, search_step(it + 1, search_step(it, st))

    _, (thr, _, n_ge, n_gt) = lax.while_loop(search_cond, search_body, (n_first, st))
    need = topk - n_gt
    any_tie = jnp.max((n_ge > topk).astype(jnp.float32)) > 0.0
    thr_adm = jnp.maximum(thr, NEG_INF_KEY + 1)

    def bias_body(kt, carry):
        k0 = pl.multiple_of(kt * KT, KT)
        kk = keys_ref[pl.ds(k0, KT), :]
        bias_ref[pl.ds(k0, KT), :] = jnp.where(kk >= thr_adm, 0.0, NEG_BIG)
        return carry

    lax.fori_loop(0, nkt, bias_body, 0)

    @pl.when(any_tie)
    def _():
        tri = _bf((lax.broadcasted_iota(jnp.int32, (LANES, LANES), 1)
                   <= lax.broadcasted_iota(jnp.int32, (LANES, LANES), 0)).astype(jnp.float32))
        need_f = need.astype(jnp.float32)
        untied = n_ge <= topk

        def tie_body(kt, seen):
            for r in range(KT // LANES):
                k0 = pl.multiple_of(kt * KT + r * LANES, LANES)
                kk = keys_ref[pl.ds(k0, LANES), :]
                eq = kk == thr
                eq_f = eq.astype(jnp.float32)
                rank = _dot(tri, _bf(eq_f)) + seen
                sel = ((kk > thr) | (eq & ((rank <= need_f) | untied))) & (kk > NEG_INF_KEY)
                bias_ref[pl.ds(k0, LANES), :] = jnp.where(sel, 0.0, NEG_BIG)
                seen = seen + jnp.sum(eq_f, axis=0, keepdims=True)
            return seen

        lax.fori_loop(0, nkt, tie_body, jnp.zeros((1, QB), jnp.float32))

    gw = grp * QB
    m8_init = tuple(jnp.full((SUBLANES, QB), NEG_BIG, jnp.float32) for _ in range(grp))
    l8_init = jnp.zeros((SUBLANES, gw), jnp.float32)
    q_groups = [jnp.concatenate(
        [q_ref[0, :, (g * grp + hh) * HEAD_DIM:(g * grp + hh + 1) * HEAD_DIM]
         for hh in range(grp)], axis=0) for g in range(N_KV)]

    def qk_tile(g, kt, m8):
        k0 = pl.multiple_of(kt * KT, KT)
        kb = k_ref[0, pl.ds(k0, KT), g * HEAD_DIM:(g + 1) * HEAD_DIM]
        bias = bias_ref[pl.ds(k0, KT), :]
        s = _dot_nt(kb, q_groups[g])
        out = []
        for hh in range(grp):
            sh = s[:, hh * QB:(hh + 1) * QB] + bias
            s_ref[pl.ds(k0, KT), g * gw + hh * QB:g * gw + (hh + 1) * QB] = sh
            out.append(jnp.maximum(
                m8[hh], jnp.max(sh.reshape(KT // SUBLANES, SUBLANES, QB), axis=0)))
        return tuple(out)

    def pv_tile(g, kt, l8, m_all):
        k0 = pl.multiple_of(kt * KT, KT)
        p = jnp.exp2(s_ref[pl.ds(k0, KT), g * gw:(g + 1) * gw] - m_all)
        pb = _bf(p)
        acc_ref[...] += sum(
            _dot(vt_ref[0, kt * vt_per_kt + r, g * HEAD_DIM:(g + 1) * HEAD_DIM, :],
                 pb[r * vt_w:(r + 1) * vt_w, :])
            for r in range(vt_per_kt))
        return l8 + jnp.sum(p.reshape(KT // SUBLANES, SUBLANES, gw), axis=0)

    def col_max(m8):
        return jnp.concatenate([jnp.max(m, axis=0, keepdims=True) for m in m8], axis=1)

    def finish_group(g, l8):
        o_t = acc_ref[...] * (1.0 / jnp.sum(l8, axis=0, keepdims=True))
        for hh in range(grp):
            h_abs = g * grp + hh
            o_ref[:, h_abs * HEAD_DIM:(h_abs + 1) * HEAD_DIM] = o_t[:, hh * QB:(hh + 1) * QB].T

    m8 = lax.fori_loop(0, nkt, functools.partial(qk_tile, 0), m8_init)
    for g in range(N_KV):
        m_all = col_max(m8)
        acc_ref[...] = jnp.zeros((HEAD_DIM, gw), jnp.float32)
        if g + 1 < N_KV:
            def both(kt, carry, g=g, m_all=m_all):
                return pv_tile(g, kt, carry[0], m_all), qk_tile(g + 1, kt, carry[1])
            l8, m8 = lax.fori_loop(0, nkt, both, (l8_init, m8_init))
        else:
            l8 = lax.fori_loop(0, nkt, lambda kt, l8, g=g, m_all=m_all: pv_tile(g, kt, l8, m_all),
                               l8_init)
        finish_group(g, l8)

    a_out = _dot(_bf(o_ref[...] * sza_ref[0]), wpa_ref[...])
    merged = sga_ref[0] * a_out + gbb_ref[0]
    y_ref[0] = x_ref[0] + gate_ref[0] * _dot(_bf(merged), wout_ref[...])


def _attn_call(q, qi, wi, k_all, vt_all, ki2_all, sza, sga, gbb, x, gate, w_pa, w_out,
               *, pos0, l_real, kt_size, topk, qb):
    b, t, d = x.shape
    lp = k_all.shape[1]
    tok = lambda w: pl.BlockSpec((1, qb, w), lambda i, j: (i, j, 0))
    per_b = lambda r, w: pl.BlockSpec((1, r, w), lambda i, j: (i, 0, 0))
    whole = pl.BlockSpec(memory_space=pltpu.VMEM)
    in_specs = [
        tok(ATT_W), tok(IDX_HEADS * IDX_DIM), tok(LANES),
        per_b(lp, KV_W),
        pl.BlockSpec((1,) + vt_all.shape[1:], lambda i, j: (i, 0, 0, 0)),
        per_b(lp, LANES),
        tok(ATT_W), tok(D_MODEL), tok(D_MODEL), tok(d), per_b(1, d),
        whole, whole,
    ]
    return pl.pallas_call(
        functools.partial(_attn_kernel, pos0=pos0, l_real=l_real, kt_size=kt_size, topk=topk,
                          qb=qb),
        grid=(b, t // qb),
        in_specs=in_specs,
        out_specs=tok(d),
        out_shape=jax.ShapeDtypeStruct((b, t, d), jnp.float32),
        scratch_shapes=[
            pltpu.VMEM((lp, qb), jnp.int32),
            pltpu.VMEM((lp, qb), jnp.float32),
            pltpu.VMEM((lp, N_HEADS * qb), jnp.float32),
            pltpu.VMEM((HEAD_DIM, N_HEADS // N_KV * qb), jnp.float32),
            pltpu.VMEM((qb, ATT_W), jnp.float32),
        ],
        compiler_params=pltpu.CompilerParams(
            dimension_semantics=("arbitrary", "arbitrary"), vmem_limit_bytes=VMEM_LIMIT),
        name="dsa_attn",
    )(q, qi, wi, k_all, vt_all, ki2_all, sza, sga, gbb, x, gate, w_pa, w_out)


def _rope_tables(pos, width, period):
    rot = period // ROT_FRAC
    half = rot // 2
    inv = ROPE_THETA ** (-2.0 * jnp.arange(half, dtype=jnp.float32) / rot)
    ang = pos.astype(jnp.float32)[:, None] * inv[None, :]
    cos, sin = jnp.cos(ang), jnp.sin(ang)
    n = pos.shape[0]
    rest = period - rot
    one = jnp.ones((n, rest), jnp.float32)
    zero_h = jnp.zeros((n, half), jnp.float32)
    zero_r = jnp.zeros((n, rest), jnp.float32)
    c = jnp.concatenate([cos, cos, one], axis=1)
    s_up = jnp.concatenate([-sin, zero_h, zero_r], axis=1)
    s_dn = jnp.concatenate([zero_h, sin, zero_r], axis=1)
    rep = width // period
    return tuple(jnp.tile(a, (1, rep)) for a in (c, s_up, s_dn))


def _prep_w_in(w_in):
    o_ki = ATT_W + 2 * KV_W + ATT_W + IDX_HEADS * IDX_DIM
    o_wi = o_ki + IDX_DIM
    o_u = o_wi + IDX_HEADS
    ki = w_in[:, o_ki:o_wi]
    pad = jnp.zeros((w_in.shape[0], C_END - C_WI - IDX_HEADS), w_in.dtype)
    w_c = jnp.concatenate([ki, ki, w_in[:, o_wi:o_u], pad], axis=1)
    return _bf(w_in[:, :o_ki]), _bf(w_in[:, o_u:]), _bf(w_c)


def _layer(x, shift, scale, gate, pos, conv_state, past, weights, *, tm, t_real, kt_size, qb):
    norm_g, qg, kg, w_r, w_pb, conv_w, w_pa, w_out = weights
    b, t, _ = x.shape
    tabs = _rope_tables(pos, LANES, HEAD_DIM) + _rope_tables(pos, LANES, IDX_DIM)
    conv_row = t_real - (t // tm - 1) * tm
    (q, k, v, kb, vt, sza, qi, ki, ki2, wi, sga, gbb, conv_new) = _proj_call(
        x, shift, scale, norm_g, qg, kg, tabs, w_r, w_pb, conv_w, conv_state,
        tm=tm, conv_row=conv_row)
    if past is None:
        p_len = 0
        k_all, vt_all, ki2_all = kb, vt, ki2
    else:
        past_k, past_v, past_ki = past
        p_len = past_k.shape[1]
        lp = -(-(p_len + t) // kt_size) * kt_size
        extra = lp - p_len - t
        k_all = jnp.concatenate(
            [_bf(past_k), kb, jnp.zeros((b, extra, KV_W), jnp.bfloat16)], axis=1)
        vt_all = jnp.concatenate(
            [_bf(jnp.swapaxes(past_v, 1, 2)), vt[:, 0], jnp.zeros((b, KV_W, extra), jnp.bfloat16)],
            axis=2)
        vt_all = jnp.swapaxes(vt_all.reshape(b, KV_W, lp // kt_size, kt_size), 1, 2)
        ki2_all = jnp.concatenate(
            [jnp.concatenate([past_ki, past_ki], axis=-1), ki2,
             jnp.zeros((b, extra, LANES), jnp.float32)], axis=1)
    l_real = p_len + t_real
    topk = min(TOPK_MAX, l_real // 4)
    y = _attn_call(q, qi, wi, k_all, vt_all, ki2_all, sza, sga, gbb, x, gate, w_pa, w_out,
                   pos0=p_len, l_real=l_real, kt_size=kt_size, topk=topk, qb=qb)
    return y, k, v, ki, conv_new


def kernel(x_prompt, x_sample, cache_k, cache_v, cache_idx_k, state_conv, c_prompt, c_sample,
           w_ada, b_ada, norm_g, w_in, q_norm_g, k_norm_g, conv_w, w_pa, w_pb, w_out):
    bp, seq, d = x_prompt.shape
    bs, dec_seq, _ = x_sample.shape
    past_len = cache_k.shape[1]

    mod = _mod_call(jnp.concatenate([c_prompt, c_sample], axis=0), w_ada, b_ada)
    shift, scale, gate = (m.reshape(bp + bs, 1, d) for m in jnp.split(mod, 3, axis=-1))

    weights = (norm_g.reshape(1, d), q_norm_g.reshape(1, HEAD_DIM), k_norm_g.reshape(1, HEAD_DIM),
               _prep_w_in(w_in), _bf(w_pb), conv_w, _bf(w_pa), _bf(w_out))

    yp, k_p, v_p, ki_p, conv_p = _layer(
        x_prompt, shift[:bp], scale[:bp], gate[:bp], jnp.arange(seq, dtype=jnp.int32),
        jnp.zeros((bp, CONV_W - 1, d), x_prompt.dtype), None, weights,
        tm=256, t_real=seq, kt_size=512, qb=QB_PROMPT)

    x_pad = jnp.pad(x_sample, ((0, 0), (0, QB_SAMPLE - dec_seq), (0, 0)))
    past = (cache_k.reshape(bs, past_len, KV_W), cache_v.reshape(bs, past_len, KV_W), cache_idx_k)
    ys, k_s, v_s, ki_s, conv_s = _layer(
        x_pad, shift[bp:], scale[bp:], gate[bp:],
        past_len + jnp.arange(QB_SAMPLE, dtype=jnp.int32), state_conv, past, weights,
        tm=QB_SAMPLE, t_real=dec_seq, kt_size=384, qb=QB_SAMPLE)

    return (yp, ys[:, :dec_seq], k_p, v_p, ki_p, conv_p,
            k_s[:, :dec_seq], v_s[:, :dec_seq], ki_s[:, :dec_seq], conv_s)
```

```python
import functools
import math

import jax
import jax.numpy as jnp
from jax import lax
from jax.experimental import pallas as pl
from jax.experimental.pallas import tpu as pltpu

D_MODEL = 1024
CHUNK = 64
CHUNK_SHIFT = 6
N_HEADS = 8
N_KV = 2
HEAD_DIM = 128
ATT_W = N_HEADS * HEAD_DIM
KV_W = N_KV * HEAD_DIM
ROT_FRAC = 4
ROPE_THETA = 500000.0
IDX_HEADS = 8
IDX_DIM = 64
TOPK_MAX = 256
CONV_W = 3
EPS = 1e-6

LANES = 128
SUBLANES = 8
QB_PROMPT = 256
QB_SAMPLE = 128
Q_SCALE = HEAD_DIM ** -0.5 * math.log2(math.e)
VMEM_LIMIT = 56 * 1024 * 1024

C_Q, C_K, C_V, C_ZA, C_QI = 0, 1024, 1280, 1536, 2560
C_U, C_BG, C_CG, C_ZB, C_GA, C_GB = 3072, 4096, 5120, 6144, 7168, 8192
C_KI, C_WI, C_END = 9216, 9344, 9472

NEG_BIG = -1e30
TINY = 1.1754943508222875e-38
SEARCH_VALUE_IT = 16
SEARCH_FIRST_IT = 18
SEARCH_MAX_IT = SEARCH_VALUE_IT + 36

_NT = (((1,), (1,)), ((), ()))


def _bf(x):
    return x.astype(jnp.bfloat16)


def _dot(a, b):
    return jnp.dot(a, b, preferred_element_type=jnp.float32)


def _dot_nt(a, b):
    return lax.dot_general(a, b, _NT, preferred_element_type=jnp.float32)


def _silu(x):
    return x * jax.nn.sigmoid(x)


def _mod_kernel(c_ref, w_ref, b_ref, o_ref):
    c = c_ref[...]
    s = _silu(c)
    w = w_ref[...]
    s_hi = _bf(s)
    s_lo = _bf(s - s_hi.astype(jnp.float32))
    w_hi = _bf(w)
    w_lo = _bf(w - w_hi.astype(jnp.float32))
    acc = _dot(s_hi, w_hi) + (_dot(s_lo, w_hi) + _dot(s_hi, w_lo))
    o_ref[...] = acc + b_ref[...]


def _mod_call(c_all, w_ada, b_ada):
    nb, d = c_all.shape
    n = w_ada.shape[1]
    bn = 1024
    return pl.pallas_call(
        _mod_kernel,
        grid=(n // bn,),
        in_specs=[
            pl.BlockSpec((nb, d), lambda i: (0, 0)),
            pl.BlockSpec((d, bn), lambda i: (0, i)),
            pl.BlockSpec((1, bn), lambda i: (0, i)),
        ],
        out_specs=pl.BlockSpec((nb, bn), lambda i: (0, i)),
        out_shape=jax.ShapeDtypeStruct((nb, n), jnp.float32),
        compiler_params=pltpu.CompilerParams(dimension_semantics=("arbitrary",)),
        name="adaln_mod",
    )(c_all, w_ada, b_ada.reshape(1, n))


def _rope(x, cos, sin_up, sin_dn, half):
    up = pltpu.roll(x, LANES - half, 1)
    dn = pltpu.roll(x, half, 1)
    return x * cos + up * sin_up + dn * sin_dn


def _proj_kernel(x_ref, shift_ref, scale_ref, ng_ref, qg_ref, kg_ref,
                 rc_ref, rs1_ref, rs2_ref, ic_ref, is1_ref, is2_ref,
                 wa_ref, wb_ref, wc_ref, wpb_ref, cw_ref, cs_ref,
                 q_ref, k_ref, v_ref, kb_ref, vt_ref, sza_ref, qi_ref, ki_ref, ki2_ref, wi_ref,
                 sga_ref, gbb_ref, conv_ref, carry_ref, *, tm, conv_row):
    t = pl.program_id(1)
    x = x_ref[0]
    ms = jnp.mean(x * x, axis=-1, keepdims=True)
    xn = x * lax.rsqrt(ms + EPS) * ng_ref[...]
    h = xn * (1.0 + scale_ref[0]) + shift_ref[0]
    hb = _bf(h)

    def proj(c0, c1):
        for w_ref, base in ((wc_ref, C_KI), (wb_ref, C_U), (wa_ref, C_Q)):
            if c0 >= base:
                return _dot(hb, w_ref[:, c0 - base:c1 - base])

    rc, rs1, rs2 = rc_ref[...], rs1_ref[...], rs2_ref[...]
    ic, is1, is2 = ic_ref[...], is1_ref[...], is2_ref[...]
    rot_half = HEAD_DIM // ROT_FRAC // 2
    idx_half = IDX_DIM // ROT_FRAC // 2

    def head_norm_rope(xh, g):
        r = lax.rsqrt(jnp.mean(xh * xh, axis=-1, keepdims=True) + EPS)
        return _rope(xh * r * g, rc, rs1, rs2, rot_half)

    q = proj(C_Q, C_K)
    qg = qg_ref[...]
    for hh in range(N_HEADS):
        sl = slice(hh * HEAD_DIM, (hh + 1) * HEAD_DIM)
        q_ref[0, :, sl] = _bf(head_norm_rope(q[:, sl], qg) * Q_SCALE)

    k = proj(C_K, C_V)
    kg = kg_ref[...]
    v = proj(C_V, C_ZA)
    for hh in range(N_KV):
        sl = slice(hh * HEAD_DIM, (hh + 1) * HEAD_DIM)
        kh = head_norm_rope(k[:, sl], kg)
        k_ref[0, :, hh, :] = kh
        kb_ref[0, :, sl] = _bf(kh)
        v_ref[0, :, hh, :] = v[:, sl]
    vt_ref[0, 0] = _bf(v.T)

    sza_ref[0] = _silu(proj(C_ZA, C_QI))

    qi = proj(C_QI, C_U)
    for p in range(IDX_HEADS * IDX_DIM // LANES):
        sl = slice(p * LANES, (p + 1) * LANES)
        qi_ref[0, :, sl] = _bf(_rope(qi[:, sl], ic, is1, is2, idx_half) * (IDX_DIM ** -0.5))

    ki2 = _rope(proj(C_KI, C_WI), ic, is1, is2, idx_half)
    ki2_ref[0] = ki2
    ki_ref[0] = ki2[:, :IDX_DIM]
    wi_ref[0] = proj(C_WI, C_END) * (IDX_HEADS ** -0.5)

    cv = proj(C_CG, C_ZB) * proj(C_U, C_BG)

    @pl.when(t == 0)
    def _():
        carry_ref[0:2, :] = cs_ref[0]

    c0 = carry_ref[0:1, :]
    c1 = carry_ref[1:2, :]
    row = lax.broadcasted_iota(jnp.int32, (tm, 1), 0)
    r1 = jnp.where(row == 0, c1, pltpu.roll(cv, 1, 0))
    r2 = jnp.where(row == 0, c0, jnp.where(row == 1, c1, pltpu.roll(cv, 2, 0)))
    cw = cw_ref[...]
    y_conv = cw[0:1, :] * r2 + cw[1:2, :] * r1 + cw[2:3, :] * cv
    carry_ref[0:2, :] = cv[tm - 2:tm, :]
    conv_ref[0] = cv[conv_row - 2:conv_row, :]

    tb = proj(C_BG, C_CG) * y_conv * _silu(proj(C_ZB, C_GA))
    b_out = _dot(_bf(tb), wpb_ref[...])
    gbb_ref[0] = jax.nn.sigmoid(proj(C_GB, C_KI)) * b_out
    sga_ref[0] = jax.nn.sigmoid(proj(C_GA, C_GB))


def _proj_call(x, shift, scale, norm_g, qg, kg, rope_tabs, w_r, w_pb, conv_w, conv_state,
               *, tm, conv_row):
    b, t, d = x.shape
    nt = t // tm
    tok = lambda w: pl.BlockSpec((1, tm, w), lambda i, j: (i, j, 0))
    per_b = lambda r, w: pl.BlockSpec((1, r, w), lambda i, j: (i, 0, 0))
    const = lambda r, w: pl.BlockSpec((r, w), lambda i, j: (0, 0))
    tab = pl.BlockSpec((tm, LANES), lambda i, j: (j, 0))
    kv4 = pl.BlockSpec((1, tm, N_KV, HEAD_DIM), lambda i, j: (i, j, 0, 0))
    whole = pl.BlockSpec(memory_space=pltpu.VMEM)
    f32, bf16 = jnp.float32, jnp.bfloat16
    out_shape = (
        jax.ShapeDtypeStruct((b, t, ATT_W), bf16),
        jax.ShapeDtypeStruct((b, t, N_KV, HEAD_DIM), f32),
        jax.ShapeDtypeStruct((b, t, N_KV, HEAD_DIM), f32),
        jax.ShapeDtypeStruct((b, t, KV_W), bf16),
        jax.ShapeDtypeStruct((b, nt, KV_W, tm), bf16),
        jax.ShapeDtypeStruct((b, t, ATT_W), f32),
        jax.ShapeDtypeStruct((b, t, IDX_HEADS * IDX_DIM), bf16),
        jax.ShapeDtypeStruct((b, t, IDX_DIM), f32),
        jax.ShapeDtypeStruct((b, t, LANES), f32),
        jax.ShapeDtypeStruct((b, t, LANES), f32),
        jax.ShapeDtypeStruct((b, t, D_MODEL), f32),
        jax.ShapeDtypeStruct((b, t, D_MODEL), f32),
        jax.ShapeDtypeStruct((b, CONV_W - 1, D_MODEL), f32),
    )
    out_specs = (
        tok(ATT_W), kv4, kv4, tok(KV_W),
        pl.BlockSpec((1, 1, KV_W, tm), lambda i, j: (i, j, 0, 0)),
        tok(ATT_W), tok(IDX_HEADS * IDX_DIM), tok(IDX_DIM), tok(LANES), tok(LANES),
        tok(D_MODEL), tok(D_MODEL), per_b(CONV_W - 1, D_MODEL),
    )
    in_specs = [
        tok(d), per_b(1, d), per_b(1, d), const(1, d), const(1, LANES), const(1, LANES),
        tab, tab, tab, tab, tab, tab,
        whole, whole, whole, whole, const(CONV_W, d), per_b(CONV_W - 1, d),
    ]
    return pl.pallas_call(
        functools.partial(_proj_kernel, tm=tm, conv_row=conv_row),
        grid=(b, nt),
        in_specs=in_specs,
        out_specs=out_specs,
        out_shape=out_shape,
        scratch_shapes=[pltpu.VMEM((SUBLANES, d), jnp.float32)],
        compiler_params=pltpu.CompilerParams(
            dimension_semantics=("arbitrary", "arbitrary"), vmem_limit_bytes=VMEM_LIMIT),
        name="in_proj",
    )(x, shift, scale, norm_g, qg, kg, *rope_tabs, *w_r, w_pb, conv_w, conv_state)


def _key_tiles(pos0, j, *, l_real, kt_size, qb, minimum=jnp.minimum):
    last_q = pos0 + j * qb + qb - 1
    kmax = minimum(l_real, ((last_q >> CHUNK_SHIFT) + 1) * CHUNK)
    return (kmax + kt_size - 1) // kt_size


def _unrolled(n, body, init):
    carry = init
    for i in range(n):
        carry = body(i, carry)
    return carry


def _attn_kernel(*refs, pos0, l_real, kt_size, topk, qb, nkt_values):
    nkt = _key_tiles(pos0, pl.program_id(1), l_real=l_real, kt_size=kt_size, qb=qb)
    for c in nkt_values:
        run = functools.partial(_attn_block, *refs, pos0=pos0, l_real=l_real, kt_size=kt_size,
                                topk=topk, qb=qb, nkt=c)
        if len(nkt_values) == 1:
            run()
        else:
            pl.when(nkt == c)(run)


def _attn_block(q_ref, qi_ref, wi_ref, k_ref, vt_ref, ki2_ref,
                sza_ref, sga_ref, gbb_ref, x_ref, gate_ref, wpa_ref, wout_ref,
                y_ref, sc_ref, bias_ref, s_ref, acc_ref, o_ref,
                *, pos0, l_real, kt_size, topk, qb, nkt):
    KT = kt_size
    QB = qb
    grp = N_HEADS // N_KV
    vt_w = vt_ref.shape[3]
    vt_per_kt = KT // vt_w
    q0 = pos0 + pl.program_id(1) * QB

    lane = lax.broadcasted_iota(jnp.int32, (1, QB), 1)
    q_chunk = (q0 + lane) >> CHUNK_SHIFT
    w_t = wi_ref[0].T
    qi = qi_ref[0]
    lane_kt = lax.broadcasted_iota(jnp.int32, (KT, LANES), 1)
    sub_kt = lax.broadcasted_iota(jnp.int32, (KT, 1), 0)

    def score_body(kt, carry):
        smax8, smin8 = carry
        k0 = kt * KT
        ki2 = ki2_ref[0, pl.ds(k0, KT), :]
        ki_lo = _bf(jnp.where(lane_kt < IDX_DIM, ki2, 0.0))
        ki_hi = _bf(jnp.where(lane_kt >= IDX_DIM, ki2, 0.0))
        acc = jnp.zeros((KT, QB), jnp.float32)
        for p in range(IDX_HEADS // 2):
            slab = qi[:, p * LANES:(p + 1) * LANES]
            acc = acc + w_t[2 * p:2 * p + 1, :] * jnp.maximum(_dot_nt(ki_lo, slab), 0.0)
            acc = acc + w_t[2 * p + 1:2 * p + 2, :] * jnp.maximum(_dot_nt(ki_hi, slab), 0.0)
        kpos = k0 + sub_kt
        adm = ((kpos >> CHUNK_SHIFT) <= q_chunk) & (kpos < l_real)
        sc_ref[pl.ds(k0, KT), :] = jnp.where(adm, acc, -jnp.inf)
        acc3 = acc.reshape(KT // SUBLANES, SUBLANES, QB)
        return (jnp.maximum(smax8, jnp.max(acc3, axis=0)),
                jnp.minimum(smin8, jnp.min(acc3, axis=0)))

    smax8, smin8 = _unrolled(
        nkt, score_body,
        (jnp.full((SUBLANES, QB), -jnp.inf, jnp.float32),
         jnp.full((SUBLANES, QB), jnp.inf, jnp.float32)))

    def count_ge(cand):
        def body(kt, acc):
            k0 = kt * KT
            m = (sc_ref[pl.ds(k0, KT), :] >= cand).astype(jnp.int32)
            return acc + jnp.sum(m.reshape(KT // SUBLANES, SUBLANES, QB), axis=0)
        acc = _unrolled(nkt, body, jnp.zeros((SUBLANES, QB), jnp.int32))
        return jnp.sum(acc, axis=0, keepdims=True)

    def inside(x, lo, hi):
        return (x > lo) & (x < hi)

    def active_rows(lo, hi, clo):
        return (clo > topk) & inside(lo * 0.5 + hi * 0.5, lo, hi)

    def any_active(st):
        lo, hi, clo, _ = st
        return jnp.max(active_rows(lo, hi, clo).astype(jnp.float32)) > 0.0

    def search_step(it, st):
        lo, hi, clo, chi = st
        act = active_rows(lo, hi, clo)
        mid = lo * 0.5 + hi * 0.5
        near_zero = jnp.where(lo >= 0.0, TINY, -TINY)
        geo = jnp.sqrt(jnp.abs(lo)) * jnp.sqrt(jnp.abs(hi)) * jnp.where(lo >= 0.0, 1.0, -1.0)
        guess = jnp.where(it == 0, 0.0, jnp.where(it == 1, near_zero,
                          jnp.where(it < SEARCH_VALUE_IT, mid, geo)))
        cand = jnp.where(inside(guess, lo, hi), guess, mid)
        cnt = count_ge(cand)
        ge = cnt >= topk
        up_lo = act & ge
        up_hi = act & jnp.logical_not(ge)
        return (jnp.where(up_lo, cand, lo), jnp.where(up_hi, cand, hi),
                jnp.where(up_lo, cnt, clo), jnp.where(up_hi, cnt, chi))

    n_adm = jnp.minimum(l_real, (q_chunk + 1) * CHUNK)
    lo0 = jnp.min(smin8, axis=0, keepdims=True)
    smax = jnp.max(smax8, axis=0, keepdims=True)
    hi0 = smax + jnp.abs(smax) + 1.0
    st = (lo0, hi0, n_adm, jnp.zeros((1, QB), jnp.int32))
    n_first = jnp.where(any_active(st), SEARCH_FIRST_IT, 0)
    st = lax.fori_loop(0, n_first, search_step, st)

    def search_cond(c):
        return (c[0] < SEARCH_MAX_IT) & any_active(c[1])

    def search_body(c):
        it, st = c
        return it + 2, search_step(it + 1, search_step(it, st))

    _, (thr, _, n_ge, n_gt) = lax.while_loop(search_cond, search_body, (n_first, st))
    need = topk - n_gt
    any_tie = jnp.max((n_ge > topk).astype(jnp.float32)) > 0.0

    def bias_body(kt, carry):
        k0 = kt * KT
        bias_ref[pl.ds(k0, KT), :] = jnp.where(sc_ref[pl.ds(k0, KT), :] >= thr, 0.0, NEG_BIG)
        return carry

    _unrolled(nkt, bias_body, 0)

    @pl.when(any_tie)
    def _():
        tri = _bf((lax.broadcasted_iota(jnp.int32, (LANES, LANES), 1)
                   <= lax.broadcasted_iota(jnp.int32, (LANES, LANES), 0)).astype(jnp.float32))
        need_f = need.astype(jnp.float32)
        untied = n_ge <= topk

        def tie_body(kt, seen):
            for r in range(KT // LANES):
                k0 = kt * KT + r * LANES
                sc = sc_ref[pl.ds(k0, LANES), :]
                eq = sc == thr
                eq_f = eq.astype(jnp.float32)
                rank = _dot(tri, _bf(eq_f)) + seen
                sel = (sc > thr) | (eq & ((rank <= need_f) | untied))
                bias_ref[pl.ds(k0, LANES), :] = jnp.where(sel, 0.0, NEG_BIG)
                seen = seen + jnp.sum(eq_f, axis=0, keepdims=True)
            return seen

        _unrolled(nkt, tie_body, jnp.zeros((1, QB), jnp.float32))

    gw = grp * QB
    m8_init = tuple(jnp.full((SUBLANES, QB), NEG_BIG, jnp.float32) for _ in range(grp))
    l8_init = jnp.zeros((SUBLANES, gw), jnp.float32)
    q_groups = [jnp.concatenate(
        [q_ref[0, :, (g * grp + hh) * HEAD_DIM:(g * grp + hh + 1) * HEAD_DIM]
         for hh in range(grp)], axis=0) for g in range(N_KV)]

    def qk_tile(g, kt, m8):
        k0 = kt * KT
        kb = k_ref[0, pl.ds(k0, KT), g * HEAD_DIM:(g + 1) * HEAD_DIM]
        bias = bias_ref[pl.ds(k0, KT), :]
        s = _dot_nt(kb, q_groups[g])
        out = []
        for hh in range(grp):
            sh = s[:, hh * QB:(hh + 1) * QB] + bias
            s_ref[pl.ds(k0, KT), g * gw + hh * QB:g * gw + (hh + 1) * QB] = sh
            out.append(jnp.maximum(
                m8[hh], jnp.max(sh.reshape(KT // SUBLANES, SUBLANES, QB), axis=0)))
        return tuple(out)

    def pv_tile(g, kt, l8, m_all):
        k0 = kt * KT
        p = jnp.exp2(s_ref[pl.ds(k0, KT), g * gw:(g + 1) * gw] - m_all)
        pb = _bf(p)
        acc_ref[...] += sum(
            _dot(vt_ref[0, kt * vt_per_kt + r, g * HEAD_DIM:(g + 1) * HEAD_DIM, :],
                 pb[r * vt_w:(r + 1) * vt_w, :])
            for r in range(vt_per_kt))
        return l8 + jnp.sum(p.reshape(KT // SUBLANES, SUBLANES, gw), axis=0)

    def col_max(m8):
        return jnp.concatenate([jnp.max(m, axis=0, keepdims=True) for m in m8], axis=1)

    def finish_group(g, l8):
        o_t = acc_ref[...] * (1.0 / jnp.sum(l8, axis=0, keepdims=True))
        for hh in range(grp):
            h_abs = g * grp + hh
            o_ref[:, h_abs * HEAD_DIM:(h_abs + 1) * HEAD_DIM] = o_t[:, hh * QB:(hh + 1) * QB].T

    m8 = _unrolled(nkt, functools.partial(qk_tile, 0), m8_init)
    for g in range(N_KV):
        m_all = col_max(m8)
        acc_ref[...] = jnp.zeros((HEAD_DIM, gw), jnp.float32)
        if g + 1 < N_KV:
            def both(kt, carry, g=g, m_all=m_all):
                return pv_tile(g, kt, carry[0], m_all), qk_tile(g + 1, kt, carry[1])
            l8, m8 = _unrolled(nkt, both, (l8_init, m8_init))
        else:
            l8 = _unrolled(nkt, lambda kt, l8, g=g, m_all=m_all: pv_tile(g, kt, l8, m_all),
                           l8_init)
        finish_group(g, l8)

    a_out = _dot(_bf(o_ref[...] * sza_ref[0]), wpa_ref[...])
    merged = sga_ref[0] * a_out + gbb_ref[0]
    y_ref[0] = x_ref[0] + gate_ref[0] * _dot(_bf(merged), wout_ref[...])


def _attn_call(q, qi, wi, k_all, vt_all, ki2_all, sza, sga, gbb, x, gate, w_pa, w_out,
               *, pos0, l_real, kt_size, topk, qb):
    b, t, d = x.shape
    lp = k_all.shape[1]
    nkt_values = sorted({_key_tiles(pos0, jj, l_real=l_real, kt_size=kt_size, qb=qb, minimum=min)
                         for jj in range(t // qb)})
    tok = lambda w: pl.BlockSpec((1, qb, w), lambda i, j: (i, j, 0))
    per_b = lambda r, w: pl.BlockSpec((1, r, w), lambda i, j: (i, 0, 0))
    whole = pl.BlockSpec(memory_space=pltpu.VMEM)
    in_specs = [
        tok(ATT_W), tok(IDX_HEADS * IDX_DIM), tok(LANES),
        per_b(lp, KV_W),
        pl.BlockSpec((1,) + vt_all.shape[1:], lambda i, j: (i, 0, 0, 0)),
        per_b(lp, LANES),
        tok(ATT_W), tok(D_MODEL), tok(D_MODEL), tok(d), per_b(1, d),
        whole, whole,
    ]
    return pl.pallas_call(
        functools.partial(_attn_kernel, pos0=pos0, l_real=l_real, kt_size=kt_size, topk=topk,
                          qb=qb, nkt_values=nkt_values),
        grid=(b, t // qb),
        in_specs=in_specs,
        out_specs=tok(d),
        out_shape=jax.ShapeDtypeStruct((b, t, d), jnp.float32),
        scratch_shapes=[
            pltpu.VMEM((lp, qb), jnp.float32),
            pltpu.VMEM((lp, qb), jnp.float32),
            pltpu.VMEM((lp, N_HEADS * qb), jnp.float32),
            pltpu.VMEM((HEAD_DIM, N_HEADS // N_KV * qb), jnp.float32),
            pltpu.VMEM((qb, ATT_W), jnp.float32),
        ],
        compiler_params=pltpu.CompilerParams(
            dimension_semantics=("arbitrary", "arbitrary"), vmem_limit_bytes=VMEM_LIMIT),
        name="dsa_attn",
    )(q, qi, wi, k_all, vt_all, ki2_all, sza, sga, gbb, x, gate, w_pa, w_out)


def _rope_tables(pos, width, period):
    rot = period // ROT_FRAC
    half = rot // 2
    inv = ROPE_THETA ** (-2.0 * jnp.arange(half, dtype=jnp.float32) / rot)
    ang = pos.astype(jnp.float32)[:, None] * inv[None, :]
    cos, sin = jnp.cos(ang), jnp.sin(ang)
    n = pos.shape[0]
    rest = period - rot
    one = jnp.ones((n, rest), jnp.float32)
    zero_h = jnp.zeros((n, half), jnp.float32)
    zero_r = jnp.zeros((n, rest), jnp.float32)
    c = jnp.concatenate([cos, cos, one], axis=1)
    s_up = jnp.concatenate([-sin, zero_h, zero_r], axis=1)
    s_dn = jnp.concatenate([zero_h, sin, zero_r], axis=1)
    rep = width // period
    return tuple(jnp.tile(a, (1, rep)) for a in (c, s_up, s_dn))


def _prep_w_in(w_in):
    o_ki = ATT_W + 2 * KV_W + ATT_W + IDX_HEADS * IDX_DIM
    o_wi = o_ki + IDX_DIM
    o_u = o_wi + IDX_HEADS
    ki = w_in[:, o_ki:o_wi]
    pad = jnp.zeros((w_in.shape[0], C_END - C_WI - IDX_HEADS), w_in.dtype)
    w_c = jnp.concatenate([ki, ki, w_in[:, o_wi:o_u], pad], axis=1)
    return _bf(w_in[:, :o_ki]), _bf(w_in[:, o_u:]), _bf(w_c)


def _layer(x, shift, scale, gate, pos, conv_state, past, weights, *, tm, t_real, kt_size, qb):
    norm_g, qg, kg, w_r, w_pb, conv_w, w_pa, w_out = weights
    b, t, _ = x.shape
    tabs = _rope_tables(pos, LANES, HEAD_DIM) + _rope_tables(pos, LANES, IDX_DIM)
    conv_row = t_real - (t // tm - 1) * tm
    (q, k, v, kb, vt, sza, qi, ki, ki2, wi, sga, gbb, conv_new) = _proj_call(
        x, shift, scale, norm_g, qg, kg, tabs, w_r, w_pb, conv_w, conv_state,
        tm=tm, conv_row=conv_row)
    if past is None:
        p_len = 0
        k_all, vt_all, ki2_all = kb, vt, ki2
    else:
        past_k, past_v, past_ki = past
        p_len = past_k.shape[1]
        lp = -(-(p_len + t) // kt_size) * kt_size
        extra = lp - p_len - t
        k_all = jnp.concatenate(
            [_bf(past_k), kb, jnp.zeros((b, extra, KV_W), jnp.bfloat16)], axis=1)
        vt_all = jnp.concatenate(
            [_bf(jnp.swapaxes(past_v, 1, 2)), vt[:, 0], jnp.zeros((b, KV_W, extra), jnp.bfloat16)],
            axis=2)
        vt_all = jnp.swapaxes(vt_all.reshape(b, KV_W, lp // kt_size, kt_size), 1, 2)
        ki2_all = jnp.concatenate(
            [jnp.concatenate([past_ki, past_ki], axis=-1), ki2,
             jnp.zeros((b, extra, LANES), jnp.float32)], axis=1)
    l_real = p_len + t_real
    topk = min(TOPK_MAX, l_real // 4)
    y = _attn_call(q, qi, wi, k_all, vt_all, ki2_all, sza, sga, gbb, x, gate, w_pa, w_out,
                   pos0=p_len, l_real=l_real, kt_size=kt_size, topk=topk, qb=qb)
    return y, k, v, ki, conv_new


def kernel(x_prompt, x_sample, cache_k, cache_v, cache_idx_k, state_conv, c_prompt, c_sample,
           w_ada, b_ada, norm_g, w_in, q_norm_g, k_norm_g, conv_w, w_pa, w_pb, w_out):
    bp, seq, d = x_prompt.shape
    bs, dec_seq, _ = x_sample.shape
    past_len = cache_k.shape[1]

    mod = _mod_call(jnp.concatenate([c_prompt, c_sample], axis=0), w_ada, b_ada)
    shift, scale, gate = (m.reshape(bp + bs, 1, d) for m in jnp.split(mod, 3, axis=-1))

    weights = (norm_g.reshape(1, d), q_norm_g.reshape(1, HEAD_DIM), k_norm_g.reshape(1, HEAD_DIM),
               _prep_w_in(w_in), _bf(w_pb), conv_w, _bf(w_pa), _bf(w_out))

    yp, k_p, v_p, ki_p, conv_p = _layer(
        x_prompt, shift[:bp], scale[:bp], gate[:bp], jnp.arange(seq, dtype=jnp.int32),
        jnp.zeros((bp, CONV_W - 1, d), x_prompt.dtype), None, weights,
        tm=256, t_real=seq, kt_size=512, qb=QB_PROMPT)

    x_pad = jnp.pad(x_sample, ((0, 0), (0, QB_SAMPLE - dec_seq), (0, 0)))
    past = (cache_k.reshape(bs, past_len, KV_W), cache_v.reshape(bs, past_len, KV_W), cache_idx_k)
    ys, k_s, v_s, ki_s, conv_s = _layer(
        x_pad, shift[bp:], scale[bp:], gate[bp:],
        past_len + jnp.arange(QB_SAMPLE, dtype=jnp.int32), state_conv, past, weights,
        tm=QB_SAMPLE, t_real=dec_seq, kt_size=384, qb=QB_SAMPLE)

    return (yp, ys[:, :dec_seq], k_p, v_p, ki_p, conv_p,
            k_s[:, :dec_seq], v_s[:, :dec_seq], ki_s[:, :dec_seq], conv_s)
```

```python
import functools
import math

import jax
import jax.numpy as jnp
from jax import lax
from jax.experimental import pallas as pl
from jax.experimental.pallas import tpu as pltpu

D_MODEL = 1024
CHUNK = 64
CHUNK_SHIFT = 6
N_HEADS = 8
N_KV = 2
HEAD_DIM = 128
ATT_W = N_HEADS * HEAD_DIM
KV_W = N_KV * HEAD_DIM
ROT_FRAC = 4
ROPE_THETA = 500000.0
IDX_HEADS = 8
IDX_DIM = 64
TOPK_MAX = 256
CONV_W = 3
EPS = 1e-6

LANES = 128
SUBLANES = 8
QB_PROMPT = 256
QB_SAMPLE = 128
Q_SCALE = HEAD_DIM ** -0.5 * math.log2(math.e)
VMEM_LIMIT = 56 * 1024 * 1024

C_Q, C_K, C_V, C_ZA, C_QI = 0, 1024, 1280, 1536, 2560
C_U, C_BG, C_CG, C_ZB, C_GA, C_GB = 3072, 4096, 5120, 6144, 7168, 8192
C_KI, C_WI, C_END = 9216, 9344, 9472

NEG_BIG = -1e30
TINY = 1.1754943508222875e-38
SEARCH_VALUE_IT = 16
SEARCH_FIRST_IT = 18
SEARCH_MAX_IT = SEARCH_VALUE_IT + 36

_NT = (((1,), (1,)), ((), ()))


def _bf(x):
    return x.astype(jnp.bfloat16)


def _dot(a, b):
    return jnp.dot(a, b, preferred_element_type=jnp.float32)


def _dot_nt(a, b):
    return lax.dot_general(a, b, _NT, preferred_element_type=jnp.float32)


def _silu(x):
    return x * jax.nn.sigmoid(x)


def _mod_kernel(c_ref, w_ref, b_ref, o_ref):
    c = c_ref[...]
    s = _silu(c)
    w = w_ref[...]
    s_hi = _bf(s)
    s_lo = _bf(s - s_hi.astype(jnp.float32))
    w_hi = _bf(w)
    w_lo = _bf(w - w_hi.astype(jnp.float32))
    acc = _dot(s_hi, w_hi) + (_dot(s_lo, w_hi) + _dot(s_hi, w_lo))
    o_ref[...] = acc + b_ref[...]


def _mod_call(c_all, w_ada, b_ada):
    nb, d = c_all.shape
    n = w_ada.shape[1]
    bn = 1024
    return pl.pallas_call(
        _mod_kernel,
        grid=(n // bn,),
        in_specs=[
            pl.BlockSpec((nb, d), lambda i: (0, 0)),
            pl.BlockSpec((d, bn), lambda i: (0, i)),
            pl.BlockSpec((1, bn), lambda i: (0, i)),
        ],
        out_specs=pl.BlockSpec((nb, bn), lambda i: (0, i)),
        out_shape=jax.ShapeDtypeStruct((nb, n), jnp.float32),
        compiler_params=pltpu.CompilerParams(dimension_semantics=("arbitrary",)),
        name="adaln_mod",
    )(c_all, w_ada, b_ada.reshape(1, n))


def _rope(x, cos, sin_up, sin_dn, half):
    up = pltpu.roll(x, LANES - half, 1)
    dn = pltpu.roll(x, half, 1)
    return x * cos + up * sin_up + dn * sin_dn


def _proj_kernel(x_ref, shift_ref, scale_ref, ng_ref, qg_ref, kg_ref,
                 rc_ref, rs1_ref, rs2_ref, ic_ref, is1_ref, is2_ref,
                 wa_ref, wb_ref, wc_ref, wpb_ref, cw_ref, cs_ref,
                 q_ref, k_ref, v_ref, kb_ref, vt_ref, sza_ref, qi_ref, ki_ref, ki2_ref, wi_ref,
                 sga_ref, gbb_ref, conv_ref, carry_ref, *, tm, conv_row):
    t = pl.program_id(1)
    x = x_ref[0]
    ms = jnp.mean(x * x, axis=-1, keepdims=True)
    xn = x * lax.rsqrt(ms + EPS) * ng_ref[...]
    h = xn * (1.0 + scale_ref[0]) + shift_ref[0]
    hb = _bf(h)

    def proj(c0, c1):
        for w_ref, base in ((wc_ref, C_KI), (wb_ref, C_U), (wa_ref, C_Q)):
            if c0 >= base:
                return _dot(hb, w_ref[:, c0 - base:c1 - base])

    rc, rs1, rs2 = rc_ref[...], rs1_ref[...], rs2_ref[...]
    ic, is1, is2 = ic_ref[...], is1_ref[...], is2_ref[...]
    rot_half = HEAD_DIM // ROT_FRAC // 2
    idx_half = IDX_DIM // ROT_FRAC // 2

    def head_norm_rope(xh, g):
        r = lax.rsqrt(jnp.mean(xh * xh, axis=-1, keepdims=True) + EPS)
        return _rope(xh * r * g, rc, rs1, rs2, rot_half)

    q = proj(C_Q, C_K)
    qg = qg_ref[...]
    for hh in range(N_HEADS):
        sl = slice(hh * HEAD_DIM, (hh + 1) * HEAD_DIM)
        q_ref[0, :, sl] = _bf(head_norm_rope(q[:, sl], qg) * Q_SCALE)

    k = proj(C_K, C_V)
    kg = kg_ref[...]
    v = proj(C_V, C_ZA)
    for hh in range(N_KV):
        sl = slice(hh * HEAD_DIM, (hh + 1) * HEAD_DIM)
        kh = head_norm_rope(k[:, sl], kg)
        k_ref[0, :, hh, :] = kh
        kb_ref[0, :, sl] = _bf(kh)
        v_ref[0, :, hh, :] = v[:, sl]
    vt_ref[0, 0] = _bf(v.T)

    sza_ref[0] = _silu(proj(C_ZA, C_QI))

    qi = proj(C_QI, C_U)
    for p in range(IDX_HEADS * IDX_DIM // LANES):
        sl = slice(p * LANES, (p + 1) * LANES)
        qi_ref[0, :, sl] = _bf(_rope(qi[:, sl], ic, is1, is2, idx_half) * (IDX_DIM ** -0.5))

    ki2 = _rope(proj(C_KI, C_WI), ic, is1, is2, idx_half)
    ki2_ref[0] = ki2
    ki_ref[0] = ki2[:, :IDX_DIM]
    wi_ref[0] = proj(C_WI, C_END) * (IDX_HEADS ** -0.5)

    cv = proj(C_CG, C_ZB) * proj(C_U, C_BG)

    @pl.when(t == 0)
    def _():
        carry_ref[0:2, :] = cs_ref[0]

    c0 = carry_ref[0:1, :]
    c1 = carry_ref[1:2, :]
    row = lax.broadcasted_iota(jnp.int32, (tm, 1), 0)
    r1 = jnp.where(row == 0, c1, pltpu.roll(cv, 1, 0))
    r2 = jnp.where(row == 0, c0, jnp.where(row == 1, c1, pltpu.roll(cv, 2, 0)))
    cw = cw_ref[...]
    y_conv = cw[0:1, :] * r2 + cw[1:2, :] * r1 + cw[2:3, :] * cv
    carry_ref[0:2, :] = cv[tm - 2:tm, :]
    conv_ref[0] = cv[conv_row - 2:conv_row, :]

    tb = proj(C_BG, C_CG) * y_conv * _silu(proj(C_ZB, C_GA))
    b_out = _dot(_bf(tb), wpb_ref[...])
    gbb_ref[0] = jax.nn.sigmoid(proj(C_GB, C_KI)) * b_out
    sga_ref[0] = jax.nn.sigmoid(proj(C_GA, C_GB))


def _proj_call(x, shift, scale, norm_g, qg, kg, rope_tabs, w_r, w_pb, conv_w, conv_state,
               *, tm, conv_row):
    b, t, d = x.shape
    nt = t // tm
    tok = lambda w: pl.BlockSpec((1, tm, w), lambda i, j: (i, j, 0))
    per_b = lambda r, w: pl.BlockSpec((1, r, w), lambda i, j: (i, 0, 0))
    const = lambda r, w: pl.BlockSpec((r, w), lambda i, j: (0, 0))
    tab = pl.BlockSpec((tm, LANES), lambda i, j: (j, 0))
    kv4 = pl.BlockSpec((1, tm, N_KV, HEAD_DIM), lambda i, j: (i, j, 0, 0))
    whole = pl.BlockSpec(memory_space=pltpu.VMEM)
    f32, bf16 = jnp.float32, jnp.bfloat16
    out_shape = (
        jax.ShapeDtypeStruct((b, t, ATT_W), bf16),
        jax.ShapeDtypeStruct((b, t, N_KV, HEAD_DIM), f32),
        jax.ShapeDtypeStruct((b, t, N_KV, HEAD_DIM), f32),
        jax.ShapeDtypeStruct((b, t, KV_W), bf16),
        jax.ShapeDtypeStruct((b, nt, KV_W, tm), bf16),
        jax.ShapeDtypeStruct((b, t, ATT_W), f32),
        jax.ShapeDtypeStruct((b, t, IDX_HEADS * IDX_DIM), bf16),
        jax.ShapeDtypeStruct((b, t, IDX_DIM), f32),
        jax.ShapeDtypeStruct((b, t, LANES), f32),
        jax.ShapeDtypeStruct((b, t, LANES), f32),
        jax.ShapeDtypeStruct((b, t, D_MODEL), f32),
        jax.ShapeDtypeStruct((b, t, D_MODEL), f32),
        jax.ShapeDtypeStruct((b, CONV_W - 1, D_MODEL), f32),
    )
    out_specs = (
        tok(ATT_W), kv4, kv4, tok(KV_W),
        pl.BlockSpec((1, 1, KV_W, tm), lambda i, j: (i, j, 0, 0)),
        tok(ATT_W), tok(IDX_HEADS * IDX_DIM), tok(IDX_DIM), tok(LANES), tok(LANES),
        tok(D_MODEL), tok(D_MODEL), per_b(CONV_W - 1, D_MODEL),
    )
    in_specs = [
        tok(d), per_b(1, d), per_b(1, d), const(1, d), const(1, LANES), const(1, LANES),
        tab, tab, tab, tab, tab, tab,
        whole, whole, whole, whole, const(CONV_W, d), per_b(CONV_W - 1, d),
    ]
    return pl.pallas_call(
        functools.partial(_proj_kernel, tm=tm, conv_row=conv_row),
        grid=(b, nt),
        in_specs=in_specs,
        out_specs=out_specs,
        out_shape=out_shape,
        scratch_shapes=[pltpu.VMEM((SUBLANES, d), jnp.float32)],
        compiler_params=pltpu.CompilerParams(
            dimension_semantics=("arbitrary", "arbitrary"), vmem_limit_bytes=VMEM_LIMIT),
        name="in_proj",
    )(x, shift, scale, norm_g, qg, kg, *rope_tabs, *w_r, w_pb, conv_w, conv_state)


def _key_tiles(pos0, j, *, l_real, kt_size, qb, minimum=jnp.minimum):
    last_q = pos0 + j * qb + qb - 1
    kmax = minimum(l_real, ((last_q >> CHUNK_SHIFT) + 1) * CHUNK)
    return (kmax + kt_size - 1) // kt_size


def _unrolled(n, body, init):
    carry = init
    for i in range(n):
        carry = body(i, carry)
    return carry


def _attn_kernel(*refs, pos0, l_real, kt_size, topk, qb, nkt_values):
    nkt = _key_tiles(pos0, pl.program_id(1), l_real=l_real, kt_size=kt_size, qb=qb)
    for c in nkt_values:
        run = functools.partial(_attn_block, *refs, pos0=pos0, l_real=l_real, kt_size=kt_size,
                                topk=topk, qb=qb, nkt=c)
        if len(nkt_values) == 1:
            run()
        else:
            pl.when(nkt == c)(run)


def _attn_block(q_ref, qi_ref, wi_ref, k_ref, vt_ref, ki2_ref,
                sza_ref, sga_ref, gbb_ref, x_ref, gate_ref, wpa_ref, wout_ref,
                y_ref, sc_ref, bias_ref, s_ref, acc_ref, o_ref,
                *, pos0, l_real, kt_size, topk, qb, nkt):
    KT = kt_size
    QB = qb
    grp = N_HEADS // N_KV
    vt_w = vt_ref.shape[3]
    vt_per_kt = KT // vt_w
    q0 = pos0 + pl.program_id(1) * QB

    lane = lax.broadcasted_iota(jnp.int32, (1, QB), 1)
    q_chunk = (q0 + lane) >> CHUNK_SHIFT
    w_t = wi_ref[0].T
    qi = qi_ref[0]
    lane_kt = lax.broadcasted_iota(jnp.int32, (KT, LANES), 1)
    sub_kt = lax.broadcasted_iota(jnp.int32, (KT, 1), 0)

    def score_body(kt, carry):
        smax8, smin8 = carry
        k0 = kt * KT
        ki2 = ki2_ref[0, pl.ds(k0, KT), :]
        ki_lo = _bf(jnp.where(lane_kt < IDX_DIM, ki2, 0.0))
        ki_hi = _bf(jnp.where(lane_kt >= IDX_DIM, ki2, 0.0))
        acc = jnp.zeros((KT, QB), jnp.float32)
        for p in range(IDX_HEADS // 2):
            slab = qi[:, p * LANES:(p + 1) * LANES]
            acc = acc + w_t[2 * p:2 * p + 1, :] * jnp.maximum(_dot_nt(ki_lo, slab), 0.0)
            acc = acc + w_t[2 * p + 1:2 * p + 2, :] * jnp.maximum(_dot_nt(ki_hi, slab), 0.0)
        kpos = k0 + sub_kt
        adm = ((kpos >> CHUNK_SHIFT) <= q_chunk) & (kpos < l_real)
        sc_ref[pl.ds(k0, KT), :] = jnp.where(adm, acc, -jnp.inf)
        acc3 = acc.reshape(KT // SUBLANES, SUBLANES, QB)
        return (jnp.maximum(smax8, jnp.max(acc3, axis=0)),
                jnp.minimum(smin8, jnp.min(acc3, axis=0)))

    smax8, smin8 = _unrolled(
        nkt, score_body,
        (jnp.full((SUBLANES, QB), -jnp.inf, jnp.float32),
         jnp.full((SUBLANES, QB), jnp.inf, jnp.float32)))

    def count_ge(cand):
        def body(kt, acc):
            k0 = kt * KT
            m = (sc_ref[pl.ds(k0, KT), :] >= cand).astype(jnp.int32)
            return acc + jnp.sum(m.reshape(KT // SUBLANES, SUBLANES, QB), axis=0)
        acc = _unrolled(nkt, body, jnp.zeros((SUBLANES, QB), jnp.int32))
        return jnp.sum(acc, axis=0, keepdims=True)

    def inside(x, lo, hi):
        return (x > lo) & (x < hi)

    def active_rows(lo, hi, clo):
        return (clo > topk) & inside(lo * 0.5 + hi * 0.5, lo, hi)

    def any_active(st):
        lo, hi, clo, _ = st
        return jnp.max(active_rows(lo, hi, clo).astype(jnp.float32)) > 0.0

    def search_step(it, st):
        lo, hi, clo, chi = st
        act = active_rows(lo, hi, clo)
        mid = lo * 0.5 + hi * 0.5
        near_zero = jnp.where(lo >= 0.0, TINY, -TINY)
        geo = jnp.sqrt(jnp.abs(lo)) * jnp.sqrt(jnp.abs(hi)) * jnp.where(lo >= 0.0, 1.0, -1.0)
        guess = jnp.where(it == 0, 0.0, jnp.where(it == 1, near_zero,
                          jnp.where(it < SEARCH_VALUE_IT, mid, geo)))
        cand = jnp.where(inside(guess, lo, hi), guess, mid)
        cnt = count_ge(cand)
        ge = cnt >= topk
        up_lo = act & ge
        up_hi = act & jnp.logical_not(ge)
        return (jnp.where(up_lo, cand, lo), jnp.where(up_hi, cand, hi),
                jnp.where(up_lo, cnt, clo), jnp.where(up_hi, cnt, chi))

    n_adm = jnp.minimum(l_real, (q_chunk + 1) * CHUNK)
    lo0 = jnp.min(smin8, axis=0, keepdims=True)
    smax = jnp.max(smax8, axis=0, keepdims=True)
    hi0 = smax + jnp.abs(smax) + 1.0
    st = (lo0, hi0, n_adm, jnp.zeros((1, QB), jnp.int32))
    n_first = jnp.where(any_active(st), SEARCH_FIRST_IT, 0)
    st = lax.fori_loop(0, n_first, search_step, st)

    def search_cond(c):
        return (c[0] < SEARCH_MAX_IT) & any_active(c[1])

    def search_body(c):
        it, st = c
        return it + 2, search_step(it + 1, search_step(it, st))

    _, (thr, _, n_ge, n_gt) = lax.while_loop(search_cond, search_body, (n_first, st))
    need = topk - n_gt
    any_tie = jnp.max((n_ge > topk).astype(jnp.float32)) > 0.0

    def bias_body(kt, carry):
        k0 = kt * KT
        bias_ref[pl.ds(k0, KT), :] = jnp.where(sc_ref[pl.ds(k0, KT), :] >= thr, 0.0, NEG_BIG)
        return carry

    _unrolled(nkt, bias_body, 0)

    @pl.when(any_tie)
    def _():
        tri = _bf((lax.broadcasted_iota(jnp.int32, (LANES, LANES), 1)
                   <= lax.broadcasted_iota(jnp.int32, (LANES, LANES), 0)).astype(jnp.float32))
        need_f = need.astype(jnp.float32)
        untied = n_ge <= topk

        def tie_body(kt, seen):
            for r in range(KT // LANES):
                k0 = kt * KT + r * LANES
                sc = sc_ref[pl.ds(k0, LANES), :]
                eq = sc == thr
                eq_f = eq.astype(jnp.float32)
                rank = _dot(tri, _bf(eq_f)) + seen
                sel = (sc > thr) | (eq & ((rank <= need_f) | untied))
                bias_ref[pl.ds(k0, LANES), :] = jnp.where(sel, 0.0, NEG_BIG)
                seen = seen + jnp.sum(eq_f, axis=0, keepdims=True)
            return seen

        _unrolled(nkt, tie_body, jnp.zeros((1, QB), jnp.float32))

    gw = grp * QB
    m8_init = tuple(jnp.full((SUBLANES, QB), NEG_BIG, jnp.float32) for _ in range(grp))
    l8_init = jnp.zeros((SUBLANES, gw), jnp.float32)
    q_groups = [jnp.concatenate(
        [q_ref[0, :, (g * grp + hh) * HEAD_DIM:(g * grp + hh + 1) * HEAD_DIM]
         for hh in range(grp)], axis=0) for g in range(N_KV)]

    def qk_tile(g, kt, m8):
        k0 = kt * KT
        kb = k_ref[0, pl.ds(k0, KT), g * HEAD_DIM:(g + 1) * HEAD_DIM]
        bias = bias_ref[pl.ds(k0, KT), :]
        s = _dot_nt(kb, q_groups[g])
        out = []
        for hh in range(grp):
            sh = s[:, hh * QB:(hh + 1) * QB] + bias
            s_ref[pl.ds(k0, KT), g * gw + hh * QB:g * gw + (hh + 1) * QB] = sh
            out.append(jnp.maximum(
                m8[hh], jnp.max(sh.reshape(KT // SUBLANES, SUBLANES, QB), axis=0)))
        return tuple(out)

    def pv_tile(g, kt, l8, m_all):
        k0 = kt * KT
        p = jnp.exp2(s_ref[pl.ds(k0, KT), g * gw:(g + 1) * gw] - m_all)
        pb = _bf(p)
        acc_ref[...] += sum(
            _dot(vt_ref[0, kt * vt_per_kt + r, g * HEAD_DIM:(g + 1) * HEAD_DIM, :],
                 pb[r * vt_w:(r + 1) * vt_w, :])
            for r in range(vt_per_kt))
        return l8 + jnp.sum(p.reshape(KT // SUBLANES, SUBLANES, gw), axis=0)

    def col_max(m8):
        return jnp.concatenate([jnp.max(m, axis=0, keepdims=True) for m in m8], axis=1)

    def finish_group(g, l8):
        o_t = acc_ref[...] * (1.0 / jnp.sum(l8, axis=0, keepdims=True))
        for hh in range(grp):
            h_abs = g * grp + hh
            o_ref[:, h_abs * HEAD_DIM:(h_abs + 1) * HEAD_DIM] = o_t[:, hh * QB:(hh + 1) * QB].T

    m8 = _unrolled(nkt, functools.partial(qk_tile, 0), m8_init)
    for g in range(N_KV):
        m_all = col_max(m8)
        acc_ref[...] = jnp.zeros((HEAD_DIM, gw), jnp.float32)
        if g + 1 < N_KV:
            def both(kt, carry, g=g, m_all=m_all):
                return pv_tile(g, kt, carry[0], m_all), qk_tile(g + 1, kt, carry[1])
            l8, m8 = _unrolled(nkt, both, (l8_init, m8_init))
        else:
            l8 = _unrolled(nkt, lambda kt, l8, g=g, m_all=m_all: pv_tile(g, kt, l8, m_all),
                           l8_init)
        finish_group(g, l8)

    a_out = _dot(_bf(o_ref[...] * sza_ref[0]), wpa_ref[...])
    merged = sga_ref[0] * a_out + gbb_ref[0]
    y_ref[0] = x_ref[0] + gate_ref[0] * _dot(_bf(merged), wout_ref[...])


def _attn_call(q, qi, wi, k_all, vt_all, ki2_all, sza, sga, gbb, x, gate, w_pa, w_out,
               *, pos0, l_real, kt_size, topk, qb):
    b, t, d = x.shape
    lp = k_all.shape[1]
    nkt_values = sorted({_key_tiles(pos0, jj, l_real=l_real, kt_size=kt_size, qb=qb, minimum=min)
                         for jj in range(t // qb)})
    tok = lambda w: pl.BlockSpec((1, qb, w), lambda i, j: (i, j, 0))
    per_b = lambda r, w: pl.BlockSpec((1, r, w), lambda i, j: (i, 0, 0))
    whole = pl.BlockSpec(memory_space=pltpu.VMEM)
    in_specs = [
        tok(ATT_W), tok(IDX_HEADS * IDX_DIM), tok(LANES),
        per_b(lp, KV_W),
        pl.BlockSpec((1,) + vt_all.shape[1:], lambda i, j: (i, 0, 0, 0)),
        per_b(lp, LANES),
        tok(ATT_W), tok(D_MODEL), tok(D_MODEL), tok(d), per_b(1, d),
        whole, whole,
    ]
    return pl.pallas_call(
        functools.partial(_attn_kernel, pos0=pos0, l_real=l_real, kt_size=kt_size, topk=topk,
                          qb=qb, nkt_values=nkt_values),
        grid=(b, t // qb),
        in_specs=in_specs,
        out_specs=tok(d),
        out_shape=jax.ShapeDtypeStruct((b, t, d), jnp.float32),
        scratch_shapes=[
            pltpu.VMEM((lp, qb), jnp.float32),
            pltpu.VMEM((lp, qb), jnp.float32),
            pltpu.VMEM((lp, N_HEADS * qb), jnp.float32),
            pltpu.VMEM((HEAD_DIM, N_HEADS // N_KV * qb), jnp.float32),
            pltpu.VMEM((qb, ATT_W), jnp.float32),
        ],
        compiler_params=pltpu.CompilerParams(
            dimension_semantics=("arbitrary", "arbitrary"), vmem_limit_bytes=VMEM_LIMIT),
        name="dsa_attn",
    )(q, qi, wi, k_all, vt_all, ki2_all, sza, sga, gbb, x, gate, w_pa, w_out)


def _rope_tables(pos, width, period):
    rot = period // ROT_FRAC
    half = rot // 2
    inv = ROPE_THETA ** (-2.0 * jnp.arange(half, dtype=jnp.float32) / rot)
    ang = pos.astype(jnp.float32)[:, None] * inv[None, :]
    cos, sin = jnp.cos(ang), jnp.sin(ang)
    n = pos.shape[0]
    rest = period - rot
    one = jnp.ones((n, rest), jnp.float32)
    zero_h = jnp.zeros((n, half), jnp.float32)
    zero_r = jnp.zeros((n, rest), jnp.float32)
    c = jnp.concatenate([cos, cos, one], axis=1)
    s_up = jnp.concatenate([-sin, zero_h, zero_r], axis=1)
    s_dn = jnp.concatenate([zero_h, sin, zero_r], axis=1)
    rep = width // period
    return tuple(jnp.tile(a, (1, rep)) for a in (c, s_up, s_dn))


def _prep_w_in(w_in):
    o_ki = ATT_W + 2 * KV_W + ATT_W + IDX_HEADS * IDX_DIM
    o_wi = o_ki + IDX_DIM
    o_u = o_wi + IDX_HEADS
    ki = w_in[:, o_ki:o_wi]
    pad = jnp.zeros((w_in.shape[0], C_END - C_WI - IDX_HEADS), w_in.dtype)
    w_c = jnp.concatenate([ki, ki, w_in[:, o_wi:o_u], pad], axis=1)
    return _bf(w_in[:, :o_ki]), _bf(w_in[:, o_u:]), _bf(w_c)


def _layer(x, shift, scale, gate, pos, conv_state, past, weights, *, tm, t_real, kt_size, qb):
    norm_g, qg, kg, w_r, w_pb, conv_w, w_pa, w_out = weights
    b, t, _ = x.shape
    tabs = _rope_tables(pos, LANES, HEAD_DIM) + _rope_tables(pos, LANES, IDX_DIM)
    conv_row = t_real - (t // tm - 1) * tm
    (q, k, v, kb, vt, sza, qi, ki, ki2, wi, sga, gbb, conv_new) = _proj_call(
        x, shift, scale, norm_g, qg, kg, tabs, w_r, w_pb, conv_w, conv_state,
        tm=tm, conv_row=conv_row)
    if past is None:
        p_len = 0
        k_all, vt_all, ki2_all = kb, vt, ki2
    else:
        past_k, past_v, past_ki = past
        p_len = past_k.shape[1]
        lp = -(-(p_len + t) // kt_size) * kt_size
        extra = lp - p_len - t
        k_all = jnp.concatenate(
            [_bf(past_k), kb, jnp.zeros((b, extra, KV_W), jnp.bfloat16)], axis=1)
        vt_all = jnp.concatenate(
            [_bf(jnp.swapaxes(past_v, 1, 2)), vt[:, 0], jnp.zeros((b, KV_W, extra), jnp.bfloat16)],
            axis=2)
        vt_all = jnp.swapaxes(vt_all.reshape(b, KV_W, lp // kt_size, kt_size), 1, 2)
        ki2_all = jnp.concatenate(
            [jnp.concatenate([past_ki, past_ki], axis=-1), ki2,
             jnp.zeros((b, extra, LANES), jnp.float32)], axis=1)
    l_real = p_len + t_real
    topk = min(TOPK_MAX, l_real // 4)
    y = _attn_call(q, qi, wi, k_all, vt_all, ki2_all, sza, sga, gbb, x, gate, w_pa, w_out,
                   pos0=p_len, l_real=l_real, kt_size=kt_size, topk=topk, qb=qb)
    return y, k, v, ki, conv_new


def kernel(x_prompt, x_sample, cache_k, cache_v, cache_idx_k, state_conv, c_prompt, c_sample,
           w_ada, b_ada, norm_g, w_in, q_norm_g, k_norm_g, conv_w, w_pa, w_pb, w_out):
    bp, seq, d = x_prompt.shape
    bs, dec_seq, _ = x_sample.shape
    past_len = cache_k.shape[1]

    mod = _mod_call(jnp.concatenate([c_prompt, c_sample], axis=0), w_ada, b_ada)
    shift, scale, gate = (m.reshape(bp + bs, 1, d) for m in jnp.split(mod, 3, axis=-1))

    weights = (norm_g.reshape(1, d), q_norm_g.reshape(1, HEAD_DIM), k_norm_g.reshape(1, HEAD_DIM),
               _prep_w_in(w_in), _bf(w_pb), conv_w, _bf(w_pa), _bf(w_out))

    yp, k_p, v_p, ki_p, conv_p = _layer(
        x_prompt, shift[:bp], scale[:bp], gate[:bp], jnp.arange(seq, dtype=jnp.int32),
        jnp.zeros((bp, CONV_W - 1, d), x_prompt.dtype), None, weights,
        tm=256, t_real=seq, kt_size=256, qb=QB_PROMPT)

    x_pad = jnp.pad(x_sample, ((0, 0), (0, QB_SAMPLE - dec_seq), (0, 0)))
    past = (cache_k.reshape(bs, past_len, KV_W), cache_v.reshape(bs, past_len, KV_W), cache_idx_k)
    ys, k_s, v_s, ki_s, conv_s = _layer(
        x_pad, shift[bp:], scale[bp:], gate[bp:],
        past_len + jnp.arange(QB_SAMPLE, dtype=jnp.int32), state_conv, past, weights,
        tm=QB_SAMPLE, t_real=dec_seq, kt_size=384, qb=QB_SAMPLE)

    return (yp, ys[:, :dec_seq], k_p, v_p, ki_p, conv_p,
            k_s[:, :dec_seq], v_s[:, :dec_seq], ki_s[:, :dec_seq], conv_s)
```

```python
import functools
import math

import jax
import jax.numpy as jnp
from jax import lax
from jax.experimental import pallas as pl
from jax.experimental.pallas import tpu as pltpu

D_MODEL = 1024
CHUNK = 64
CHUNK_SHIFT = 6
N_HEADS = 8
N_KV = 2
HEAD_DIM = 128
ATT_W = N_HEADS * HEAD_DIM
KV_W = N_KV * HEAD_DIM
ROT_FRAC = 4
ROPE_THETA = 500000.0
IDX_HEADS = 8
IDX_DIM = 64
TOPK_MAX = 256
CONV_W = 3
EPS = 1e-6

LANES = 128
SUBLANES = 8
QB_PROMPT = 256
QB_SAMPLE = 128
Q_SCALE = HEAD_DIM ** -0.5 * math.log2(math.e)
VMEM_LIMIT = 56 * 1024 * 1024

C_Q, C_K, C_V, C_ZA, C_QI = 0, 1024, 1280, 1536, 2560
C_U, C_BG, C_CG, C_ZB, C_GA, C_GB = 3072, 4096, 5120, 6144, 7168, 8192
C_KI, C_WI, C_END = 9216, 9344, 9472

NEG_BIG = -1e30
TINY = 1.1754943508222875e-38
SEARCH_VALUE_IT = 16
SEARCH_FIRST_IT = 18
SEARCH_MAX_IT = SEARCH_VALUE_IT + 36

_NT = (((1,), (1,)), ((), ()))


def _bf(x):
    return x.astype(jnp.bfloat16)


def _dot(a, b):
    return jnp.dot(a, b, preferred_element_type=jnp.float32)


def _dot_nt(a, b):
    return lax.dot_general(a, b, _NT, preferred_element_type=jnp.float32)


def _silu(x):
    return x * jax.nn.sigmoid(x)


def _mod_kernel(c_ref, w_ref, b_ref, o_ref):
    c = c_ref[...]
    s = _silu(c)
    w = w_ref[...]
    s_hi = _bf(s)
    s_lo = _bf(s - s_hi.astype(jnp.float32))
    w_hi = _bf(w)
    w_lo = _bf(w - w_hi.astype(jnp.float32))
    acc = _dot(s_hi, w_hi) + (_dot(s_lo, w_hi) + _dot(s_hi, w_lo))
    o_ref[...] = acc + b_ref[...]


def _mod_call(c_all, w_ada, b_ada):
    nb, d = c_all.shape
    n = w_ada.shape[1]
    bn = 1024
    return pl.pallas_call(
        _mod_kernel,
        grid=(n // bn,),
        in_specs=[
            pl.BlockSpec((nb, d), lambda i: (0, 0)),
            pl.BlockSpec((d, bn), lambda i: (0, i)),
            pl.BlockSpec((1, bn), lambda i: (0, i)),
        ],
        out_specs=pl.BlockSpec((nb, bn), lambda i: (0, i)),
        out_shape=jax.ShapeDtypeStruct((nb, n), jnp.float32),
        compiler_params=pltpu.CompilerParams(dimension_semantics=("arbitrary",)),
        name="adaln_mod",
    )(c_all, w_ada, b_ada.reshape(1, n))


def _rope(x, cos, sin_up, sin_dn, half):
    up = pltpu.roll(x, LANES - half, 1)
    dn = pltpu.roll(x, half, 1)
    return x * cos + up * sin_up + dn * sin_dn


def _proj_kernel(x_ref, shift_ref, scale_ref, ng_ref, qg_ref, kg_ref,
                 rc_ref, rs1_ref, rs2_ref, ic_ref, is1_ref, is2_ref,
                 wa_ref, wb_ref, wc_ref, wpb_ref, cw_ref, cs_ref,
                 q_ref, k_ref, v_ref, kb_ref, vt_ref, sza_ref, qi_ref, ki_ref, ki2_ref, wi_ref,
                 sga_ref, gbb_ref, conv_ref, carry_ref, *, tm, conv_row):
    t = pl.program_id(1)
    x = x_ref[0]
    ms = jnp.mean(x * x, axis=-1, keepdims=True)
    xn = x * lax.rsqrt(ms + EPS) * ng_ref[...]
    h = xn * (1.0 + scale_ref[0]) + shift_ref[0]
    hb = _bf(h)

    def proj(c0, c1):
        for w_ref, base in ((wc_ref, C_KI), (wb_ref, C_U), (wa_ref, C_Q)):
            if c0 >= base:
                return _dot(hb, w_ref[:, c0 - base:c1 - base])

    rc, rs1, rs2 = rc_ref[...], rs1_ref[...], rs2_ref[...]
    ic, is1, is2 = ic_ref[...], is1_ref[...], is2_ref[...]
    rot_half = HEAD_DIM // ROT_FRAC // 2
    idx_half = IDX_DIM // ROT_FRAC // 2

    def head_norm_rope(xh, g):
        r = lax.rsqrt(jnp.mean(xh * xh, axis=-1, keepdims=True) + EPS)
        return _rope(xh * r * g, rc, rs1, rs2, rot_half)

    q = proj(C_Q, C_K)
    qg = qg_ref[...]
    for hh in range(N_HEADS):
        sl = slice(hh * HEAD_DIM, (hh + 1) * HEAD_DIM)
        q_ref[0, :, sl] = _bf(head_norm_rope(q[:, sl], qg) * Q_SCALE)

    k = proj(C_K, C_V)
    kg = kg_ref[...]
    v = proj(C_V, C_ZA)
    for hh in range(N_KV):
        sl = slice(hh * HEAD_DIM, (hh + 1) * HEAD_DIM)
        kh = head_norm_rope(k[:, sl], kg)
        k_ref[0, :, hh, :] = kh
        kb_ref[0, :, sl] = _bf(kh)
        v_ref[0, :, hh, :] = v[:, sl]
    vt_ref[0, 0] = _bf(v.T)

    sza_ref[0] = _silu(proj(C_ZA, C_QI))

    qi = proj(C_QI, C_U)
    for p in range(IDX_HEADS * IDX_DIM // LANES):
        sl = slice(p * LANES, (p + 1) * LANES)
        qi_ref[0, :, sl] = _bf(_rope(qi[:, sl], ic, is1, is2, idx_half) * (IDX_DIM ** -0.5))

    ki2 = _rope(proj(C_KI, C_WI), ic, is1, is2, idx_half)
    ki2_ref[0] = ki2
    ki_ref[0] = ki2[:, :IDX_DIM]
    wi_ref[0] = proj(C_WI, C_END) * (IDX_HEADS ** -0.5)

    cv = proj(C_CG, C_ZB) * proj(C_U, C_BG)

    @pl.when(t == 0)
    def _():
        carry_ref[0:2, :] = cs_ref[0]

    c0 = carry_ref[0:1, :]
    c1 = carry_ref[1:2, :]
    row = lax.broadcasted_iota(jnp.int32, (tm, 1), 0)
    r1 = jnp.where(row == 0, c1, pltpu.roll(cv, 1, 0))
    r2 = jnp.where(row == 0, c0, jnp.where(row == 1, c1, pltpu.roll(cv, 2, 0)))
    cw = cw_ref[...]
    y_conv = cw[0:1, :] * r2 + cw[1:2, :] * r1 + cw[2:3, :] * cv
    carry_ref[0:2, :] = cv[tm - 2:tm, :]
    conv_ref[0] = cv[conv_row - 2:conv_row, :]

    tb = proj(C_BG, C_CG) * y_conv * _silu(proj(C_ZB, C_GA))
    b_out = _dot(_bf(tb), wpb_ref[...])
    gbb_ref[0] = jax.nn.sigmoid(proj(C_GB, C_KI)) * b_out
    sga_ref[0] = jax.nn.sigmoid(proj(C_GA, C_GB))


def _proj_call(x, shift, scale, norm_g, qg, kg, rope_tabs, w_r, w_pb, conv_w, conv_state,
               *, tm, conv_row):
    b, t, d = x.shape
    nt = t // tm
    tok = lambda w: pl.BlockSpec((1, tm, w), lambda i, j: (i, j, 0))
    per_b = lambda r, w: pl.BlockSpec((1, r, w), lambda i, j: (i, 0, 0))
    const = lambda r, w: pl.BlockSpec((r, w), lambda i, j: (0, 0))
    tab = pl.BlockSpec((tm, LANES), lambda i, j: (j, 0))
    kv4 = pl.BlockSpec((1, tm, N_KV, HEAD_DIM), lambda i, j: (i, j, 0, 0))
    whole = pl.BlockSpec(memory_space=pltpu.VMEM)
    f32, bf16 = jnp.float32, jnp.bfloat16
    out_shape = (
        jax.ShapeDtypeStruct((b, t, ATT_W), bf16),
        jax.ShapeDtypeStruct((b, t, N_KV, HEAD_DIM), f32),
        jax.ShapeDtypeStruct((b, t, N_KV, HEAD_DIM), f32),
        jax.ShapeDtypeStruct((b, t, KV_W), bf16),
        jax.ShapeDtypeStruct((b, nt, KV_W, tm), bf16),
        jax.ShapeDtypeStruct((b, t, ATT_W), f32),
        jax.ShapeDtypeStruct((b, t, IDX_HEADS * IDX_DIM), bf16),
        jax.ShapeDtypeStruct((b, t, IDX_DIM), f32),
        jax.ShapeDtypeStruct((b, t, LANES), f32),
        jax.ShapeDtypeStruct((b, t, LANES), f32),
        jax.ShapeDtypeStruct((b, t, D_MODEL), f32),
        jax.ShapeDtypeStruct((b, t, D_MODEL), f32),
        jax.ShapeDtypeStruct((b, CONV_W - 1, D_MODEL), f32),
    )
    out_specs = (
        tok(ATT_W), kv4, kv4, tok(KV_W),
        pl.BlockSpec((1, 1, KV_W, tm), lambda i, j: (i, j, 0, 0)),
        tok(ATT_W), tok(IDX_HEADS * IDX_DIM), tok(IDX_DIM), tok(LANES), tok(LANES),
        tok(D_MODEL), tok(D_MODEL), per_b(CONV_W - 1, D_MODEL),
    )
    in_specs = [
        tok(d), per_b(1, d), per_b(1, d), const(1, d), const(1, LANES), const(1, LANES),
        tab, tab, tab, tab, tab, tab,
        whole, whole, whole, whole, const(CONV_W, d), per_b(CONV_W - 1, d),
    ]
    return pl.pallas_call(
        functools.partial(_proj_kernel, tm=tm, conv_row=conv_row),
        grid=(b, nt),
        in_specs=in_specs,
        out_specs=out_specs,
        out_shape=out_shape,
        scratch_shapes=[pltpu.VMEM((SUBLANES, d), jnp.float32)],
        compiler_params=pltpu.CompilerParams(
            dimension_semantics=("arbitrary", "arbitrary"), vmem_limit_bytes=VMEM_LIMIT),
        name="in_proj",
    )(x, shift, scale, norm_g, qg, kg, *rope_tabs, *w_r, w_pb, conv_w, conv_state)


def _key_tiles(pos0, j, *, l_real, kt_size, qb, minimum=jnp.minimum):
    last_q = pos0 + j * qb + qb - 1
    kmax = minimum(l_real, ((last_q >> CHUNK_SHIFT) + 1) * CHUNK)
    return (kmax + kt_size - 1) // kt_size


def _unrolled(n, body, init):
    carry = init
    for i in range(n):
        carry = body(i, carry)
    return carry


def _attn_kernel(*refs, pos0, l_real, kt_size, topk, qb, nkt_values):
    nkt = _key_tiles(pos0, pl.program_id(1), l_real=l_real, kt_size=kt_size, qb=qb)
    for c in nkt_values:
        run = functools.partial(_attn_block, *refs, pos0=pos0, l_real=l_real, kt_size=kt_size,
                                topk=topk, qb=qb, nkt=c)
        if len(nkt_values) == 1:
            run()
        else:
            pl.when(nkt == c)(run)


def _attn_block(q_ref, qi_ref, wi_ref, k_ref, vt_ref, ki2_ref,
                sza_ref, sga_ref, gbb_ref, x_ref, gate_ref, wpa_ref, wout_ref,
                y_ref, sc_ref, bias_ref, s_ref, acc_ref, o_ref,
                *, pos0, l_real, kt_size, topk, qb, nkt):
    KT = kt_size
    QB = qb
    grp = N_HEADS // N_KV
    vt_w = vt_ref.shape[3]
    vt_per_kt = KT // vt_w
    q0 = pos0 + pl.program_id(1) * QB

    lane = lax.broadcasted_iota(jnp.int32, (1, QB), 1)
    q_chunk = (q0 + lane) >> CHUNK_SHIFT
    w_t = wi_ref[0].T
    qi = qi_ref[0]
    lane_kt = lax.broadcasted_iota(jnp.int32, (KT, LANES), 1)
    sub_kt = lax.broadcasted_iota(jnp.int32, (KT, 1), 0)

    def score_body(kt, carry):
        smax8, smin8 = carry
        k0 = kt * KT
        ki2 = ki2_ref[0, pl.ds(k0, KT), :]
        ki_lo = _bf(jnp.where(lane_kt < IDX_DIM, ki2, 0.0))
        ki_hi = _bf(jnp.where(lane_kt >= IDX_DIM, ki2, 0.0))
        acc = jnp.zeros((KT, QB), jnp.float32)
        for p in range(IDX_HEADS // 2):
            slab = qi[:, p * LANES:(p + 1) * LANES]
            acc = acc + w_t[2 * p:2 * p + 1, :] * jnp.maximum(_dot_nt(ki_lo, slab), 0.0)
            acc = acc + w_t[2 * p + 1:2 * p + 2, :] * jnp.maximum(_dot_nt(ki_hi, slab), 0.0)
        kpos = k0 + sub_kt
        adm = ((kpos >> CHUNK_SHIFT) <= q_chunk) & (kpos < l_real)
        sc_ref[pl.ds(k0, KT), :] = jnp.where(adm, acc, -jnp.inf)
        acc3 = acc.reshape(KT // SUBLANES, SUBLANES, QB)
        return (jnp.maximum(smax8, jnp.max(acc3, axis=0)),
                jnp.minimum(smin8, jnp.min(acc3, axis=0)))

    smax8, smin8 = _unrolled(
        nkt, score_body,
        (jnp.full((SUBLANES, QB), -jnp.inf, jnp.float32),
         jnp.full((SUBLANES, QB), jnp.inf, jnp.float32)))

    def count_ge(cand):
        def body(kt, acc):
            k0 = kt * KT
            m = (sc_ref[pl.ds(k0, KT), :] >= cand).astype(jnp.int32)
            return acc + jnp.sum(m.reshape(KT // SUBLANES, SUBLANES, QB), axis=0)
        acc = _unrolled(nkt, body, jnp.zeros((SUBLANES, QB), jnp.int32))
        return jnp.sum(acc, axis=0, keepdims=True)

    def inside(x, lo, hi):
        return (x > lo) & (x < hi)

    def active_rows(lo, hi, clo):
        return (clo > topk) & inside(lo * 0.5 + hi * 0.5, lo, hi)

    def any_active(st):
        lo, hi, clo, _ = st
        return jnp.max(active_rows(lo, hi, clo).astype(jnp.float32)) > 0.0

    def search_step(it, st):
        lo, hi, clo, chi = st
        act = active_rows(lo, hi, clo)
        mid = lo * 0.5 + hi * 0.5
        near_zero = jnp.where(lo >= 0.0, TINY, -TINY)
        geo = jnp.sqrt(jnp.abs(lo)) * jnp.sqrt(jnp.abs(hi)) * jnp.where(lo >= 0.0, 1.0, -1.0)
        guess = jnp.where(it == 0, 0.0, jnp.where(it == 1, near_zero,
                          jnp.where(it < SEARCH_VALUE_IT, mid, geo)))
        cand = jnp.where(inside(guess, lo, hi), guess, mid)
        cnt = count_ge(cand)
        ge = cnt >= topk
        up_lo = act & ge
        up_hi = act & jnp.logical_not(ge)
        return (jnp.where(up_lo, cand, lo), jnp.where(up_hi, cand, hi),
                jnp.where(up_lo, cnt, clo), jnp.where(up_hi, cnt, chi))

    n_adm = jnp.minimum(l_real, (q_chunk + 1) * CHUNK)
    lo0 = jnp.min(smin8, axis=0, keepdims=True)
    smax = jnp.max(smax8, axis=0, keepdims=True)
    hi0 = smax + jnp.abs(smax) + 1.0
    st = (lo0, hi0, n_adm, jnp.zeros((1, QB), jnp.int32))
    n_first = jnp.where(any_active(st), SEARCH_FIRST_IT, 0)
    st = lax.fori_loop(0, n_first, search_step, st)

    def search_cond(c):
        return (c[0] < SEARCH_MAX_IT) & any_active(c[1])

    def search_body(c):
        it, st = c
        return it + 2, search_step(it + 1, search_step(it, st))

    _, (thr, _, n_ge, n_gt) = lax.while_loop(search_cond, search_body, (n_first, st))
    need = topk - n_gt
    any_tie = jnp.max((n_ge > topk).astype(jnp.float32)) > 0.0

    def bias_body(kt, carry):
        k0 = kt * KT
        bias_ref[pl.ds(k0, KT), :] = jnp.where(sc_ref[pl.ds(k0, KT), :] >= thr, 0.0, NEG_BIG)
        return carry

    _unrolled(nkt, bias_body, 0)

    @pl.when(any_tie)
    def _():
        tri = _bf((lax.broadcasted_iota(jnp.int32, (LANES, LANES), 1)
                   <= lax.broadcasted_iota(jnp.int32, (LANES, LANES), 0)).astype(jnp.float32))
        need_f = need.astype(jnp.float32)
        untied = n_ge <= topk

        def tie_body(kt, seen):
            for r in range(KT // LANES):
                k0 = kt * KT + r * LANES
                sc = sc_ref[pl.ds(k0, LANES), :]
                eq = sc == thr
                eq_f = eq.astype(jnp.float32)
                rank = _dot(tri, _bf(eq_f)) + seen
                sel = (sc > thr) | (eq & ((rank <= need_f) | untied))
                bias_ref[pl.ds(k0, LANES), :] = jnp.where(sel, 0.0, NEG_BIG)
                seen = seen + jnp.sum(eq_f, axis=0, keepdims=True)
            return seen

        _unrolled(nkt, tie_body, jnp.zeros((1, QB), jnp.float32))

    gw = grp * QB
    m8_init = tuple(jnp.full((SUBLANES, QB), NEG_BIG, jnp.float32) for _ in range(grp))
    l8_init = jnp.zeros((SUBLANES, gw), jnp.float32)
    q_groups = [jnp.concatenate(
        [q_ref[0, :, (g * grp + hh) * HEAD_DIM:(g * grp + hh + 1) * HEAD_DIM]
         for hh in range(grp)], axis=0) for g in range(N_KV)]

    def qk_tile(g, kt, m8):
        k0 = kt * KT
        kb = k_ref[0, pl.ds(k0, KT), g * HEAD_DIM:(g + 1) * HEAD_DIM]
        bias = bias_ref[pl.ds(k0, KT), :]
        s = _dot_nt(kb, q_groups[g])
        out = []
        for hh in range(grp):
            sh = s[:, hh * QB:(hh + 1) * QB] + bias
            s_ref[pl.ds(k0, KT), g * gw + hh * QB:g * gw + (hh + 1) * QB] = sh
            out.append(jnp.maximum(
                m8[hh], jnp.max(sh.reshape(KT // SUBLANES, SUBLANES, QB), axis=0)))
        return tuple(out)

    def pv_tile(g, kt, l8, m_all):
        k0 = kt * KT
        p = jnp.exp2(s_ref[pl.ds(k0, KT), g * gw:(g + 1) * gw] - m_all)
        pb = _bf(p)
        acc_ref[...] += sum(
            _dot(vt_ref[0, kt * vt_per_kt + r, g * HEAD_DIM:(g + 1) * HEAD_DIM, :],
                 pb[r * vt_w:(r + 1) * vt_w, :])
            for r in range(vt_per_kt))
        return l8 + jnp.sum(p.reshape(KT // SUBLANES, SUBLANES, gw), axis=0)

    def col_max(m8):
        return jnp.concatenate([jnp.max(m, axis=0, keepdims=True) for m in m8], axis=1)

    def finish_group(g, l8):
        o_t = acc_ref[...] * (1.0 / jnp.sum(l8, axis=0, keepdims=True))
        for hh in range(grp):
            h_abs = g * grp + hh
            o_ref[:, h_abs * HEAD_DIM:(h_abs + 1) * HEAD_DIM] = o_t[:, hh * QB:(hh + 1) * QB].T

    m8 = _unrolled(nkt, functools.partial(qk_tile, 0), m8_init)
    for g in range(N_KV):
        m_all = col_max(m8)
        acc_ref[...] = jnp.zeros((HEAD_DIM, gw), jnp.float32)
        if g + 1 < N_KV:
            def both(kt, carry, g=g, m_all=m_all):
                return pv_tile(g, kt, carry[0], m_all), qk_tile(g + 1, kt, carry[1])
            l8, m8 = _unrolled(nkt, both, (l8_init, m8_init))
        else:
            l8 = _unrolled(nkt, lambda kt, l8, g=g, m_all=m_all: pv_tile(g, kt, l8, m_all),
                           l8_init)
        finish_group(g, l8)

    a_out = _dot(_bf(o_ref[...] * sza_ref[0]), wpa_ref[...])
    merged = sga_ref[0] * a_out + gbb_ref[0]
    y_ref[0] = x_ref[0] + gate_ref[0] * _dot(_bf(merged), wout_ref[...])


def _attn_call(q, qi, wi, k_all, vt_all, ki2_all, sza, sga, gbb, x, gate, w_pa, w_out,
               *, pos0, l_real, kt_size, topk, qb):
    b, t, d = x.shape
    lp = k_all.shape[1]
    nkt_values = sorted({_key_tiles(pos0, jj, l_real=l_real, kt_size=kt_size, qb=qb, minimum=min)
                         for jj in range(t // qb)})
    tok = lambda w: pl.BlockSpec((1, qb, w), lambda i, j: (i, j, 0))
    per_b = lambda r, w: pl.BlockSpec((1, r, w), lambda i, j: (i, 0, 0))
    whole = pl.BlockSpec(memory_space=pltpu.VMEM)
    in_specs = [
        tok(ATT_W), tok(IDX_HEADS * IDX_DIM), tok(LANES),
        per_b(lp, KV_W),
        pl.BlockSpec((1,) + vt_all.shape[1:], lambda i, j: (i, 0, 0, 0)),
        per_b(lp, LANES),
        tok(ATT_W), tok(D_MODEL), tok(D_MODEL), tok(d), per_b(1, d),
        whole, whole,
    ]
    return pl.pallas_call(
        functools.partial(_attn_kernel, pos0=pos0, l_real=l_real, kt_size=kt_size, topk=topk,
                          qb=qb, nkt_values=nkt_values),
        grid=(b, t // qb),
        in_specs=in_specs,
        out_specs=tok(d),
        out_shape=jax.ShapeDtypeStruct((b, t, d), jnp.float32),
        scratch_shapes=[
            pltpu.VMEM((lp, qb), jnp.float32),
            pltpu.VMEM((lp, qb), jnp.float32),
            pltpu.VMEM((lp, N_HEADS * qb), jnp.float32),
            pltpu.VMEM((HEAD_DIM, N_HEADS // N_KV * qb), jnp.float32),
            pltpu.VMEM((qb, ATT_W), jnp.float32),
        ],
        compiler_params=pltpu.CompilerParams(
            dimension_semantics=("arbitrary", "arbitrary"), vmem_limit_bytes=VMEM_LIMIT),
        name="dsa_attn",
    )(q, qi, wi, k_all, vt_all, ki2_all, sza, sga, gbb, x, gate, w_pa, w_out)


def _rope_tables(pos, width, period):
    rot = period // ROT_FRAC
    half = rot // 2
    idx = jnp.arange(width, dtype=jnp.int32) % period
    in_rot = (idx < rot)[None, :]
    low = (idx < half)[None, :]
    inv = ROPE_THETA ** (-2.0 * (idx % half).astype(jnp.float32) / rot)
    ang = pos.astype(jnp.float32)[:, None] * inv[None, :]
    cos, sin = jnp.cos(ang), jnp.sin(ang)
    c = jnp.where(in_rot, cos, 1.0)
    s_up = jnp.where(in_rot & low, -sin, 0.0)
    s_dn = jnp.where(in_rot & jnp.logical_not(low), sin, 0.0)
    return c, s_up, s_dn


def _shift_cast_kernel(a_ref, b_ref, o_ref, *, shift):
    x = jnp.concatenate([a_ref[...], b_ref[...]], axis=1)
    o_ref[...] = _bf(x[:, shift:shift + o_ref.shape[1]])


def _shift_cast_call(w, col0, ncols):
    rows = w.shape[0]
    br, bc = 256, 1024
    base, shift = divmod(col0, LANES)
    assert (base * LANES) % bc == 0 and ncols % bc == 0 and rows % br == 0
    first = base * LANES // bc
    return pl.pallas_call(
        functools.partial(_shift_cast_kernel, shift=shift),
        grid=(rows // br, ncols // bc),
        in_specs=[
            pl.BlockSpec((br, bc), lambda i, j: (i, j + first)),
            pl.BlockSpec((br, LANES), lambda i, j: (i, (j + first + 1) * (bc // LANES))),
        ],
        out_specs=pl.BlockSpec((br, bc), lambda i, j: (i, j)),
        out_shape=jax.ShapeDtypeStruct((rows, ncols), jnp.bfloat16),
        compiler_params=pltpu.CompilerParams(dimension_semantics=("arbitrary", "arbitrary")),
        name="w_shift_cast",
    )(w, w)


def _prep_w_in(w_in):
    o_ki = ATT_W + 2 * KV_W + ATT_W + IDX_HEADS * IDX_DIM
    o_wi = o_ki + IDX_DIM
    o_u = o_wi + IDX_HEADS
    ki = w_in[:, o_ki:o_wi]
    pad = jnp.zeros((w_in.shape[0], C_END - C_WI - IDX_HEADS), w_in.dtype)
    w_c = jnp.concatenate([ki, ki, w_in[:, o_wi:o_u], pad], axis=1)
    return _bf(w_in[:, :o_ki]), _shift_cast_call(w_in, o_u, w_in.shape[1] - o_u), _bf(w_c)


def _layer(x, shift, scale, gate, pos, conv_state, past, weights, *, tm, t_real, kt_size, qb):
    norm_g, qg, kg, w_r, w_pb, conv_w, w_pa, w_out = weights
    b, t, _ = x.shape
    tabs = _rope_tables(pos, LANES, HEAD_DIM) + _rope_tables(pos, LANES, IDX_DIM)
    conv_row = t_real - (t // tm - 1) * tm
    (q, k, v, kb, vt, sza, qi, ki, ki2, wi, sga, gbb, conv_new) = _proj_call(
        x, shift, scale, norm_g, qg, kg, tabs, w_r, w_pb, conv_w, conv_state,
        tm=tm, conv_row=conv_row)
    if past is None:
        p_len = 0
        k_all, vt_all, ki2_all = kb, vt, ki2
    else:
        past_k, past_v, past_ki = past
        p_len = past_k.shape[1]
        lp = -(-(p_len + t) // kt_size) * kt_size
        extra = lp - p_len - t
        k_all = jnp.concatenate(
            [_bf(past_k), kb, jnp.zeros((b, extra, KV_W), jnp.bfloat16)], axis=1)
        vt_all = jnp.concatenate(
            [_bf(jnp.swapaxes(past_v, 1, 2)), vt[:, 0], jnp.zeros((b, KV_W, extra), jnp.bfloat16)],
            axis=2)
        vt_all = jnp.swapaxes(vt_all.reshape(b, KV_W, lp // kt_size, kt_size), 1, 2)
        ki2_all = jnp.concatenate(
            [jnp.concatenate([past_ki, past_ki], axis=-1), ki2,
             jnp.zeros((b, extra, LANES), jnp.float32)], axis=1)
    l_real = p_len + t_real
    topk = min(TOPK_MAX, l_real // 4)
    y = _attn_call(q, qi, wi, k_all, vt_all, ki2_all, sza, sga, gbb, x, gate, w_pa, w_out,
                   pos0=p_len, l_real=l_real, kt_size=kt_size, topk=topk, qb=qb)
    return y, k, v, ki, conv_new


def kernel(x_prompt, x_sample, cache_k, cache_v, cache_idx_k, state_conv, c_prompt, c_sample,
           w_ada, b_ada, norm_g, w_in, q_norm_g, k_norm_g, conv_w, w_pa, w_pb, w_out):
    bp, seq, d = x_prompt.shape
    bs, dec_seq, _ = x_sample.shape
    past_len = cache_k.shape[1]

    mod = _mod_call(jnp.concatenate([c_prompt, c_sample], axis=0), w_ada, b_ada)
    shift, scale, gate = (m.reshape(bp + bs, 1, d) for m in jnp.split(mod, 3, axis=-1))

    weights = (norm_g.reshape(1, d), q_norm_g.reshape(1, HEAD_DIM), k_norm_g.reshape(1, HEAD_DIM),
               _prep_w_in(w_in), _bf(w_pb), conv_w, _bf(w_pa), _bf(w_out))

    yp, k_p, v_p, ki_p, conv_p = _layer(
        x_prompt, shift[:bp], scale[:bp], gate[:bp], jnp.arange(seq, dtype=jnp.int32),
        jnp.zeros((bp, CONV_W - 1, d), x_prompt.dtype), None, weights,
        tm=256, t_real=seq, kt_size=512, qb=QB_PROMPT)

    x_pad = jnp.pad(x_sample, ((0, 0), (0, QB_SAMPLE - dec_seq), (0, 0)))
    past = (cache_k.reshape(bs, past_len, KV_W), cache_v.reshape(bs, past_len, KV_W), cache_idx_k)
    ys, k_s, v_s, ki_s, conv_s = _layer(
        x_pad, shift[bp:], scale[bp:], gate[bp:],
        past_len + jnp.arange(QB_SAMPLE, dtype=jnp.int32), state_conv, past, weights,
        tm=QB_SAMPLE, t_real=dec_seq, kt_size=384, qb=QB_SAMPLE)

    return (yp, ys[:, :dec_seq], k_p, v_p, ki_p, conv_p,
            k_s[:, :dec_seq], v_s[:, :dec_seq], ki_s[:, :dec_seq], conv_s)
```

```python
import functools
import math

import jax
import jax.numpy as jnp
from jax import lax
from jax.experimental import pallas as pl
from jax.experimental.pallas import tpu as pltpu

D_MODEL = 1024
CHUNK = 64
CHUNK_SHIFT = 6
N_HEADS = 8
N_KV = 2
HEAD_DIM = 128
ATT_W = N_HEADS * HEAD_DIM
KV_W = N_KV * HEAD_DIM
ROT_FRAC = 4
ROPE_THETA = 500000.0
IDX_HEADS = 8
IDX_DIM = 64
TOPK_MAX = 256
CONV_W = 3
EPS = 1e-6

LANES = 128
SUBLANES = 8
QB_PROMPT = 256
QB_SAMPLE = 128
Q_SCALE = HEAD_DIM ** -0.5 * math.log2(math.e)
VMEM_LIMIT = 56 * 1024 * 1024

C_Q, C_K, C_V, C_ZA, C_QI = 0, 1024, 1280, 1536, 2560
C_U, C_BG, C_CG, C_ZB, C_GA, C_GB = 3072, 4096, 5120, 6144, 7168, 8192
C_KI, C_WI, C_END = 9216, 9344, 9472

NEG_BIG = -1e30
TINY = 1.1754943508222875e-38
STAT_STRIDE = 4
STAT_MARGIN = 0.25
SEARCH_VALUE_IT = 18
SEARCH_FIRST_IT = 18
SEARCH_MAX_IT = SEARCH_VALUE_IT + 36

_NT = (((1,), (1,)), ((), ()))


def _bf(x):
    return x.astype(jnp.bfloat16)


def _dot(a, b):
    return jnp.dot(a, b, preferred_element_type=jnp.float32)


def _dot_nt(a, b):
    return lax.dot_general(a, b, _NT, preferred_element_type=jnp.float32)


def _silu(x):
    return x * jax.nn.sigmoid(x)


def _mod_kernel(c_ref, w_ref, b_ref, o_ref):
    c = c_ref[...]
    s = _silu(c)
    w = w_ref[...]
    s_hi = _bf(s)
    s_lo = _bf(s - s_hi.astype(jnp.float32))
    w_hi = _bf(w)
    w_lo = _bf(w - w_hi.astype(jnp.float32))
    acc = _dot(s_hi, w_hi) + (_dot(s_lo, w_hi) + _dot(s_hi, w_lo))
    o_ref[...] = acc + b_ref[...]


def _mod_call(c_all, w_ada, b_ada):
    nb, d = c_all.shape
    n = w_ada.shape[1]
    bn = 1024
    return pl.pallas_call(
        _mod_kernel,
        grid=(n // bn,),
        in_specs=[
            pl.BlockSpec((nb, d), lambda i: (0, 0)),
            pl.BlockSpec((d, bn), lambda i: (0, i)),
            pl.BlockSpec((1, bn), lambda i: (0, i)),
        ],
        out_specs=pl.BlockSpec((nb, bn), lambda i: (0, i)),
        out_shape=jax.ShapeDtypeStruct((nb, n), jnp.float32),
        compiler_params=pltpu.CompilerParams(dimension_semantics=("arbitrary",)),
        name="adaln_mod",
    )(c_all, w_ada, b_ada.reshape(1, n))


def _rope(x, cos, sin_up, sin_dn, half):
    up = pltpu.roll(x, LANES - half, 1)
    dn = pltpu.roll(x, half, 1)
    return x * cos + up * sin_up + dn * sin_dn


def _proj_kernel(x_ref, shift_ref, scale_ref, ng_ref, qg_ref, kg_ref,
                 rc_ref, rs1_ref, rs2_ref, ic_ref, is1_ref, is2_ref,
                 wa_ref, wb_ref, wc_ref, wpb_ref, cw_ref, cs_ref,
                 q_ref, k_ref, v_ref, kb_ref, vt_ref, sza_ref, qi_ref, ki_ref, ki2_ref, wi_ref,
                 sga_ref, gbb_ref, conv_ref, carry_ref, *, tm, conv_row):
    t = pl.program_id(1)
    x = x_ref[0]
    ms = jnp.mean(x * x, axis=-1, keepdims=True)
    xn = x * lax.rsqrt(ms + EPS) * ng_ref[...]
    h = xn * (1.0 + scale_ref[0]) + shift_ref[0]
    hb = _bf(h)

    def proj(c0, c1):
        for w_ref, base in ((wc_ref, C_KI), (wb_ref, C_U), (wa_ref, C_Q)):
            if c0 >= base:
                return _dot(hb, w_ref[:, c0 - base:c1 - base])

    rc, rs1, rs2 = rc_ref[...], rs1_ref[...], rs2_ref[...]
    ic, is1, is2 = ic_ref[...], is1_ref[...], is2_ref[...]
    rot_half = HEAD_DIM // ROT_FRAC // 2
    idx_half = IDX_DIM // ROT_FRAC // 2

    def head_norm_rope(xh, g):
        r = lax.rsqrt(jnp.mean(xh * xh, axis=-1, keepdims=True) + EPS)
        return _rope(xh * r * g, rc, rs1, rs2, rot_half)

    q = proj(C_Q, C_K)
    qg = qg_ref[...]
    for hh in range(N_HEADS):
        sl = slice(hh * HEAD_DIM, (hh + 1) * HEAD_DIM)
        q_ref[0, :, sl] = _bf(head_norm_rope(q[:, sl], qg) * Q_SCALE)

    k = proj(C_K, C_V)
    kg = kg_ref[...]
    v = proj(C_V, C_ZA)
    for hh in range(N_KV):
        sl = slice(hh * HEAD_DIM, (hh + 1) * HEAD_DIM)
        kh = head_norm_rope(k[:, sl], kg)
        k_ref[0, :, hh, :] = kh
        kb_ref[0, :, sl] = _bf(kh)
        v_ref[0, :, hh, :] = v[:, sl]
    vt_ref[0, 0] = _bf(v.T)

    sza_ref[0] = _silu(proj(C_ZA, C_QI))

    qi = proj(C_QI, C_U)
    for p in range(IDX_HEADS * IDX_DIM // LANES):
        sl = slice(p * LANES, (p + 1) * LANES)
        qi_ref[0, :, sl] = _bf(_rope(qi[:, sl], ic, is1, is2, idx_half) * (IDX_DIM ** -0.5))

    ki2 = _rope(proj(C_KI, C_WI), ic, is1, is2, idx_half)
    ki2_ref[0] = ki2
    ki_ref[0] = ki2[:, :IDX_DIM]
    wi_ref[0] = proj(C_WI, C_END) * (IDX_HEADS ** -0.5)

    cv = proj(C_CG, C_ZB) * proj(C_U, C_BG)

    @pl.when(t == 0)
    def _():
        carry_ref[0:2, :] = cs_ref[0]

    c0 = carry_ref[0:1, :]
    c1 = carry_ref[1:2, :]
    row = lax.broadcasted_iota(jnp.int32, (tm, 1), 0)
    r1 = jnp.where(row == 0, c1, pltpu.roll(cv, 1, 0))
    r2 = jnp.where(row == 0, c0, jnp.where(row == 1, c1, pltpu.roll(cv, 2, 0)))
    cw = cw_ref[...]
    y_conv = cw[0:1, :] * r2 + cw[1:2, :] * r1 + cw[2:3, :] * cv
    carry_ref[0:2, :] = cv[tm - 2:tm, :]
    conv_ref[0] = cv[conv_row - 2:conv_row, :]

    tb = proj(C_BG, C_CG) * y_conv * _silu(proj(C_ZB, C_GA))
    b_out = _dot(_bf(tb), wpb_ref[...])
    gbb_ref[0] = jax.nn.sigmoid(proj(C_GB, C_KI)) * b_out
    sga_ref[0] = jax.nn.sigmoid(proj(C_GA, C_GB))


def _proj_call(x, shift, scale, norm_g, qg, kg, rope_tabs, w_r, w_pb, conv_w, conv_state,
               *, tm, conv_row):
    b, t, d = x.shape
    nt = t // tm
    tok = lambda w: pl.BlockSpec((1, tm, w), lambda i, j: (i, j, 0))
    per_b = lambda r, w: pl.BlockSpec((1, r, w), lambda i, j: (i, 0, 0))
    const = lambda r, w: pl.BlockSpec((r, w), lambda i, j: (0, 0))
    tab = pl.BlockSpec((tm, LANES), lambda i, j: (j, 0))
    kv4 = pl.BlockSpec((1, tm, N_KV, HEAD_DIM), lambda i, j: (i, j, 0, 0))
    whole = pl.BlockSpec(memory_space=pltpu.VMEM)
    f32, bf16 = jnp.float32, jnp.bfloat16
    out_shape = (
        jax.ShapeDtypeStruct((b, t, ATT_W), bf16),
        jax.ShapeDtypeStruct((b, t, N_KV, HEAD_DIM), f32),
        jax.ShapeDtypeStruct((b, t, N_KV, HEAD_DIM), f32),
        jax.ShapeDtypeStruct((b, t, KV_W), bf16),
        jax.ShapeDtypeStruct((b, nt, KV_W, tm), bf16),
        jax.ShapeDtypeStruct((b, t, ATT_W), f32),
        jax.ShapeDtypeStruct((b, t, IDX_HEADS * IDX_DIM), bf16),
        jax.ShapeDtypeStruct((b, t, IDX_DIM), f32),
        jax.ShapeDtypeStruct((b, t, LANES), f32),
        jax.ShapeDtypeStruct((b, t, LANES), f32),
        jax.ShapeDtypeStruct((b, t, D_MODEL), f32),
        jax.ShapeDtypeStruct((b, t, D_MODEL), f32),
        jax.ShapeDtypeStruct((b, CONV_W - 1, D_MODEL), f32),
    )
    out_specs = (
        tok(ATT_W), kv4, kv4, tok(KV_W),
        pl.BlockSpec((1, 1, KV_W, tm), lambda i, j: (i, j, 0, 0)),
        tok(ATT_W), tok(IDX_HEADS * IDX_DIM), tok(IDX_DIM), tok(LANES), tok(LANES),
        tok(D_MODEL), tok(D_MODEL), per_b(CONV_W - 1, D_MODEL),
    )
    in_specs = [
        tok(d), per_b(1, d), per_b(1, d), const(1, d), const(1, LANES), const(1, LANES),
        tab, tab, tab, tab, tab, tab,
        whole, whole, whole, whole, const(CONV_W, d), per_b(CONV_W - 1, d),
    ]
    return pl.pallas_call(
        functools.partial(_proj_kernel, tm=tm, conv_row=conv_row),
        grid=(b, nt),
        in_specs=in_specs,
        out_specs=out_specs,
        out_shape=out_shape,
        scratch_shapes=[pltpu.VMEM((SUBLANES, d), jnp.float32)],
        compiler_params=pltpu.CompilerParams(
            dimension_semantics=("arbitrary", "arbitrary"), vmem_limit_bytes=VMEM_LIMIT),
        name="in_proj",
    )(x, shift, scale, norm_g, qg, kg, *rope_tabs, *w_r, w_pb, conv_w, conv_state)


def _key_tiles(pos0, j, *, l_real, kt_size, qb, minimum=jnp.minimum):
    last_q = pos0 + j * qb + qb - 1
    kmax = minimum(l_real, ((last_q >> CHUNK_SHIFT) + 1) * CHUNK)
    return (kmax + kt_size - 1) // kt_size


def _unrolled(n, body, init):
    carry = init
    for i in range(n):
        carry = body(i, carry)
    return carry


def _attn_kernel(*refs, pos0, l_real, kt_size, topk, qb, nkt_values):
    nkt = _key_tiles(pos0, pl.program_id(1), l_real=l_real, kt_size=kt_size, qb=qb)
    for c in nkt_values:
        run = functools.partial(_attn_block, *refs, pos0=pos0, l_real=l_real, kt_size=kt_size,
                                topk=topk, qb=qb, nkt=c)
        if len(nkt_values) == 1:
            run()
        else:
            pl.when(nkt == c)(run)


def _attn_block(q_ref, qi_ref, wi_ref, k_ref, vt_ref, ki2_ref,
                sza_ref, sga_ref, gbb_ref, x_ref, gate_ref, wpa_ref, wout_ref,
                y_ref, sc_ref, bias_ref, s_ref, acc_ref, o_ref,
                *, pos0, l_real, kt_size, topk, qb, nkt):
    KT = kt_size
    QB = qb
    grp = N_HEADS // N_KV
    vt_w = vt_ref.shape[3]
    vt_per_kt = KT // vt_w
    q0 = pos0 + pl.program_id(1) * QB

    lane = lax.broadcasted_iota(jnp.int32, (1, QB), 1)
    q_chunk = (q0 + lane) >> CHUNK_SHIFT
    w_t = wi_ref[0].T
    qi = qi_ref[0]
    lane_kt = lax.broadcasted_iota(jnp.int32, (KT, LANES), 1)
    sub_kt = lax.broadcasted_iota(jnp.int32, (KT, 1), 0)

    def score_body(kt, carry):
        smax8, smin8, sum8, sq8 = carry
        k0 = kt * KT
        ki2 = ki2_ref[0, pl.ds(k0, KT), :]
        ki_lo = _bf(jnp.where(lane_kt < IDX_DIM, ki2, 0.0))
        ki_hi = _bf(jnp.where(lane_kt >= IDX_DIM, ki2, 0.0))
        acc = jnp.zeros((KT, QB), jnp.float32)
        for p in range(IDX_HEADS // 2):
            slab = qi[:, p * LANES:(p + 1) * LANES]
            acc = acc + w_t[2 * p:2 * p + 1, :] * jnp.maximum(_dot_nt(ki_lo, slab), 0.0)
            acc = acc + w_t[2 * p + 1:2 * p + 2, :] * jnp.maximum(_dot_nt(ki_hi, slab), 0.0)
        kpos = k0 + sub_kt
        adm = ((kpos >> CHUNK_SHIFT) <= q_chunk) & (kpos < l_real)
        sc_ref[pl.ds(k0, KT), :] = jnp.where(adm, acc, -jnp.inf)
        acc3 = acc.reshape(KT // SUBLANES, SUBLANES, QB)
        sub = acc.reshape(KT // (STAT_STRIDE * SUBLANES), STAT_STRIDE, SUBLANES, QB)[:, 0]
        return (jnp.maximum(smax8, jnp.max(acc3, axis=0)),
                jnp.minimum(smin8, jnp.min(acc3, axis=0)),
                sum8 + jnp.sum(sub, axis=0), sq8 + jnp.sum(sub * sub, axis=0))

    smax8, smin8, sum8, sq8 = _unrolled(
        nkt, score_body,
        (jnp.full((SUBLANES, QB), -jnp.inf, jnp.float32),
         jnp.full((SUBLANES, QB), jnp.inf, jnp.float32),
         jnp.zeros((SUBLANES, QB), jnp.float32), jnp.zeros((SUBLANES, QB), jnp.float32)))

    def count_ge(cand):
        def body(kt, acc):
            k0 = kt * KT
            m = (sc_ref[pl.ds(k0, KT), :] >= cand).astype(jnp.int32)
            return acc + jnp.sum(m.reshape(KT // SUBLANES, SUBLANES, QB), axis=0)
        acc = _unrolled(nkt, body, jnp.zeros((SUBLANES, QB), jnp.int32))
        return jnp.sum(acc, axis=0, keepdims=True)

    def inside(x, lo, hi):
        return (x > lo) & (x < hi)

    def active_rows(lo, hi, clo):
        return (clo > topk) & inside(lo * 0.5 + hi * 0.5, lo, hi)

    def any_active(st):
        lo, hi, clo, _ = st
        return jnp.max(active_rows(lo, hi, clo).astype(jnp.float32)) > 0.0

    def search_step(it, st):
        lo, hi, clo, chi = st
        act = active_rows(lo, hi, clo)
        mid = lo * 0.5 + hi * 0.5
        near_zero = jnp.where(lo >= 0.0, TINY, -TINY)
        geo = jnp.sqrt(jnp.abs(lo)) * jnp.sqrt(jnp.abs(hi)) * jnp.where(lo >= 0.0, 1.0, -1.0)
        guess = jnp.where(it == 0, q_below, jnp.where(it == 1, q_above,
                          jnp.where(it == 2, 0.0, jnp.where(it == 3, near_zero,
                          jnp.where(it < SEARCH_VALUE_IT, mid, geo)))))
        cand = jnp.where(inside(guess, lo, hi), guess, mid)
        cnt = count_ge(cand)
        ge = cnt >= topk
        up_lo = act & ge
        up_hi = act & jnp.logical_not(ge)
        return (jnp.where(up_lo, cand, lo), jnp.where(up_hi, cand, hi),
                jnp.where(up_lo, cnt, clo), jnp.where(up_hi, cnt, chi))

    n_adm = jnp.minimum(l_real, (q_chunk + 1) * CHUNK)
    lo0 = jnp.min(smin8, axis=0, keepdims=True)
    smax = jnp.max(smax8, axis=0, keepdims=True)
    hi0 = smax + jnp.abs(smax) * 2.0 ** -10 + 1e-30
    n_stat = nkt * KT // STAT_STRIDE
    mean = jnp.sum(sum8, axis=0, keepdims=True) * (1.0 / n_stat)
    std = jnp.sqrt(jnp.maximum(jnp.sum(sq8, axis=0, keepdims=True) * (1.0 / n_stat) - mean * mean, 0.0))
    z_q = jnp.log(jnp.maximum(n_adm - topk, 1).astype(jnp.float32) * (1.0 / topk)) * (1.0 / 1.7)
    q_below = mean + (z_q - STAT_MARGIN) * std
    q_above = mean + (z_q + STAT_MARGIN) * std
    st = (lo0, hi0, n_adm, jnp.zeros((1, QB), jnp.int32))
    n_first = jnp.where(any_active(st), SEARCH_FIRST_IT, 0)
    st = lax.fori_loop(0, n_first, search_step, st)

    def search_cond(c):
        return (c[0] < SEARCH_MAX_IT) & any_active(c[1])

    def search_body(c):
        it, st = c
        return it + 2, search_step(it + 1, search_step(it, st))

    _, (thr, _, n_ge, n_gt) = lax.while_loop(search_cond, search_body, (n_first, st))
    need = topk - n_gt
    any_tie = jnp.max((n_ge > topk).astype(jnp.float32)) > 0.0

    def bias_body(kt, carry):
        k0 = kt * KT
        bias_ref[pl.ds(k0, KT), :] = jnp.where(sc_ref[pl.ds(k0, KT), :] >= thr, 0.0, NEG_BIG)
        return carry

    _unrolled(nkt, bias_body, 0)

    @pl.when(any_tie)
    def _():
        tri = _bf((lax.broadcasted_iota(jnp.int32, (LANES, LANES), 1)
                   <= lax.broadcasted_iota(jnp.int32, (LANES, LANES), 0)).astype(jnp.float32))
        need_f = need.astype(jnp.float32)
        untied = n_ge <= topk

        def tie_body(kt, seen):
            for r in range(KT // LANES):
                k0 = kt * KT + r * LANES
                sc = sc_ref[pl.ds(k0, LANES), :]
                eq = sc == thr
                eq_f = eq.astype(jnp.float32)
                rank = _dot(tri, _bf(eq_f)) + seen
                sel = (sc > thr) | (eq & ((rank <= need_f) | untied))
                bias_ref[pl.ds(k0, LANES), :] = jnp.where(sel, 0.0, NEG_BIG)
                seen = seen + jnp.sum(eq_f, axis=0, keepdims=True)
            return seen

        _unrolled(nkt, tie_body, jnp.zeros((1, QB), jnp.float32))

    gw = grp * QB
    m8_init = tuple(jnp.full((SUBLANES, QB), NEG_BIG, jnp.float32) for _ in range(grp))
    l8_init = jnp.zeros((SUBLANES, gw), jnp.float32)
    q_groups = [jnp.concatenate(
        [q_ref[0, :, (g * grp + hh) * HEAD_DIM:(g * grp + hh + 1) * HEAD_DIM]
         for hh in range(grp)], axis=0) for g in range(N_KV)]

    def qk_tile(g, kt, m8):
        k0 = kt * KT
        kb = k_ref[0, pl.ds(k0, KT), g * HEAD_DIM:(g + 1) * HEAD_DIM]
        bias = bias_ref[pl.ds(k0, KT), :]
        s = _dot_nt(kb, q_groups[g])
        out = []
        for hh in range(grp):
            sh = s[:, hh * QB:(hh + 1) * QB] + bias
            s_ref[pl.ds(k0, KT), g * gw + hh * QB:g * gw + (hh + 1) * QB] = sh
            out.append(jnp.maximum(
                m8[hh], jnp.max(sh.reshape(KT // SUBLANES, SUBLANES, QB), axis=0)))
        return tuple(out)

    def pv_tile(g, kt, l8, m_all):
        k0 = kt * KT
        p = jnp.exp2(s_ref[pl.ds(k0, KT), g * gw:(g + 1) * gw] - m_all)
        pb = _bf(p)
        acc_ref[...] += sum(
            _dot(vt_ref[0, kt * vt_per_kt + r, g * HEAD_DIM:(g + 1) * HEAD_DIM, :],
                 pb[r * vt_w:(r + 1) * vt_w, :])
            for r in range(vt_per_kt))
        return l8 + jnp.sum(p.reshape(KT // SUBLANES, SUBLANES, gw), axis=0)

    def col_max(m8):
        return jnp.concatenate([jnp.max(m, axis=0, keepdims=True) for m in m8], axis=1)

    def finish_group(g, l8):
        o_t = acc_ref[...] * (1.0 / jnp.sum(l8, axis=0, keepdims=True))
        for hh in range(grp):
            h_abs = g * grp + hh
            o_ref[:, h_abs * HEAD_DIM:(h_abs + 1) * HEAD_DIM] = o_t[:, hh * QB:(hh + 1) * QB].T

    m8 = _unrolled(nkt, functools.partial(qk_tile, 0), m8_init)
    for g in range(N_KV):
        m_all = col_max(m8)
        acc_ref[...] = jnp.zeros((HEAD_DIM, gw), jnp.float32)
        if g + 1 < N_KV:
            def both(kt, carry, g=g, m_all=m_all):
                return pv_tile(g, kt, carry[0], m_all), qk_tile(g + 1, kt, carry[1])
            l8, m8 = _unrolled(nkt, both, (l8_init, m8_init))
        else:
            l8 = _unrolled(nkt, lambda kt, l8, g=g, m_all=m_all: pv_tile(g, kt, l8, m_all),
                           l8_init)
        finish_group(g, l8)

    a_out = _dot(_bf(o_ref[...] * sza_ref[0]), wpa_ref[...])
    merged = sga_ref[0] * a_out + gbb_ref[0]
    y_ref[0] = x_ref[0] + gate_ref[0] * _dot(_bf(merged), wout_ref[...])


def _attn_call(q, qi, wi, k_all, vt_all, ki2_all, sza, sga, gbb, x, gate, w_pa, w_out,
               *, pos0, l_real, kt_size, topk, qb):
    b, t, d = x.shape
    lp = k_all.shape[1]
    nkt_values = sorted({_key_tiles(pos0, jj, l_real=l_real, kt_size=kt_size, qb=qb, minimum=min)
                         for jj in range(t // qb)})
    tok = lambda w: pl.BlockSpec((1, qb, w), lambda i, j: (i, j, 0))
    per_b = lambda r, w: pl.BlockSpec((1, r, w), lambda i, j: (i, 0, 0))
    whole = pl.BlockSpec(memory_space=pltpu.VMEM)
    in_specs = [
        tok(ATT_W), tok(IDX_HEADS * IDX_DIM), tok(LANES),
        per_b(lp, KV_W),
        pl.BlockSpec((1,) + vt_all.shape[1:], lambda i, j: (i, 0, 0, 0)),
        per_b(lp, LANES),
        tok(ATT_W), tok(D_MODEL), tok(D_MODEL), tok(d), per_b(1, d),
        whole, whole,
    ]
    return pl.pallas_call(
        functools.partial(_attn_kernel, pos0=pos0, l_real=l_real, kt_size=kt_size, topk=topk,
                          qb=qb, nkt_values=nkt_values),
        grid=(b, t // qb),
        in_specs=in_specs,
        out_specs=tok(d),
        out_shape=jax.ShapeDtypeStruct((b, t, d), jnp.float32),
        scratch_shapes=[
            pltpu.VMEM((lp, qb), jnp.float32),
            pltpu.VMEM((lp, qb), jnp.float32),
            pltpu.VMEM((lp, N_HEADS * qb), jnp.float32),
            pltpu.VMEM((HEAD_DIM, N_HEADS // N_KV * qb), jnp.float32),
            pltpu.VMEM((qb, ATT_W), jnp.float32),
        ],
        compiler_params=pltpu.CompilerParams(
            dimension_semantics=("arbitrary", "arbitrary"), vmem_limit_bytes=VMEM_LIMIT),
        name="dsa_attn",
    )(q, qi, wi, k_all, vt_all, ki2_all, sza, sga, gbb, x, gate, w_pa, w_out)


def _rope_tables(pos, width, period):
    rot = period // ROT_FRAC
    half = rot // 2
    inv = ROPE_THETA ** (-2.0 * jnp.arange(half, dtype=jnp.float32) / rot)
    ang = pos.astype(jnp.float32)[:, None] * inv[None, :]
    cos, sin = jnp.cos(ang), jnp.sin(ang)
    n = pos.shape[0]
    rest = period - rot
    one = jnp.ones((n, rest), jnp.float32)
    zero_h = jnp.zeros((n, half), jnp.float32)
    zero_r = jnp.zeros((n, rest), jnp.float32)
    c = jnp.concatenate([cos, cos, one], axis=1)
    s_up = jnp.concatenate([-sin, zero_h, zero_r], axis=1)
    s_dn = jnp.concatenate([zero_h, sin, zero_r], axis=1)
    rep = width // period
    return tuple(jnp.tile(a, (1, rep)) for a in (c, s_up, s_dn))


def _prep_w_in(w_in):
    o_ki = ATT_W + 2 * KV_W + ATT_W + IDX_HEADS * IDX_DIM
    o_wi = o_ki + IDX_DIM
    o_u = o_wi + IDX_HEADS
    ki = w_in[:, o_ki:o_wi]
    pad = jnp.zeros((w_in.shape[0], C_END - C_WI - IDX_HEADS), w_in.dtype)
    w_c = jnp.concatenate([ki, ki, w_in[:, o_wi:o_u], pad], axis=1)
    return _bf(w_in[:, :o_ki]), _bf(w_in[:, o_u:]), _bf(w_c)


def _layer(x, shift, scale, gate, pos, conv_state, past, weights, *, tm, t_real, kt_size, qb):
    norm_g, qg, kg, w_r, w_pb, conv_w, w_pa, w_out = weights
    b, t, _ = x.shape
    tabs = _rope_tables(pos, LANES, HEAD_DIM) + _rope_tables(pos, LANES, IDX_DIM)
    conv_row = t_real - (t // tm - 1) * tm
    (q, k, v, kb, vt, sza, qi, ki, ki2, wi, sga, gbb, conv_new) = _proj_call(
        x, shift, scale, norm_g, qg, kg, tabs, w_r, w_pb, conv_w, conv_state,
        tm=tm, conv_row=conv_row)
    if past is None:
        p_len = 0
        k_all, vt_all, ki2_all = kb, vt, ki2
    else:
        past_k, past_v, past_ki = past
        p_len = past_k.shape[1]
        lp = -(-(p_len + t) // kt_size) * kt_size
        extra = lp - p_len - t
        k_all = jnp.concatenate(
            [_bf(past_k), kb, jnp.zeros((b, extra, KV_W), jnp.bfloat16)], axis=1)
        vt_all = jnp.concatenate(
            [_bf(jnp.swapaxes(past_v, 1, 2)), vt[:, 0], jnp.zeros((b, KV_W, extra), jnp.bfloat16)],
            axis=2)
        vt_all = jnp.swapaxes(vt_all.reshape(b, KV_W, lp // kt_size, kt_size), 1, 2)
        ki2_all = jnp.concatenate(
            [jnp.concatenate([past_ki, past_ki], axis=-1), ki2,
             jnp.zeros((b, extra, LANES), jnp.float32)], axis=1)
    l_real = p_len + t_real
    topk = min(TOPK_MAX, l_real // 4)
    y = _attn_call(q, qi, wi, k_all, vt_all, ki2_all, sza, sga, gbb, x, gate, w_pa, w_out,
                   pos0=p_len, l_real=l_real, kt_size=kt_size, topk=topk, qb=qb)
    return y, k, v, ki, conv_new


def kernel(x_prompt, x_sample, cache_k, cache_v, cache_idx_k, state_conv, c_prompt, c_sample,
           w_ada, b_ada, norm_g, w_in, q_norm_g, k_norm_g, conv_w, w_pa, w_pb, w_out):
    bp, seq, d = x_prompt.shape
    bs, dec_seq, _ = x_sample.shape
    past_len = cache_k.shape[1]

    mod = _mod_call(jnp.concatenate([c_prompt, c_sample], axis=0), w_ada, b_ada)
    shift, scale, gate = (m.reshape(bp + bs, 1, d) for m in jnp.split(mod, 3, axis=-1))

    weights = (norm_g.reshape(1, d), q_norm_g.reshape(1, HEAD_DIM), k_norm_g.reshape(1, HEAD_DIM),
               _prep_w_in(w_in), _bf(w_pb), conv_w, _bf(w_pa), _bf(w_out))

    yp, k_p, v_p, ki_p, conv_p = _layer(
        x_prompt, shift[:bp], scale[:bp], gate[:bp], jnp.arange(seq, dtype=jnp.int32),
        jnp.zeros((bp, CONV_W - 1, d), x_prompt.dtype), None, weights,
        tm=256, t_real=seq, kt_size=512, qb=QB_PROMPT)

    x_pad = jnp.pad(x_sample, ((0, 0), (0, QB_SAMPLE - dec_seq), (0, 0)))
    past = (cache_k.reshape(bs, past_len, KV_W), cache_v.reshape(bs, past_len, KV_W), cache_idx_k)
    ys, k_s, v_s, ki_s, conv_s = _layer(
        x_pad, shift[bp:], scale[bp:], gate[bp:],
        past_len + jnp.arange(QB_SAMPLE, dtype=jnp.int32), state_conv, past, weights,
        tm=QB_SAMPLE, t_real=dec_seq, kt_size=384, qb=QB_SAMPLE)

    return (yp, ys[:, :dec_seq], k_p, v_p, ki_p, conv_p,
            k_s[:, :dec_seq], v_s[:, :dec_seq], ki_s[:, :dec_seq], conv_s)
```

```python
import functools
import math

import jax
import jax.numpy as jnp
from jax import lax
from jax.experimental import pallas as pl
from jax.experimental.pallas import tpu as pltpu

D_MODEL = 1024
CHUNK = 64
CHUNK_SHIFT = 6
N_HEADS = 8
N_KV = 2
HEAD_DIM = 128
ATT_W = N_HEADS * HEAD_DIM
KV_W = N_KV * HEAD_DIM
ROT_FRAC = 4
ROPE_THETA = 500000.0
IDX_HEADS = 8
IDX_DIM = 64
TOPK_MAX = 256
CONV_W = 3
EPS = 1e-6

LANES = 128
SUBLANES = 8
QB_PROMPT = 256
QB_SAMPLE = 128
Q_SCALE = HEAD_DIM ** -0.5 * math.log2(math.e)
VMEM_LIMIT = 56 * 1024 * 1024

C_Q, C_K, C_V, C_ZA, C_QI = 0, 1024, 1280, 1536, 2560
C_U, C_BG, C_CG, C_ZB, C_GA, C_GB = 3072, 4096, 5120, 6144, 7168, 8192
C_KI, C_WI, C_END = 9216, 9344, 9472

NEG_BIG = -1e30
TINY = 1.1754943508222875e-38
FOLD_CHAINS = 4
STAT_STRIDE = 4
STAT_MARGIN = 0.25
SEARCH_VALUE_IT = 18
SEARCH_FIRST_IT = 18
SEARCH_MAX_IT = SEARCH_VALUE_IT + 36

_NT = (((1,), (1,)), ((), ()))


def _bf(x):
    return x.astype(jnp.bfloat16)


def _dot(a, b):
    return jnp.dot(a, b, preferred_element_type=jnp.float32)


def _dot_nt(a, b):
    return lax.dot_general(a, b, _NT, preferred_element_type=jnp.float32)


def _silu(x):
    return x * jax.nn.sigmoid(x)


def _mod_kernel(c_ref, w_ref, b_ref, o_ref):
    c = c_ref[...]
    s = _silu(c)
    w = w_ref[...]
    s_hi = _bf(s)
    s_lo = _bf(s - s_hi.astype(jnp.float32))
    w_hi = _bf(w)
    w_lo = _bf(w - w_hi.astype(jnp.float32))
    acc = _dot(s_hi, w_hi) + (_dot(s_lo, w_hi) + _dot(s_hi, w_lo))
    o_ref[...] = acc + b_ref[...]


def _mod_call(c_all, w_ada, b_ada):
    nb, d = c_all.shape
    n = w_ada.shape[1]
    bn = 1024
    return pl.pallas_call(
        _mod_kernel,
        grid=(n // bn,),
        in_specs=[
            pl.BlockSpec((nb, d), lambda i: (0, 0)),
            pl.BlockSpec((d, bn), lambda i: (0, i)),
            pl.BlockSpec((1, bn), lambda i: (0, i)),
        ],
        out_specs=pl.BlockSpec((nb, bn), lambda i: (0, i)),
        out_shape=jax.ShapeDtypeStruct((nb, n), jnp.float32),
        compiler_params=pltpu.CompilerParams(dimension_semantics=("arbitrary",)),
        name="adaln_mod",
    )(c_all, w_ada, b_ada.reshape(1, n))


def _rope(x, cos, sin_up, sin_dn, half):
    up = pltpu.roll(x, LANES - half, 1)
    dn = pltpu.roll(x, half, 1)
    return x * cos + up * sin_up + dn * sin_dn


def _proj_kernel(x_ref, shift_ref, scale_ref, ng_ref, qg_ref, kg_ref,
                 rc_ref, rs1_ref, rs2_ref, ic_ref, is1_ref, is2_ref,
                 wa_ref, wb_ref, wc_ref, wpb_ref, cw_ref, cs_ref,
                 q_ref, k_ref, v_ref, kb_ref, vt_ref, sza_ref, qi_ref, ki_ref, ki2_ref, wi_ref,
                 sga_ref, gbb_ref, conv_ref, carry_ref, *, tm, conv_row):
    t = pl.program_id(1)
    x = x_ref[0]
    ms = jnp.mean(x * x, axis=-1, keepdims=True)
    xn = x * lax.rsqrt(ms + EPS) * ng_ref[...]
    h = xn * (1.0 + scale_ref[0]) + shift_ref[0]
    hb = _bf(h)

    def proj(c0, c1):
        for w_ref, base in ((wc_ref, C_KI), (wb_ref, C_U), (wa_ref, C_Q)):
            if c0 >= base:
                return _dot(hb, w_ref[:, c0 - base:c1 - base])

    rc, rs1, rs2 = rc_ref[...], rs1_ref[...], rs2_ref[...]
    ic, is1, is2 = ic_ref[...], is1_ref[...], is2_ref[...]
    rot_half = HEAD_DIM // ROT_FRAC // 2
    idx_half = IDX_DIM // ROT_FRAC // 2

    def head_norm_rope(xh, g):
        r = lax.rsqrt(jnp.mean(xh * xh, axis=-1, keepdims=True) + EPS)
        return _rope(xh * r * g, rc, rs1, rs2, rot_half)

    q = proj(C_Q, C_K)
    qg = qg_ref[...]
    for hh in range(N_HEADS):
        sl = slice(hh * HEAD_DIM, (hh + 1) * HEAD_DIM)
        q_ref[0, :, sl] = _bf(head_norm_rope(q[:, sl], qg) * Q_SCALE)

    k = proj(C_K, C_V)
    kg = kg_ref[...]
    v = proj(C_V, C_ZA)
    for hh in range(N_KV):
        sl = slice(hh * HEAD_DIM, (hh + 1) * HEAD_DIM)
        kh = head_norm_rope(k[:, sl], kg)
        k_ref[0, :, hh, :] = kh
        kb_ref[0, :, sl] = _bf(kh)
        v_ref[0, :, hh, :] = v[:, sl]
    vt_ref[0, 0] = _bf(v.T)

    sza_ref[0] = _silu(proj(C_ZA, C_QI))

    qi = proj(C_QI, C_U)
    for p in range(IDX_HEADS * IDX_DIM // LANES):
        sl = slice(p * LANES, (p + 1) * LANES)
        qi_ref[0, :, sl] = _bf(_rope(qi[:, sl], ic, is1, is2, idx_half) * (IDX_DIM ** -0.5))

    ki2 = _rope(proj(C_KI, C_WI), ic, is1, is2, idx_half)
    ki2_ref[0] = ki2
    ki_ref[0] = ki2[:, :IDX_DIM]
    wi_ref[0] = proj(C_WI, C_END) * (IDX_HEADS ** -0.5)

    cv = proj(C_CG, C_ZB) * proj(C_U, C_BG)

    @pl.when(t == 0)
    def _():
        carry_ref[0:2, :] = cs_ref[0]

    c0 = carry_ref[0:1, :]
    c1 = carry_ref[1:2, :]
    row = lax.broadcasted_iota(jnp.int32, (tm, 1), 0)
    r1 = jnp.where(row == 0, c1, pltpu.roll(cv, 1, 0))
    r2 = jnp.where(row == 0, c0, jnp.where(row == 1, c1, pltpu.roll(cv, 2, 0)))
    cw = cw_ref[...]
    y_conv = cw[0:1, :] * r2 + cw[1:2, :] * r1 + cw[2:3, :] * cv
    carry_ref[0:2, :] = cv[tm - 2:tm, :]
    conv_ref[0] = cv[conv_row - 2:conv_row, :]

    tb = proj(C_BG, C_CG) * y_conv * _silu(proj(C_ZB, C_GA))
    b_out = _dot(_bf(tb), wpb_ref[...])
    gbb_ref[0] = jax.nn.sigmoid(proj(C_GB, C_KI)) * b_out
    sga_ref[0] = jax.nn.sigmoid(proj(C_GA, C_GB))


def _proj_call(x, shift, scale, norm_g, qg, kg, rope_tabs, w_r, w_pb, conv_w, conv_state,
               *, tm, conv_row):
    b, t, d = x.shape
    nt = t // tm
    tok = lambda w: pl.BlockSpec((1, tm, w), lambda i, j: (i, j, 0))
    per_b = lambda r, w: pl.BlockSpec((1, r, w), lambda i, j: (i, 0, 0))
    const = lambda r, w: pl.BlockSpec((r, w), lambda i, j: (0, 0))
    tab = pl.BlockSpec((tm, LANES), lambda i, j: (j, 0))
    kv4 = pl.BlockSpec((1, tm, N_KV, HEAD_DIM), lambda i, j: (i, j, 0, 0))
    whole = pl.BlockSpec(memory_space=pltpu.VMEM)
    f32, bf16 = jnp.float32, jnp.bfloat16
    out_shape = (
        jax.ShapeDtypeStruct((b, t, ATT_W), bf16),
        jax.ShapeDtypeStruct((b, t, N_KV, HEAD_DIM), f32),
        jax.ShapeDtypeStruct((b, t, N_KV, HEAD_DIM), f32),
        jax.ShapeDtypeStruct((b, t, KV_W), bf16),
        jax.ShapeDtypeStruct((b, nt, KV_W, tm), bf16),
        jax.ShapeDtypeStruct((b, t, ATT_W), f32),
        jax.ShapeDtypeStruct((b, t, IDX_HEADS * IDX_DIM), bf16),
        jax.ShapeDtypeStruct((b, t, IDX_DIM), f32),
        jax.ShapeDtypeStruct((b, t, LANES), f32),
        jax.ShapeDtypeStruct((b, t, LANES), f32),
        jax.ShapeDtypeStruct((b, t, D_MODEL), f32),
        jax.ShapeDtypeStruct((b, t, D_MODEL), f32),
        jax.ShapeDtypeStruct((b, CONV_W - 1, D_MODEL), f32),
    )
    out_specs = (
        tok(ATT_W), kv4, kv4, tok(KV_W),
        pl.BlockSpec((1, 1, KV_W, tm), lambda i, j: (i, j, 0, 0)),
        tok(ATT_W), tok(IDX_HEADS * IDX_DIM), tok(IDX_DIM), tok(LANES), tok(LANES),
        tok(D_MODEL), tok(D_MODEL), per_b(CONV_W - 1, D_MODEL),
    )
    in_specs = [
        tok(d), per_b(1, d), per_b(1, d), const(1, d), const(1, LANES), const(1, LANES),
        tab, tab, tab, tab, tab, tab,
        whole, whole, whole, whole, const(CONV_W, d), per_b(CONV_W - 1, d),
    ]
    return pl.pallas_call(
        functools.partial(_proj_kernel, tm=tm, conv_row=conv_row),
        grid=(b, nt),
        in_specs=in_specs,
        out_specs=out_specs,
        out_shape=out_shape,
        scratch_shapes=[pltpu.VMEM((SUBLANES, d), jnp.float32)],
        compiler_params=pltpu.CompilerParams(
            dimension_semantics=("arbitrary", "arbitrary"), vmem_limit_bytes=VMEM_LIMIT),
        name="in_proj",
    )(x, shift, scale, norm_g, qg, kg, *rope_tabs, *w_r, w_pb, conv_w, conv_state)


def _key_tiles(pos0, j, *, l_real, kt_size, qb, minimum=jnp.minimum):
    last_q = pos0 + j * qb + qb - 1
    kmax = minimum(l_real, ((last_q >> CHUNK_SHIFT) + 1) * CHUNK)
    return (kmax + kt_size - 1) // kt_size


def _fold_rows(x, op):
    rows, w = x.shape
    part = op(x.reshape(FOLD_CHAINS, rows // (FOLD_CHAINS * SUBLANES), SUBLANES, w), axis=1)
    return op(part, axis=0)


def _unrolled(n, body, init):
    carry = init
    for i in range(n):
        carry = body(i, carry)
    return carry


def _attn_kernel(*refs, pos0, l_real, kt_size, topk, qb, nkt_values):
    nkt = _key_tiles(pos0, pl.program_id(1), l_real=l_real, kt_size=kt_size, qb=qb)
    for c in nkt_values:
        run = functools.partial(_attn_block, *refs, pos0=pos0, l_real=l_real, kt_size=kt_size,
                                topk=topk, qb=qb, nkt=c)
        if len(nkt_values) == 1:
            run()
        else:
            pl.when(nkt == c)(run)


def _attn_block(q_ref, qi_ref, wi_ref, k_ref, vt_ref, ki2_ref,
                sza_ref, sga_ref, gbb_ref, x_ref, gate_ref, wpa_ref, wout_ref,
                y_ref, sc_ref, bias_ref, s_ref, acc_ref, o_ref,
                *, pos0, l_real, kt_size, topk, qb, nkt):
    KT = kt_size
    QB = qb
    grp = N_HEADS // N_KV
    vt_w = vt_ref.shape[3]
    vt_per_kt = KT // vt_w
    q0 = pos0 + pl.program_id(1) * QB

    lane = lax.broadcasted_iota(jnp.int32, (1, QB), 1)
    q_chunk = (q0 + lane) >> CHUNK_SHIFT
    w_t = wi_ref[0].T
    qi = qi_ref[0]
    lane_kt = lax.broadcasted_iota(jnp.int32, (KT, LANES), 1)
    sub_kt = lax.broadcasted_iota(jnp.int32, (KT, 1), 0)

    def score_body(kt, carry):
        smax8, smin8, sum8, sq8 = carry
        k0 = kt * KT
        ki2 = ki2_ref[0, pl.ds(k0, KT), :]
        ki_lo = _bf(jnp.where(lane_kt < IDX_DIM, ki2, 0.0))
        ki_hi = _bf(jnp.where(lane_kt >= IDX_DIM, ki2, 0.0))
        acc = jnp.zeros((KT, QB), jnp.float32)
        for p in range(IDX_HEADS // 2):
            slab = qi[:, p * LANES:(p + 1) * LANES]
            acc = acc + w_t[2 * p:2 * p + 1, :] * jnp.maximum(_dot_nt(ki_lo, slab), 0.0)
            acc = acc + w_t[2 * p + 1:2 * p + 2, :] * jnp.maximum(_dot_nt(ki_hi, slab), 0.0)
        kpos = k0 + sub_kt
        adm = ((kpos >> CHUNK_SHIFT) <= q_chunk) & (kpos < l_real)
        sc_ref[pl.ds(k0, KT), :] = jnp.where(adm, acc, -jnp.inf)
        sub = acc.reshape(KT // (STAT_STRIDE * SUBLANES), STAT_STRIDE, SUBLANES, QB)[:, 0]
        return (jnp.maximum(smax8, _fold_rows(acc, jnp.max)),
                jnp.minimum(smin8, _fold_rows(acc, jnp.min)),
                sum8 + jnp.sum(sub, axis=0), sq8 + jnp.sum(sub * sub, axis=0))

    smax8, smin8, sum8, sq8 = _unrolled(
        nkt, score_body,
        (jnp.full((SUBLANES, QB), -jnp.inf, jnp.float32),
         jnp.full((SUBLANES, QB), jnp.inf, jnp.float32),
         jnp.zeros((SUBLANES, QB), jnp.float32), jnp.zeros((SUBLANES, QB), jnp.float32)))

    def count_ge(cand):
        def body(kt, acc):
            k0 = kt * KT
            m = (sc_ref[pl.ds(k0, KT), :] >= cand).astype(jnp.int32)
            return acc + _fold_rows(m, jnp.sum)
        acc = _unrolled(nkt, body, jnp.zeros((SUBLANES, QB), jnp.int32))
        return jnp.sum(acc, axis=0, keepdims=True)

    def inside(x, lo, hi):
        return (x > lo) & (x < hi)

    def active_rows(lo, hi, clo):
        return (clo > topk) & inside(lo * 0.5 + hi * 0.5, lo, hi)

    def any_active(st):
        lo, hi, clo, _ = st
        return jnp.max(active_rows(lo, hi, clo).astype(jnp.float32)) > 0.0

    def search_step(it, st):
        lo, hi, clo, chi = st
        act = active_rows(lo, hi, clo)
        mid = lo * 0.5 + hi * 0.5
        near_zero = jnp.where(lo >= 0.0, TINY, -TINY)
        geo = jnp.sqrt(jnp.abs(lo)) * jnp.sqrt(jnp.abs(hi)) * jnp.where(lo >= 0.0, 1.0, -1.0)
        guess = jnp.where(it == 0, q_below, jnp.where(it == 1, q_above,
                          jnp.where(it == 2, 0.0, jnp.where(it == 3, near_zero,
                          jnp.where(it < SEARCH_VALUE_IT, mid, geo)))))
        cand = jnp.where(inside(guess, lo, hi), guess, mid)
        cnt = count_ge(cand)
        ge = cnt >= topk
        up_lo = act & ge
        up_hi = act & jnp.logical_not(ge)
        return (jnp.where(up_lo, cand, lo), jnp.where(up_hi, cand, hi),
                jnp.where(up_lo, cnt, clo), jnp.where(up_hi, cnt, chi))

    n_adm = jnp.minimum(l_real, (q_chunk + 1) * CHUNK)
    lo0 = jnp.min(smin8, axis=0, keepdims=True)
    smax = jnp.max(smax8, axis=0, keepdims=True)
    hi0 = smax + jnp.abs(smax) * 2.0 ** -10 + 1e-30
    n_stat = nkt * KT // STAT_STRIDE
    mean = jnp.sum(sum8, axis=0, keepdims=True) * (1.0 / n_stat)
    std = jnp.sqrt(jnp.maximum(jnp.sum(sq8, axis=0, keepdims=True) * (1.0 / n_stat) - mean * mean, 0.0))
    z_q = jnp.log(jnp.maximum(n_adm - topk, 1).astype(jnp.float32) * (1.0 / topk)) * (1.0 / 1.7)
    q_below = mean + (z_q - STAT_MARGIN) * std
    q_above = mean + (z_q + STAT_MARGIN) * std
    st = (lo0, hi0, n_adm, jnp.zeros((1, QB), jnp.int32))
    n_first = jnp.where(any_active(st), SEARCH_FIRST_IT, 0)
    st = lax.fori_loop(0, n_first, search_step, st)

    def search_cond(c):
        return (c[0] < SEARCH_MAX_IT) & any_active(c[1])

    def search_body(c):
        it, st = c
        return it + 2, search_step(it + 1, search_step(it, st))

    _, (thr, _, n_ge, n_gt) = lax.while_loop(search_cond, search_body, (n_first, st))
    need = topk - n_gt
    any_tie = jnp.max((n_ge > topk).astype(jnp.float32)) > 0.0

    def bias_body(kt, carry):
        k0 = kt * KT
        bias_ref[pl.ds(k0, KT), :] = jnp.where(sc_ref[pl.ds(k0, KT), :] >= thr, 0.0, NEG_BIG)
        return carry

    _unrolled(nkt, bias_body, 0)

    @pl.when(any_tie)
    def _():
        tri = _bf((lax.broadcasted_iota(jnp.int32, (LANES, LANES), 1)
                   <= lax.broadcasted_iota(jnp.int32, (LANES, LANES), 0)).astype(jnp.float32))
        need_f = need.astype(jnp.float32)
        untied = n_ge <= topk

        def tie_body(kt, seen):
            for r in range(KT // LANES):
                k0 = kt * KT + r * LANES
                sc = sc_ref[pl.ds(k0, LANES), :]
                eq = sc == thr
                eq_f = eq.astype(jnp.float32)
                rank = _dot(tri, _bf(eq_f)) + seen
                sel = (sc > thr) | (eq & ((rank <= need_f) | untied))
                bias_ref[pl.ds(k0, LANES), :] = jnp.where(sel, 0.0, NEG_BIG)
                seen = seen + jnp.sum(eq_f, axis=0, keepdims=True)
            return seen

        _unrolled(nkt, tie_body, jnp.zeros((1, QB), jnp.float32))

    gw = grp * QB
    m8_init = tuple(jnp.full((SUBLANES, QB), NEG_BIG, jnp.float32) for _ in range(grp))
    l8_init = jnp.zeros((SUBLANES, gw), jnp.float32)
    q_groups = [jnp.concatenate(
        [q_ref[0, :, (g * grp + hh) * HEAD_DIM:(g * grp + hh + 1) * HEAD_DIM]
         for hh in range(grp)], axis=0) for g in range(N_KV)]

    def qk_tile(g, kt, m8):
        k0 = kt * KT
        kb = k_ref[0, pl.ds(k0, KT), g * HEAD_DIM:(g + 1) * HEAD_DIM]
        bias = bias_ref[pl.ds(k0, KT), :]
        s = _dot_nt(kb, q_groups[g])
        out = []
        for hh in range(grp):
            sh = s[:, hh * QB:(hh + 1) * QB] + bias
            s_ref[pl.ds(k0, KT), g * gw + hh * QB:g * gw + (hh + 1) * QB] = sh
            out.append(jnp.maximum(
                m8[hh], _fold_rows(sh, jnp.max)))
        return tuple(out)

    def pv_tile(g, kt, l8, m_all):
        k0 = kt * KT
        p = jnp.exp2(s_ref[pl.ds(k0, KT), g * gw:(g + 1) * gw] - m_all)
        pb = _bf(p)
        acc_ref[...] += sum(
            _dot(vt_ref[0, kt * vt_per_kt + r, g * HEAD_DIM:(g + 1) * HEAD_DIM, :],
                 pb[r * vt_w:(r + 1) * vt_w, :])
            for r in range(vt_per_kt))
        return l8 + _fold_rows(p, jnp.sum)

    def col_max(m8):
        return jnp.concatenate([jnp.max(m, axis=0, keepdims=True) for m in m8], axis=1)

    def finish_group(g, l8):
        o_t = acc_ref[...] * (1.0 / jnp.sum(l8, axis=0, keepdims=True))
        for hh in range(grp):
            h_abs = g * grp + hh
            o_ref[:, h_abs * HEAD_DIM:(h_abs + 1) * HEAD_DIM] = o_t[:, hh * QB:(hh + 1) * QB].T

    m8 = _unrolled(nkt, functools.partial(qk_tile, 0), m8_init)
    for g in range(N_KV):
        m_all = col_max(m8)
        acc_ref[...] = jnp.zeros((HEAD_DIM, gw), jnp.float32)
        if g + 1 < N_KV:
            def both(kt, carry, g=g, m_all=m_all):
                return pv_tile(g, kt, carry[0], m_all), qk_tile(g + 1, kt, carry[1])
            l8, m8 = _unrolled(nkt, both, (l8_init, m8_init))
        else:
            l8 = _unrolled(nkt, lambda kt, l8, g=g, m_all=m_all: pv_tile(g, kt, l8, m_all),
                           l8_init)
        finish_group(g, l8)

    a_out = _dot(_bf(o_ref[...] * sza_ref[0]), wpa_ref[...])
    merged = sga_ref[0] * a_out + gbb_ref[0]
    y_ref[0] = x_ref[0] + gate_ref[0] * _dot(_bf(merged), wout_ref[...])


def _attn_call(q, qi, wi, k_all, vt_all, ki2_all, sza, sga, gbb, x, gate, w_pa, w_out,
               *, pos0, l_real, kt_size, topk, qb):
    b, t, d = x.shape
    lp = k_all.shape[1]
    nkt_values = sorted({_key_tiles(pos0, jj, l_real=l_real, kt_size=kt_size, qb=qb, minimum=min)
                         for jj in range(t // qb)})
    tok = lambda w: pl.BlockSpec((1, qb, w), lambda i, j: (i, j, 0))
    per_b = lambda r, w: pl.BlockSpec((1, r, w), lambda i, j: (i, 0, 0))
    whole = pl.BlockSpec(memory_space=pltpu.VMEM)
    in_specs = [
        tok(ATT_W), tok(IDX_HEADS * IDX_DIM), tok(LANES),
        per_b(lp, KV_W),
        pl.BlockSpec((1,) + vt_all.shape[1:], lambda i, j: (i, 0, 0, 0)),
        per_b(lp, LANES),
        tok(ATT_W), tok(D_MODEL), tok(D_MODEL), tok(d), per_b(1, d),
        whole, whole,
    ]
    return pl.pallas_call(
        functools.partial(_attn_kernel, pos0=pos0, l_real=l_real, kt_size=kt_size, topk=topk,
                          qb=qb, nkt_values=nkt_values),
        grid=(b, t // qb),
        in_specs=in_specs,
        out_specs=tok(d),
        out_shape=jax.ShapeDtypeStruct((b, t, d), jnp.float32),
        scratch_shapes=[
            pltpu.VMEM((lp, qb), jnp.float32),
            pltpu.VMEM((lp, qb), jnp.float32),
            pltpu.VMEM((lp, N_HEADS * qb), jnp.float32),
            pltpu.VMEM((HEAD_DIM, N_HEADS // N_KV * qb), jnp.float32),
            pltpu.VMEM((qb, ATT_W), jnp.float32),
        ],
        compiler_params=pltpu.CompilerParams(
            dimension_semantics=("arbitrary", "arbitrary"), vmem_limit_bytes=VMEM_LIMIT),
        name="dsa_attn",
    )(q, qi, wi, k_all, vt_all, ki2_all, sza, sga, gbb, x, gate, w_pa, w_out)


def _rope_tables(pos, width, period):
    rot = period // ROT_FRAC
    half = rot // 2
    inv = ROPE_THETA ** (-2.0 * jnp.arange(half, dtype=jnp.float32) / rot)
    ang = pos.astype(jnp.float32)[:, None] * inv[None, :]
    cos, sin = jnp.cos(ang), jnp.sin(ang)
    n = pos.shape[0]
    rest = period - rot
    one = jnp.ones((n, rest), jnp.float32)
    zero_h = jnp.zeros((n, half), jnp.float32)
    zero_r = jnp.zeros((n, rest), jnp.float32)
    c = jnp.concatenate([cos, cos, one], axis=1)
    s_up = jnp.concatenate([-sin, zero_h, zero_r], axis=1)
    s_dn = jnp.concatenate([zero_h, sin, zero_r], axis=1)
    rep = width // period
    return tuple(jnp.tile(a, (1, rep)) for a in (c, s_up, s_dn))


def _prep_w_in(w_in):
    o_ki = ATT_W + 2 * KV_W + ATT_W + IDX_HEADS * IDX_DIM
    o_wi = o_ki + IDX_DIM
    o_u = o_wi + IDX_HEADS
    ki = w_in[:, o_ki:o_wi]
    pad = jnp.zeros((w_in.shape[0], C_END - C_WI - IDX_HEADS), w_in.dtype)
    w_c = jnp.concatenate([ki, ki, w_in[:, o_wi:o_u], pad], axis=1)
    return _bf(w_in[:, :o_ki]), _bf(w_in[:, o_u:]), _bf(w_c)


def _layer(x, shift, scale, gate, pos, conv_state, past, weights, *, tm, t_real, kt_size, qb):
    norm_g, qg, kg, w_r, w_pb, conv_w, w_pa, w_out = weights
    b, t, _ = x.shape
    tabs = _rope_tables(pos, LANES, HEAD_DIM) + _rope_tables(pos, LANES, IDX_DIM)
    conv_row = t_real - (t // tm - 1) * tm
    (q, k, v, kb, vt, sza, qi, ki, ki2, wi, sga, gbb, conv_new) = _proj_call(
        x, shift, scale, norm_g, qg, kg, tabs, w_r, w_pb, conv_w, conv_state,
        tm=tm, conv_row=conv_row)
    if past is None:
        p_len = 0
        k_all, vt_all, ki2_all = kb, vt, ki2
    else:
        past_k, past_v, past_ki = past
        p_len = past_k.shape[1]
        lp = -(-(p_len + t) // kt_size) * kt_size
        extra = lp - p_len - t
        k_all = jnp.concatenate(
            [_bf(past_k), kb, jnp.zeros((b, extra, KV_W), jnp.bfloat16)], axis=1)
        vt_all = jnp.concatenate(
            [_bf(jnp.swapaxes(past_v, 1, 2)), vt[:, 0], jnp.zeros((b, KV_W, extra), jnp.bfloat16)],
            axis=2)
        vt_all = jnp.swapaxes(vt_all.reshape(b, KV_W, lp // kt_size, kt_size), 1, 2)
        ki2_all = jnp.concatenate(
            [jnp.concatenate([past_ki, past_ki], axis=-1), ki2,
             jnp.zeros((b, extra, LANES), jnp.float32)], axis=1)
    l_real = p_len + t_real
    topk = min(TOPK_MAX, l_real // 4)
    y = _attn_call(q, qi, wi, k_all, vt_all, ki2_all, sza, sga, gbb, x, gate, w_pa, w_out,
                   pos0=p_len, l_real=l_real, kt_size=kt_size, topk=topk, qb=qb)
    return y, k, v, ki, conv_new


def kernel(x_prompt, x_sample, cache_k, cache_v, cache_idx_k, state_conv, c_prompt, c_sample,
           w_ada, b_ada, norm_g, w_in, q_norm_g, k_norm_g, conv_w, w_pa, w_pb, w_out):
    bp, seq, d = x_prompt.shape
    bs, dec_seq, _ = x_sample.shape
    past_len = cache_k.shape[1]

    mod = _mod_call(jnp.concatenate([c_prompt, c_sample], axis=0), w_ada, b_ada)
    shift, scale, gate = (m.reshape(bp + bs, 1, d) for m in jnp.split(mod, 3, axis=-1))

    weights = (norm_g.reshape(1, d), q_norm_g.reshape(1, HEAD_DIM), k_norm_g.reshape(1, HEAD_DIM),
               _prep_w_in(w_in), _bf(w_pb), conv_w, _bf(w_pa), _bf(w_out))

    yp, k_p, v_p, ki_p, conv_p = _layer(
        x_prompt, shift[:bp], scale[:bp], gate[:bp], jnp.arange(seq, dtype=jnp.int32),
        jnp.zeros((bp, CONV_W - 1, d), x_prompt.dtype), None, weights,
        tm=256, t_real=seq, kt_size=512, qb=QB_PROMPT)

    x_pad = jnp.pad(x_sample, ((0, 0), (0, QB_SAMPLE - dec_seq), (0, 0)))
    past = (cache_k.reshape(bs, past_len, KV_W), cache_v.reshape(bs, past_len, KV_W), cache_idx_k)
    ys, k_s, v_s, ki_s, conv_s = _layer(
        x_pad, shift[bp:], scale[bp:], gate[bp:],
        past_len + jnp.arange(QB_SAMPLE, dtype=jnp.int32), state_conv, past, weights,
        tm=QB_SAMPLE, t_real=dec_seq, kt_size=384, qb=QB_SAMPLE)

    return (yp, ys[:, :dec_seq], k_p, v_p, ki_p, conv_p,
            k_s[:, :dec_seq], v_s[:, :dec_seq], ki_s[:, :dec_seq], conv_s)
```

```python
import functools
import math

import jax
import jax.numpy as jnp
from jax import lax
from jax.experimental import pallas as pl
from jax.experimental.pallas import tpu as pltpu

D_MODEL = 1024
CHUNK = 64
CHUNK_SHIFT = 6
N_HEADS = 8
N_KV = 2
HEAD_DIM = 128
ATT_W = N_HEADS * HEAD_DIM
KV_W = N_KV * HEAD_DIM
ROT_FRAC = 4
ROPE_THETA = 500000.0
IDX_HEADS = 8
IDX_DIM = 64
TOPK_MAX = 256
CONV_W = 3
EPS = 1e-6

LANES = 128
SUBLANES = 8
QB_PROMPT = 256
QB_SAMPLE = 128
Q_SCALE = HEAD_DIM ** -0.5 * math.log2(math.e)
VMEM_LIMIT = 56 * 1024 * 1024

C_Q, C_K, C_V, C_ZA, C_QI = 0, 1024, 1280, 1536, 2560
C_U, C_BG, C_CG, C_ZB, C_GA, C_GB = 3072, 4096, 5120, 6144, 7168, 8192
C_KI, C_WI, C_END = 9216, 9344, 9472

NEG_BIG = -1e30
TINY = 1.1754943508222875e-38
FOLD_CHAINS = 4
STAT_STRIDE = 4
STAT_MARGIN = 0.25
SEARCH_VALUE_IT = 18
SEARCH_FIRST_IT = 18
SEARCH_MAX_IT = SEARCH_VALUE_IT + 36

_NT = (((1,), (1,)), ((), ()))


def _bf(x):
    return x.astype(jnp.bfloat16)


def _dot(a, b):
    return jnp.dot(a, b, preferred_element_type=jnp.float32)


def _dot_nt(a, b):
    return lax.dot_general(a, b, _NT, preferred_element_type=jnp.float32)


def _silu(x):
    return x * jax.nn.sigmoid(x)


def _mod_kernel(c_ref, w_ref, b_ref, o_ref):
    c = c_ref[...]
    s = _silu(c)
    w = w_ref[...]
    s_hi = _bf(s)
    s_lo = _bf(s - s_hi.astype(jnp.float32))
    w_hi = _bf(w)
    w_lo = _bf(w - w_hi.astype(jnp.float32))
    acc = _dot(s_hi, w_hi) + (_dot(s_lo, w_hi) + _dot(s_hi, w_lo))
    o_ref[...] = acc + b_ref[...]


def _mod_call(c_all, w_ada, b_ada):
    nb, d = c_all.shape
    n = w_ada.shape[1]
    bn = 1024
    return pl.pallas_call(
        _mod_kernel,
        grid=(n // bn,),
        in_specs=[
            pl.BlockSpec((nb, d), lambda i: (0, 0)),
            pl.BlockSpec((d, bn), lambda i: (0, i)),
            pl.BlockSpec((1, bn), lambda i: (0, i)),
        ],
        out_specs=pl.BlockSpec((nb, bn), lambda i: (0, i)),
        out_shape=jax.ShapeDtypeStruct((nb, n), jnp.float32),
        compiler_params=pltpu.CompilerParams(dimension_semantics=("arbitrary",)),
        name="adaln_mod",
    )(c_all, w_ada, b_ada.reshape(1, n))


def _rope(x, cos, sin_up, sin_dn, half):
    up = pltpu.roll(x, LANES - half, 1)
    dn = pltpu.roll(x, half, 1)
    return x * cos + up * sin_up + dn * sin_dn


def _proj_kernel(x_ref, shift_ref, scale_ref, ng_ref, qg_ref, kg_ref,
                 rc_ref, rs1_ref, rs2_ref, ic_ref, is1_ref, is2_ref,
                 wa_ref, wb_ref, wc_ref, wpb_ref, cw_ref, cs_ref,
                 q_ref, k_ref, v_ref, kb_ref, vt_ref, sza_ref, qi_ref, ki_ref, ki2_ref, wi_ref,
                 sga_ref, gbb_ref, conv_ref, carry_ref, *, tm, conv_row):
    @pl.when(pl.program_id(1) == 0)
    def _():
        carry_ref[0:2, :] = cs_ref[0]

    x = x_ref[0]
    ms = jnp.mean(x * x, axis=-1, keepdims=True)
    xn = x * lax.rsqrt(ms + EPS) * ng_ref[...]
    h = xn * (1.0 + scale_ref[0]) + shift_ref[0]
    hb = _bf(h)

    def proj(c0, c1):
        for w_ref, base in ((wc_ref, C_KI), (wb_ref, C_U), (wa_ref, C_Q)):
            if c0 >= base:
                return _dot(hb, w_ref[:, c0 - base:c1 - base])

    rc, rs1, rs2 = rc_ref[...], rs1_ref[...], rs2_ref[...]
    ic, is1, is2 = ic_ref[...], is1_ref[...], is2_ref[...]
    rot_half = HEAD_DIM // ROT_FRAC // 2
    idx_half = IDX_DIM // ROT_FRAC // 2

    def head_norm_rope(xh, g):
        r = lax.rsqrt(jnp.mean(xh * xh, axis=-1, keepdims=True) + EPS)
        return _rope(xh * r * g, rc, rs1, rs2, rot_half)

    q = proj(C_Q, C_K)
    qg = qg_ref[...]
    for hh in range(N_HEADS):
        sl = slice(hh * HEAD_DIM, (hh + 1) * HEAD_DIM)
        q_ref[0, :, sl] = _bf(head_norm_rope(q[:, sl], qg) * Q_SCALE)

    k = proj(C_K, C_V)
    kg = kg_ref[...]
    v = proj(C_V, C_ZA)
    for hh in range(N_KV):
        sl = slice(hh * HEAD_DIM, (hh + 1) * HEAD_DIM)
        kh = head_norm_rope(k[:, sl], kg)
        k_ref[0, :, hh, :] = kh
        kb_ref[0, :, sl] = _bf(kh)
        v_ref[0, :, hh, :] = v[:, sl]
    vt_ref[0, 0] = _bf(v.T)

    sza_ref[0] = _silu(proj(C_ZA, C_QI))

    qi = proj(C_QI, C_U)
    for p in range(IDX_HEADS * IDX_DIM // LANES):
        sl = slice(p * LANES, (p + 1) * LANES)
        qi_ref[0, :, sl] = _bf(_rope(qi[:, sl], ic, is1, is2, idx_half) * (IDX_DIM ** -0.5))

    ki2 = _rope(proj(C_KI, C_WI), ic, is1, is2, idx_half)
    ki2_ref[0] = ki2
    ki_ref[0] = ki2[:, :IDX_DIM]
    wi_ref[0] = proj(C_WI, C_END) * (IDX_HEADS ** -0.5)

    cv = proj(C_CG, C_ZB) * proj(C_U, C_BG)
    c0 = carry_ref[0:1, :]
    c1 = carry_ref[1:2, :]
    row = lax.broadcasted_iota(jnp.int32, (tm, 1), 0)
    r1 = jnp.where(row == 0, c1, pltpu.roll(cv, 1, 0))
    r2 = jnp.where(row == 0, c0, jnp.where(row == 1, c1, pltpu.roll(cv, 2, 0)))
    cw = cw_ref[...]
    y_conv = cw[0:1, :] * r2 + cw[1:2, :] * r1 + cw[2:3, :] * cv
    carry_ref[0:2, :] = cv[tm - 2:tm, :]
    conv_ref[0] = cv[conv_row - 2:conv_row, :]

    tb = proj(C_BG, C_CG) * y_conv * _silu(proj(C_ZB, C_GA))
    b_out = _dot(_bf(tb), wpb_ref[...])
    gbb_ref[0] = jax.nn.sigmoid(proj(C_GB, C_KI)) * b_out
    sga_ref[0] = jax.nn.sigmoid(proj(C_GA, C_GB))


def _proj_call(x, shift, scale, norm_g, qg, kg, rope_tabs, w_r, w_pb, conv_w, conv_state,
               *, tm, conv_row):
    b, t, d = x.shape
    nt = t // tm
    tok = lambda w: pl.BlockSpec((1, tm, w), lambda i, j: (i, j, 0))
    per_b = lambda r, w: pl.BlockSpec((1, r, w), lambda i, j: (i, 0, 0))
    const = lambda r, w: pl.BlockSpec((r, w), lambda i, j: (0, 0))
    tab = pl.BlockSpec((tm, LANES), lambda i, j: (j, 0))
    kv4 = pl.BlockSpec((1, tm, N_KV, HEAD_DIM), lambda i, j: (i, j, 0, 0))
    whole = pl.BlockSpec(memory_space=pltpu.VMEM)
    f32, bf16 = jnp.float32, jnp.bfloat16
    out_shape = (
        jax.ShapeDtypeStruct((b, t, ATT_W), bf16),
        jax.ShapeDtypeStruct((b, t, N_KV, HEAD_DIM), f32),
        jax.ShapeDtypeStruct((b, t, N_KV, HEAD_DIM), f32),
        jax.ShapeDtypeStruct((b, t, KV_W), bf16),
        jax.ShapeDtypeStruct((b, nt, KV_W, tm), bf16),
        jax.ShapeDtypeStruct((b, t, ATT_W), f32),
        jax.ShapeDtypeStruct((b, t, IDX_HEADS * IDX_DIM), bf16),
        jax.ShapeDtypeStruct((b, t, IDX_DIM), f32),
        jax.ShapeDtypeStruct((b, t, LANES), f32),
        jax.ShapeDtypeStruct((b, t, LANES), f32),
        jax.ShapeDtypeStruct((b, t, D_MODEL), f32),
        jax.ShapeDtypeStruct((b, t, D_MODEL), f32),
        jax.ShapeDtypeStruct((b, CONV_W - 1, D_MODEL), f32),
    )
    out_specs = (
        tok(ATT_W), kv4, kv4, tok(KV_W),
        pl.BlockSpec((1, 1, KV_W, tm), lambda i, j: (i, j, 0, 0)),
        tok(ATT_W), tok(IDX_HEADS * IDX_DIM), tok(IDX_DIM), tok(LANES), tok(LANES),
        tok(D_MODEL), tok(D_MODEL), per_b(CONV_W - 1, D_MODEL),
    )
    in_specs = [
        tok(d), per_b(1, d), per_b(1, d), const(1, d), const(1, LANES), const(1, LANES),
        tab, tab, tab, tab, tab, tab,
        whole, whole, whole, whole, const(CONV_W, d), per_b(CONV_W - 1, d),
    ]
    return pl.pallas_call(
        functools.partial(_proj_kernel, tm=tm, conv_row=conv_row),
        grid=(b, nt),
        in_specs=in_specs,
        out_specs=out_specs,
        out_shape=out_shape,
        scratch_shapes=[pltpu.VMEM((SUBLANES, d), jnp.float32)],
        compiler_params=pltpu.CompilerParams(
            dimension_semantics=("arbitrary", "arbitrary"), vmem_limit_bytes=VMEM_LIMIT),
        name="in_proj",
    )(x, shift, scale, norm_g, qg, kg, *rope_tabs, *w_r, w_pb, conv_w, conv_state)


def _key_tiles(pos0, j, *, l_real, kt_size, qb, minimum=jnp.minimum):
    last_q = pos0 + j * qb + qb - 1
    kmax = minimum(l_real, ((last_q >> CHUNK_SHIFT) + 1) * CHUNK)
    return (kmax + kt_size - 1) // kt_size


def _fold_rows(x, op):
    rows, w = x.shape
    part = op(x.reshape(FOLD_CHAINS, rows // (FOLD_CHAINS * SUBLANES), SUBLANES, w), axis=1)
    return op(part, axis=0)


def _unrolled(n, body, init):
    carry = init
    for i in range(n):
        carry = body(i, carry)
    return carry


def _attn_kernel(*refs, pos0, l_real, kt_size, topk, qb, nkt_values):
    nkt = _key_tiles(pos0, pl.program_id(1), l_real=l_real, kt_size=kt_size, qb=qb)
    for c in nkt_values:
        run = functools.partial(_attn_block, *refs, pos0=pos0, l_real=l_real, kt_size=kt_size,
                                topk=topk, qb=qb, nkt=c)
        if len(nkt_values) == 1:
            run()
        else:
            pl.when(nkt == c)(run)


def _attn_block(q_ref, qi_ref, wi_ref, k_ref, vt_ref, ki2_ref,
                sza_ref, sga_ref, gbb_ref, x_ref, gate_ref, wpa_ref, wout_ref,
                y_ref, sc_ref, bias_ref, s_ref, acc_ref, o_ref,
                *, pos0, l_real, kt_size, topk, qb, nkt):
    KT = kt_size
    QB = qb
    grp = N_HEADS // N_KV
    vt_w = vt_ref.shape[3]
    vt_per_kt = KT // vt_w
    q0 = pos0 + pl.program_id(1) * QB

    lane = lax.broadcasted_iota(jnp.int32, (1, QB), 1)
    q_chunk = (q0 + lane) >> CHUNK_SHIFT
    w_t = wi_ref[0].T
    qi = qi_ref[0]
    lane_kt = lax.broadcasted_iota(jnp.int32, (KT, LANES), 1)
    sub_kt = lax.broadcasted_iota(jnp.int32, (KT, 1), 0)

    def score_body(kt, carry):
        smax8, smin8, sum8, sq8 = carry
        k0 = kt * KT
        ki2 = ki2_ref[0, pl.ds(k0, KT), :]
        ki_lo = _bf(jnp.where(lane_kt < IDX_DIM, ki2, 0.0))
        ki_hi = _bf(jnp.where(lane_kt >= IDX_DIM, ki2, 0.0))
        acc = jnp.zeros((KT, QB), jnp.float32)
        for p in range(IDX_HEADS // 2):
            slab = qi[:, p * LANES:(p + 1) * LANES]
            acc = acc + w_t[2 * p:2 * p + 1, :] * jnp.maximum(_dot_nt(ki_lo, slab), 0.0)
            acc = acc + w_t[2 * p + 1:2 * p + 2, :] * jnp.maximum(_dot_nt(ki_hi, slab), 0.0)
        kpos = k0 + sub_kt
        adm = ((kpos >> CHUNK_SHIFT) <= q_chunk) & (kpos < l_real)
        sc_ref[pl.ds(k0, KT), :] = jnp.where(adm, acc, -jnp.inf)
        sub = acc.reshape(KT // (STAT_STRIDE * SUBLANES), STAT_STRIDE, SUBLANES, QB)[:, 0]
        return (jnp.maximum(smax8, _fold_rows(acc, jnp.max)),
                jnp.minimum(smin8, _fold_rows(acc, jnp.min)),
                sum8 + jnp.sum(sub, axis=0), sq8 + jnp.sum(sub * sub, axis=0))

    smax8, smin8, sum8, sq8 = _unrolled(
        nkt, score_body,
        (jnp.full((SUBLANES, QB), -jnp.inf, jnp.float32),
         jnp.full((SUBLANES, QB), jnp.inf, jnp.float32),
         jnp.zeros((SUBLANES, QB), jnp.float32), jnp.zeros((SUBLANES, QB), jnp.float32)))

    def count_ge(cand):
        def body(kt, acc):
            k0 = kt * KT
            m = (sc_ref[pl.ds(k0, KT), :] >= cand).astype(jnp.int32)
            return acc + _fold_rows(m, jnp.sum)
        acc = _unrolled(nkt, body, jnp.zeros((SUBLANES, QB), jnp.int32))
        return jnp.sum(acc, axis=0, keepdims=True)

    def inside(x, lo, hi):
        return (x > lo) & (x < hi)

    def active_rows(lo, hi, clo):
        return (clo > topk) & inside(lo * 0.5 + hi * 0.5, lo, hi)

    def any_active(st):
        lo, hi, clo, _ = st
        return jnp.max(active_rows(lo, hi, clo).astype(jnp.float32)) > 0.0

    def search_step(it, st):
        lo, hi, clo, chi = st
        act = active_rows(lo, hi, clo)
        mid = lo * 0.5 + hi * 0.5
        near_zero = jnp.where(lo >= 0.0, TINY, -TINY)
        geo = jnp.sqrt(jnp.abs(lo)) * jnp.sqrt(jnp.abs(hi)) * jnp.where(lo >= 0.0, 1.0, -1.0)
        guess = jnp.where(it == 0, q_below, jnp.where(it == 1, q_above,
                          jnp.where(it == 2, 0.0, jnp.where(it == 3, near_zero,
                          jnp.where(it < SEARCH_VALUE_IT, mid, geo)))))
        cand = jnp.where(inside(guess, lo, hi), guess, mid)
        cnt = count_ge(cand)
        ge = cnt >= topk
        up_lo = act & ge
        up_hi = act & jnp.logical_not(ge)
        return (jnp.where(up_lo, cand, lo), jnp.where(up_hi, cand, hi),
                jnp.where(up_lo, cnt, clo), jnp.where(up_hi, cnt, chi))

    n_adm = jnp.minimum(l_real, (q_chunk + 1) * CHUNK)
    lo0 = jnp.min(smin8, axis=0, keepdims=True)
    smax = jnp.max(smax8, axis=0, keepdims=True)
    hi0 = smax + jnp.abs(smax) * 2.0 ** -10 + 1e-30
    n_stat = nkt * KT // STAT_STRIDE
    mean = jnp.sum(sum8, axis=0, keepdims=True) * (1.0 / n_stat)
    std = jnp.sqrt(jnp.maximum(jnp.sum(sq8, axis=0, keepdims=True) * (1.0 / n_stat) - mean * mean, 0.0))
    z_q = jnp.log(jnp.maximum(n_adm - topk, 1).astype(jnp.float32) * (1.0 / topk)) * (1.0 / 1.7)
    q_below = mean + (z_q - STAT_MARGIN) * std
    q_above = mean + (z_q + STAT_MARGIN) * std
    st = (lo0, hi0, n_adm, jnp.zeros((1, QB), jnp.int32))
    n_first = jnp.where(any_active(st), SEARCH_FIRST_IT, 0)
    st = lax.fori_loop(0, n_first, search_step, st)

    def search_cond(c):
        return (c[0] < SEARCH_MAX_IT) & any_active(c[1])

    def search_body(c):
        it, st = c
        return it + 2, search_step(it + 1, search_step(it, st))

    _, (thr, _, n_ge, n_gt) = lax.while_loop(search_cond, search_body, (n_first, st))
    need = topk - n_gt
    any_tie = jnp.max((n_ge > topk).astype(jnp.float32)) > 0.0

    def bias_body(kt, carry):
        k0 = kt * KT
        bias_ref[pl.ds(k0, KT), :] = jnp.where(sc_ref[pl.ds(k0, KT), :] >= thr, 0.0, NEG_BIG)
        return carry

    _unrolled(nkt, bias_body, 0)

    @pl.when(any_tie)
    def _():
        tri = _bf((lax.broadcasted_iota(jnp.int32, (LANES, LANES), 1)
                   <= lax.broadcasted_iota(jnp.int32, (LANES, LANES), 0)).astype(jnp.float32))
        need_f = need.astype(jnp.float32)
        untied = n_ge <= topk

        def tie_body(kt, seen):
            for r in range(KT // LANES):
                k0 = kt * KT + r * LANES
                sc = sc_ref[pl.ds(k0, LANES), :]
                eq = sc == thr
                eq_f = eq.astype(jnp.float32)
                rank = _dot(tri, _bf(eq_f)) + seen
                sel = (sc > thr) | (eq & ((rank <= need_f) | untied))
                bias_ref[pl.ds(k0, LANES), :] = jnp.where(sel, 0.0, NEG_BIG)
                seen = seen + jnp.sum(eq_f, axis=0, keepdims=True)
            return seen

        _unrolled(nkt, tie_body, jnp.zeros((1, QB), jnp.float32))

    gw = grp * QB
    m8_init = tuple(jnp.full((SUBLANES, QB), NEG_BIG, jnp.float32) for _ in range(grp))
    l8_init = jnp.zeros((SUBLANES, gw), jnp.float32)
    q_groups = [jnp.concatenate(
        [q_ref[0, :, (g * grp + hh) * HEAD_DIM:(g * grp + hh + 1) * HEAD_DIM]
         for hh in range(grp)], axis=0) for g in range(N_KV)]

    def qk_tile(g, kt, m8):
        k0 = kt * KT
        kb = k_ref[0, pl.ds(k0, KT), g * HEAD_DIM:(g + 1) * HEAD_DIM]
        bias = bias_ref[pl.ds(k0, KT), :]
        s = _dot_nt(kb, q_groups[g])
        out = []
        for hh in range(grp):
            sh = s[:, hh * QB:(hh + 1) * QB] + bias
            s_ref[pl.ds(k0, KT), g * gw + hh * QB:g * gw + (hh + 1) * QB] = sh
            out.append(jnp.maximum(
                m8[hh], _fold_rows(sh, jnp.max)))
        return tuple(out)

    def pv_tile(g, kt, l8, m_all):
        k0 = kt * KT
        p = jnp.exp2(s_ref[pl.ds(k0, KT), g * gw:(g + 1) * gw] - m_all)
        pb = _bf(p)
        acc_ref[...] += sum(
            _dot(vt_ref[0, kt * vt_per_kt + r, g * HEAD_DIM:(g + 1) * HEAD_DIM, :],
                 pb[r * vt_w:(r + 1) * vt_w, :])
            for r in range(vt_per_kt))
        return l8 + _fold_rows(p, jnp.sum)

    def col_max(m8):
        return jnp.concatenate([jnp.max(m, axis=0, keepdims=True) for m in m8], axis=1)

    def finish_group(g, l8):
        o_t = acc_ref[...] * (1.0 / jnp.sum(l8, axis=0, keepdims=True))
        for hh in range(grp):
            h_abs = g * grp + hh
            o_ref[:, h_abs * HEAD_DIM:(h_abs + 1) * HEAD_DIM] = o_t[:, hh * QB:(hh + 1) * QB].T

    m8 = _unrolled(nkt, functools.partial(qk_tile, 0), m8_init)
    for g in range(N_KV):
        m_all = col_max(m8)
        acc_ref[...] = jnp.zeros((HEAD_DIM, gw), jnp.float32)
        if g + 1 < N_KV:
            def both(kt, carry, g=g, m_all=m_all):
                return pv_tile(g, kt, carry[0], m_all), qk_tile(g + 1, kt, carry[1])
            l8, m8 = _unrolled(nkt, both, (l8_init, m8_init))
        else:
            l8 = _unrolled(nkt, lambda kt, l8, g=g, m_all=m_all: pv_tile(g, kt, l8, m_all),
                           l8_init)
        finish_group(g, l8)

    a_out = _dot(_bf(o_ref[...] * sza_ref[0]), wpa_ref[...])
    merged = sga_ref[0] * a_out + gbb_ref[0]
    y_ref[0] = x_ref[0] + gate_ref[0] * _dot(_bf(merged), wout_ref[...])


def _attn_call(q, qi, wi, k_all, vt_all, ki2_all, sza, sga, gbb, x, gate, w_pa, w_out,
               *, pos0, l_real, kt_size, topk, qb):
    b, t, d = x.shape
    lp = k_all.shape[1]
    nkt_values = sorted({_key_tiles(pos0, jj, l_real=l_real, kt_size=kt_size, qb=qb, minimum=min)
                         for jj in range(t // qb)})
    tok = lambda w: pl.BlockSpec((1, qb, w), lambda i, j: (i, j, 0))
    per_b = lambda r, w: pl.BlockSpec((1, r, w), lambda i, j: (i, 0, 0))
    whole = pl.BlockSpec(memory_space=pltpu.VMEM)
    in_specs = [
        tok(ATT_W), tok(IDX_HEADS * IDX_DIM), tok(LANES),
        per_b(lp, KV_W),
        pl.BlockSpec((1,) + vt_all.shape[1:], lambda i, j: (i, 0, 0, 0)),
        per_b(lp, LANES),
        tok(ATT_W), tok(D_MODEL), tok(D_MODEL), tok(d), per_b(1, d),
        whole, whole,
    ]
    return pl.pallas_call(
        functools.partial(_attn_kernel, pos0=pos0, l_real=l_real, kt_size=kt_size, topk=topk,
                          qb=qb, nkt_values=nkt_values),
        grid=(b, t // qb),
        in_specs=in_specs,
        out_specs=tok(d),
        out_shape=jax.ShapeDtypeStruct((b, t, d), jnp.float32),
        scratch_shapes=[
            pltpu.VMEM((lp, qb), jnp.float32),
            pltpu.VMEM((lp, qb), jnp.float32),
            pltpu.VMEM((lp, N_HEADS * qb), jnp.float32),
            pltpu.VMEM((HEAD_DIM, N_HEADS // N_KV * qb), jnp.float32),
            pltpu.VMEM((qb, ATT_W), jnp.float32),
        ],
        compiler_params=pltpu.CompilerParams(
            dimension_semantics=("arbitrary", "arbitrary"), vmem_limit_bytes=VMEM_LIMIT),
        name="dsa_attn",
    )(q, qi, wi, k_all, vt_all, ki2_all, sza, sga, gbb, x, gate, w_pa, w_out)


def _rope_tables(pos, width, period):
    rot = period // ROT_FRAC
    half = rot // 2
    inv = ROPE_THETA ** (-2.0 * jnp.arange(half, dtype=jnp.float32) / rot)
    ang = pos.astype(jnp.float32)[:, None] * inv[None, :]
    cos, sin = jnp.cos(ang), jnp.sin(ang)
    n = pos.shape[0]
    rest = period - rot
    one = jnp.ones((n, rest), jnp.float32)
    zero_h = jnp.zeros((n, half), jnp.float32)
    zero_r = jnp.zeros((n, rest), jnp.float32)
    c = jnp.concatenate([cos, cos, one], axis=1)
    s_up = jnp.concatenate([-sin, zero_h, zero_r], axis=1)
    s_dn = jnp.concatenate([zero_h, sin, zero_r], axis=1)
    rep = width // period
    return tuple(jnp.tile(a, (1, rep)) for a in (c, s_up, s_dn))


def _prep_w_in(w_in):
    o_ki = ATT_W + 2 * KV_W + ATT_W + IDX_HEADS * IDX_DIM
    o_wi = o_ki + IDX_DIM
    o_u = o_wi + IDX_HEADS
    ki = w_in[:, o_ki:o_wi]
    pad = jnp.zeros((w_in.shape[0], C_END - C_WI - IDX_HEADS), w_in.dtype)
    w_c = jnp.concatenate([ki, ki, w_in[:, o_wi:o_u], pad], axis=1)
    return _bf(w_in[:, :o_ki]), _bf(w_in[:, o_u:]), _bf(w_c)


def _layer(x, shift, scale, gate, pos, conv_state, past, weights, *, tm, t_real, kt_size, qb):
    norm_g, qg, kg, w_r, w_pb, conv_w, w_pa, w_out = weights
    b, t, _ = x.shape
    tabs = _rope_tables(pos, LANES, HEAD_DIM) + _rope_tables(pos, LANES, IDX_DIM)
    conv_row = t_real - (t // tm - 1) * tm
    (q, k, v, kb, vt, sza, qi, ki, ki2, wi, sga, gbb, conv_new) = _proj_call(
        x, shift, scale, norm_g, qg, kg, tabs, w_r, w_pb, conv_w, conv_state,
        tm=tm, conv_row=conv_row)
    if past is None:
        p_len = 0
        k_all, vt_all, ki2_all = kb, vt, ki2
    else:
        past_k, past_v, past_ki = past
        p_len = past_k.shape[1]
        lp = -(-(p_len + t) // kt_size) * kt_size
        extra = lp - p_len - t
        k_all = jnp.concatenate(
            [_bf(past_k), kb, jnp.zeros((b, extra, KV_W), jnp.bfloat16)], axis=1)
        vt_all = jnp.concatenate(
            [_bf(jnp.swapaxes(past_v, 1, 2)), vt[:, 0], jnp.zeros((b, KV_W, extra), jnp.bfloat16)],
            axis=2)
        vt_all = jnp.swapaxes(vt_all.reshape(b, KV_W, lp // kt_size, kt_size), 1, 2)
        ki2_all = jnp.concatenate(
            [jnp.concatenate([past_ki, past_ki], axis=-1), ki2,
             jnp.zeros((b, extra, LANES), jnp.float32)], axis=1)
    l_real = p_len + t_real
    topk = min(TOPK_MAX, l_real // 4)
    y = _attn_call(q, qi, wi, k_all, vt_all, ki2_all, sza, sga, gbb, x, gate, w_pa, w_out,
                   pos0=p_len, l_real=l_real, kt_size=kt_size, topk=topk, qb=qb)
    return y, k, v, ki, conv_new


def kernel(x_prompt, x_sample, cache_k, cache_v, cache_idx_k, state_conv, c_prompt, c_sample,
           w_ada, b_ada, norm_g, w_in, q_norm_g, k_norm_g, conv_w, w_pa, w_pb, w_out):
    bp, seq, d = x_prompt.shape
    bs, dec_seq, _ = x_sample.shape
    past_len = cache_k.shape[1]

    mod = _mod_call(jnp.concatenate([c_prompt, c_sample], axis=0), w_ada, b_ada)
    shift, scale, gate = (m.reshape(bp + bs, 1, d) for m in jnp.split(mod, 3, axis=-1))

    weights = (norm_g.reshape(1, d), q_norm_g.reshape(1, HEAD_DIM), k_norm_g.reshape(1, HEAD_DIM),
               _prep_w_in(w_in), _bf(w_pb), conv_w, _bf(w_pa), _bf(w_out))

    yp, k_p, v_p, ki_p, conv_p = _layer(
        x_prompt, shift[:bp], scale[:bp], gate[:bp], jnp.arange(seq, dtype=jnp.int32),
        jnp.zeros((bp, CONV_W - 1, d), x_prompt.dtype), None, weights,
        tm=256, t_real=seq, kt_size=512, qb=QB_PROMPT)

    x_pad = jnp.pad(x_sample, ((0, 0), (0, QB_SAMPLE - dec_seq), (0, 0)))
    past = (cache_k.reshape(bs, past_len, KV_W), cache_v.reshape(bs, past_len, KV_W), cache_idx_k)
    ys, k_s, v_s, ki_s, conv_s = _layer(
        x_pad, shift[bp:], scale[bp:], gate[bp:],
        past_len + jnp.arange(QB_SAMPLE, dtype=jnp.int32), state_conv, past, weights,
        tm=QB_SAMPLE, t_real=dec_seq, kt_size=384, qb=QB_SAMPLE)

    return (yp, ys[:, :dec_seq], k_p, v_p, ki_p, conv_p,
            k_s[:, :dec_seq], v_s[:, :dec_seq], ki_s[:, :dec_seq], conv_s)
```

```python
import functools
import math

import jax
import jax.numpy as jnp
from jax import lax
from jax.experimental import pallas as pl
from jax.experimental.pallas import tpu as pltpu

D_MODEL = 1024
CHUNK = 64
CHUNK_SHIFT = 6
N_HEADS = 8
N_KV = 2
HEAD_DIM = 128
ATT_W = N_HEADS * HEAD_DIM
KV_W = N_KV * HEAD_DIM
ROT_FRAC = 4
ROPE_THETA = 500000.0
IDX_HEADS = 8
IDX_DIM = 64
TOPK_MAX = 256
CONV_W = 3
EPS = 1e-6

LANES = 128
SUBLANES = 8
QB_PROMPT = 256
QB_SAMPLE = 128
Q_SCALE = HEAD_DIM ** -0.5 * math.log2(math.e)
VMEM_LIMIT = 56 * 1024 * 1024

C_Q, C_K, C_V, C_ZA, C_QI = 0, 1024, 1280, 1536, 2560
C_U, C_BG, C_CG, C_ZB, C_GA, C_GB = 3072, 4096, 5120, 6144, 7168, 8192
C_KI, C_WI, C_END = 9216, 9344, 9472

NEG_BIG = -1e30
TINY = 1.1754943508222875e-38
FOLD_CHAINS = 4
STAT_STRIDE = 4
STAT_MARGIN = 0.25
SEARCH_VALUE_IT = 18
SEARCH_FIRST_IT = 18
SEARCH_MAX_IT = SEARCH_VALUE_IT + 36

_NT = (((1,), (1,)), ((), ()))


def _bf(x):
    return x.astype(jnp.bfloat16)


def _dot(a, b):
    return jnp.dot(a, b, preferred_element_type=jnp.float32)


def _dot_nt(a, b):
    return lax.dot_general(a, b, _NT, preferred_element_type=jnp.float32)


def _silu(x):
    return x * jax.nn.sigmoid(x)


def _mod_kernel(c_ref, w_ref, b_ref, o_ref):
    c = c_ref[...]
    s = _silu(c)
    w = w_ref[...]
    s_hi = _bf(s)
    s_lo = _bf(s - s_hi.astype(jnp.float32))
    w_hi = _bf(w)
    w_lo = _bf(w - w_hi.astype(jnp.float32))
    acc = _dot(s_hi, w_hi) + (_dot(s_lo, w_hi) + _dot(s_hi, w_lo))
    o_ref[...] = acc + b_ref[...]


def _mod_call(c_all, w_ada, b_ada):
    nb, d = c_all.shape
    n = w_ada.shape[1]
    bn = 1024
    return pl.pallas_call(
        _mod_kernel,
        grid=(n // bn,),
        in_specs=[
            pl.BlockSpec((nb, d), lambda i: (0, 0)),
            pl.BlockSpec((d, bn), lambda i: (0, i)),
            pl.BlockSpec((1, bn), lambda i: (0, i)),
        ],
        out_specs=pl.BlockSpec((nb, bn), lambda i: (0, i)),
        out_shape=jax.ShapeDtypeStruct((nb, n), jnp.float32),
        compiler_params=pltpu.CompilerParams(dimension_semantics=("arbitrary",)),
        name="adaln_mod",
    )(c_all, w_ada, b_ada.reshape(1, n))


def _rope(x, cos, sin_up, sin_dn, half):
    up = pltpu.roll(x, LANES - half, 1)
    dn = pltpu.roll(x, half, 1)
    return x * cos + up * sin_up + dn * sin_dn


def _proj_kernel(x_ref, shift_ref, scale_ref, ng_ref, qg_ref, kg_ref,
                 rc_ref, rs1_ref, rs2_ref, ic_ref, is1_ref, is2_ref,
                 wa_ref, wb_ref, wc_ref, wpb_ref, cw_ref, cs_ref,
                 q_ref, k_ref, v_ref, kb_ref, vt_ref, sza_ref, qi_ref, ki_ref, ki2_ref, wi_ref,
                 sga_ref, gbb_ref, conv_ref, carry_ref, *, tm, conv_row):
    @pl.when(pl.program_id(1) == 0)
    def _():
        carry_ref[0:2, :] = cs_ref[0]

    x = x_ref[0]
    ms = jnp.mean(x * x, axis=-1, keepdims=True)
    xn = x * lax.rsqrt(ms + EPS) * ng_ref[...]
    h = xn * (1.0 + scale_ref[0]) + shift_ref[0]
    hb = _bf(h)

    def proj(c0, c1):
        for w_ref, base in ((wc_ref, C_KI), (wb_ref, C_U), (wa_ref, C_Q)):
            if c0 >= base:
                return _dot(hb, w_ref[:, c0 - base:c1 - base])

    rc, rs1, rs2 = rc_ref[...], rs1_ref[...], rs2_ref[...]
    ic, is1, is2 = ic_ref[...], is1_ref[...], is2_ref[...]
    rot_half = HEAD_DIM // ROT_FRAC // 2
    idx_half = IDX_DIM // ROT_FRAC // 2

    def head_norm_rope(xh, g):
        r = lax.rsqrt(jnp.mean(xh * xh, axis=-1, keepdims=True) + EPS)
        return _rope(xh * r * g, rc, rs1, rs2, rot_half)

    q = proj(C_Q, C_K)
    qg = qg_ref[...]
    for hh in range(N_HEADS):
        sl = slice(hh * HEAD_DIM, (hh + 1) * HEAD_DIM)
        q_ref[0, :, sl] = _bf(head_norm_rope(q[:, sl], qg) * Q_SCALE)

    k = proj(C_K, C_V)
    kg = kg_ref[...]
    v = proj(C_V, C_ZA)
    for hh in range(N_KV):
        sl = slice(hh * HEAD_DIM, (hh + 1) * HEAD_DIM)
        kh = head_norm_rope(k[:, sl], kg)
        k_ref[0, :, hh, :] = kh
        kb_ref[0, :, sl] = _bf(kh)
        v_ref[0, :, hh, :] = v[:, sl]
    vt_ref[0, 0] = _bf(v.T)

    sza_ref[0] = _silu(proj(C_ZA, C_QI))

    qi = proj(C_QI, C_U)
    for p in range(IDX_HEADS * IDX_DIM // LANES):
        sl = slice(p * LANES, (p + 1) * LANES)
        qi_ref[0, :, sl] = _bf(_rope(qi[:, sl], ic, is1, is2, idx_half) * (IDX_DIM ** -0.5))

    ki2 = _rope(proj(C_KI, C_WI), ic, is1, is2, idx_half)
    ki2_ref[0] = ki2
    ki_ref[0] = ki2[:, :IDX_DIM]
    wi_ref[0] = proj(C_WI, C_END) * (IDX_HEADS ** -0.5)

    cv = proj(C_CG, C_ZB) * proj(C_U, C_BG)
    c0 = carry_ref[0:1, :]
    c1 = carry_ref[1:2, :]
    row = lax.broadcasted_iota(jnp.int32, (tm, 1), 0)
    r1 = jnp.where(row == 0, c1, pltpu.roll(cv, 1, 0))
    r2 = jnp.where(row == 0, c0, jnp.where(row == 1, c1, pltpu.roll(cv, 2, 0)))
    cw = cw_ref[...]
    y_conv = cw[0:1, :] * r2 + cw[1:2, :] * r1 + cw[2:3, :] * cv
    carry_ref[0:2, :] = cv[tm - 2:tm, :]
    conv_ref[0] = cv[conv_row - 2:conv_row, :]

    tb = proj(C_BG, C_CG) * y_conv * _silu(proj(C_ZB, C_GA))
    b_out = _dot(_bf(tb), wpb_ref[...])
    gbb_ref[0] = jax.nn.sigmoid(proj(C_GB, C_KI)) * b_out
    sga_ref[0] = jax.nn.sigmoid(proj(C_GA, C_GB))


def _proj_call(x, shift, scale, norm_g, qg, kg, rope_tabs, w_r, w_pb, conv_w, conv_state,
               *, tm, conv_row):
    b, t, d = x.shape
    nt = t // tm
    tok = lambda w: pl.BlockSpec((1, tm, w), lambda i, j: (i, j, 0))
    per_b = lambda r, w: pl.BlockSpec((1, r, w), lambda i, j: (i, 0, 0))
    const = lambda r, w: pl.BlockSpec((r, w), lambda i, j: (0, 0))
    tab = pl.BlockSpec((tm, LANES), lambda i, j: (j, 0))
    kv4 = pl.BlockSpec((1, tm, N_KV, HEAD_DIM), lambda i, j: (i, j, 0, 0))
    whole = pl.BlockSpec(memory_space=pltpu.VMEM)
    f32, bf16 = jnp.float32, jnp.bfloat16
    out_shape = (
        jax.ShapeDtypeStruct((b, t, ATT_W), bf16),
        jax.ShapeDtypeStruct((b, t, N_KV, HEAD_DIM), f32),
        jax.ShapeDtypeStruct((b, t, N_KV, HEAD_DIM), f32),
        jax.ShapeDtypeStruct((b, t, KV_W), bf16),
        jax.ShapeDtypeStruct((b, nt, KV_W, tm), bf16),
        jax.ShapeDtypeStruct((b, t, ATT_W), f32),
        jax.ShapeDtypeStruct((b, t, IDX_HEADS * IDX_DIM), bf16),
        jax.ShapeDtypeStruct((b, t, IDX_DIM), f32),
        jax.ShapeDtypeStruct((b, t, LANES), f32),
        jax.ShapeDtypeStruct((b, t, LANES), f32),
        jax.ShapeDtypeStruct((b, t, D_MODEL), f32),
        jax.ShapeDtypeStruct((b, t, D_MODEL), f32),
        jax.ShapeDtypeStruct((b, CONV_W - 1, D_MODEL), f32),
    )
    out_specs = (
        tok(ATT_W), kv4, kv4, tok(KV_W),
        pl.BlockSpec((1, 1, KV_W, tm), lambda i, j: (i, j, 0, 0)),
        tok(ATT_W), tok(IDX_HEADS * IDX_DIM), tok(IDX_DIM), tok(LANES), tok(LANES),
        tok(D_MODEL), tok(D_MODEL), per_b(CONV_W - 1, D_MODEL),
    )
    in_specs = [
        tok(d), per_b(1, d), per_b(1, d), const(1, d), const(1, LANES), const(1, LANES),
        tab, tab, tab, tab, tab, tab,
        whole, whole, whole, whole, const(CONV_W, d), per_b(CONV_W - 1, d),
    ]
    return pl.pallas_call(
        functools.partial(_proj_kernel, tm=tm, conv_row=conv_row),
        grid=(b, nt),
        in_specs=in_specs,
        out_specs=out_specs,
        out_shape=out_shape,
        scratch_shapes=[pltpu.VMEM((SUBLANES, d), jnp.float32)],
        compiler_params=pltpu.CompilerParams(
            dimension_semantics=("arbitrary", "arbitrary"), vmem_limit_bytes=VMEM_LIMIT),
        name="in_proj",
    )(x, shift, scale, norm_g, qg, kg, *rope_tabs, *w_r, w_pb, conv_w, conv_state)


def _key_tiles(pos0, j, *, l_real, kt_size, qb, minimum=jnp.minimum):
    last_q = pos0 + j * qb + qb - 1
    kmax = minimum(l_real, ((last_q >> CHUNK_SHIFT) + 1) * CHUNK)
    return (kmax + kt_size - 1) // kt_size


def _fold_rows(x, op):
    rows, w = x.shape
    part = op(x.reshape(FOLD_CHAINS, rows // (FOLD_CHAINS * SUBLANES), SUBLANES, w), axis=1)
    return op(part, axis=0)


def _unrolled(n, body, init):
    carry = init
    for i in range(n):
        carry = body(i, carry)
    return carry


def _attn_kernel(*refs, pos0, l_real, kt_size, topk, qb, nkt_values):
    nkt = _key_tiles(pos0, pl.program_id(1), l_real=l_real, kt_size=kt_size, qb=qb)
    for c in nkt_values:
        run = functools.partial(_attn_block, *refs, pos0=pos0, l_real=l_real, kt_size=kt_size,
                                topk=topk, qb=qb, nkt=c)
        if len(nkt_values) == 1:
            run()
        else:
            pl.when(nkt == c)(run)


def _attn_block(q_ref, qi_ref, wi_ref, k_ref, vt_ref, ki2_ref,
                sza_ref, sga_ref, gbb_ref, x_ref, gate_ref, wpa_ref, wout_ref,
                y_ref, sc_ref, bias_ref, s_ref, acc_ref,
                *, pos0, l_real, kt_size, topk, qb, nkt):
    KT = kt_size
    QB = qb
    grp = N_HEADS // N_KV
    vt_w = vt_ref.shape[3]
    vt_per_kt = KT // vt_w
    q0 = pos0 + pl.program_id(1) * QB

    lane = lax.broadcasted_iota(jnp.int32, (1, QB), 1)
    q_chunk = (q0 + lane) >> CHUNK_SHIFT
    w_t = wi_ref[0].T
    qi = qi_ref[0]
    lane_kt = lax.broadcasted_iota(jnp.int32, (KT, LANES), 1)
    sub_kt = lax.broadcasted_iota(jnp.int32, (KT, 1), 0)

    def score_body(kt, carry):
        smax8, smin8, sum8, sq8 = carry
        k0 = kt * KT
        ki2 = ki2_ref[0, pl.ds(k0, KT), :]
        ki_lo = _bf(jnp.where(lane_kt < IDX_DIM, ki2, 0.0))
        ki_hi = _bf(jnp.where(lane_kt >= IDX_DIM, ki2, 0.0))
        acc = jnp.zeros((KT, QB), jnp.float32)
        for p in range(IDX_HEADS // 2):
            slab = qi[:, p * LANES:(p + 1) * LANES]
            acc = acc + w_t[2 * p:2 * p + 1, :] * jnp.maximum(_dot_nt(ki_lo, slab), 0.0)
            acc = acc + w_t[2 * p + 1:2 * p + 2, :] * jnp.maximum(_dot_nt(ki_hi, slab), 0.0)
        kpos = k0 + sub_kt
        adm = ((kpos >> CHUNK_SHIFT) <= q_chunk) & (kpos < l_real)
        sc_ref[pl.ds(k0, KT), :] = jnp.where(adm, acc, -jnp.inf)
        sub = acc.reshape(KT // (STAT_STRIDE * SUBLANES), STAT_STRIDE, SUBLANES, QB)[:, 0]
        return (jnp.maximum(smax8, _fold_rows(acc, jnp.max)),
                jnp.minimum(smin8, _fold_rows(acc, jnp.min)),
                sum8 + jnp.sum(sub, axis=0), sq8 + jnp.sum(sub * sub, axis=0))

    smax8, smin8, sum8, sq8 = _unrolled(
        nkt, score_body,
        (jnp.full((SUBLANES, QB), -jnp.inf, jnp.float32),
         jnp.full((SUBLANES, QB), jnp.inf, jnp.float32),
         jnp.zeros((SUBLANES, QB), jnp.float32), jnp.zeros((SUBLANES, QB), jnp.float32)))

    def count_ge(cand):
        def body(kt, acc):
            k0 = kt * KT
            m = (sc_ref[pl.ds(k0, KT), :] >= cand).astype(jnp.int32)
            return acc + _fold_rows(m, jnp.sum)
        acc = _unrolled(nkt, body, jnp.zeros((SUBLANES, QB), jnp.int32))
        return jnp.sum(acc, axis=0, keepdims=True)

    def inside(x, lo, hi):
        return (x > lo) & (x < hi)

    def active_rows(lo, hi, clo):
        return (clo > topk) & inside(lo * 0.5 + hi * 0.5, lo, hi)

    def any_active(st):
        lo, hi, clo, _ = st
        return jnp.max(active_rows(lo, hi, clo).astype(jnp.float32)) > 0.0

    def search_step(it, st):
        lo, hi, clo, chi = st
        act = active_rows(lo, hi, clo)
        mid = lo * 0.5 + hi * 0.5
        near_zero = jnp.where(lo >= 0.0, TINY, -TINY)
        geo = jnp.sqrt(jnp.abs(lo)) * jnp.sqrt(jnp.abs(hi)) * jnp.where(lo >= 0.0, 1.0, -1.0)
        guess = jnp.where(it == 0, q_below, jnp.where(it == 1, q_above,
                          jnp.where(it == 2, 0.0, jnp.where(it == 3, near_zero,
                          jnp.where(it < SEARCH_VALUE_IT, mid, geo)))))
        cand = jnp.where(inside(guess, lo, hi), guess, mid)
        cnt = count_ge(cand)
        ge = cnt >= topk
        up_lo = act & ge
        up_hi = act & jnp.logical_not(ge)
        return (jnp.where(up_lo, cand, lo), jnp.where(up_hi, cand, hi),
                jnp.where(up_lo, cnt, clo), jnp.where(up_hi, cnt, chi))

    n_adm = jnp.minimum(l_real, (q_chunk + 1) * CHUNK)
    lo0 = jnp.min(smin8, axis=0, keepdims=True)
    smax = jnp.max(smax8, axis=0, keepdims=True)
    hi0 = smax + jnp.abs(smax) * 2.0 ** -10 + 1e-30
    n_stat = nkt * KT // STAT_STRIDE
    mean = jnp.sum(sum8, axis=0, keepdims=True) * (1.0 / n_stat)
    std = jnp.sqrt(jnp.maximum(jnp.sum(sq8, axis=0, keepdims=True) * (1.0 / n_stat) - mean * mean, 0.0))
    z_q = jnp.log(jnp.maximum(n_adm - topk, 1).astype(jnp.float32) * (1.0 / topk)) * (1.0 / 1.7)
    q_below = mean + (z_q - STAT_MARGIN) * std
    q_above = mean + (z_q + STAT_MARGIN) * std
    st = (lo0, hi0, n_adm, jnp.zeros((1, QB), jnp.int32))
    most_adm = jnp.minimum(l_real, (((q0 + QB - 1) >> CHUNK_SHIFT) + 1) * CHUNK)
    n_first = jnp.where(most_adm > topk, SEARCH_FIRST_IT, 0)
    st = lax.fori_loop(0, n_first, search_step, st)

    def search_cond(c):
        return (c[0] < SEARCH_MAX_IT) & any_active(c[1])

    def search_body(c):
        it, st = c
        return it + 2, search_step(it + 1, search_step(it, st))

    _, (thr, _, n_ge, n_gt) = lax.while_loop(search_cond, search_body, (n_first, st))
    need = topk - n_gt
    any_tie = jnp.max((n_ge > topk).astype(jnp.float32)) > 0.0

    def bias_body(kt, carry):
        k0 = kt * KT
        bias_ref[pl.ds(k0, KT), :] = jnp.where(sc_ref[pl.ds(k0, KT), :] >= thr, 0.0, NEG_BIG)
        return carry

    _unrolled(nkt, bias_body, 0)

    @pl.when(any_tie)
    def _():
        tri = _bf((lax.broadcasted_iota(jnp.int32, (LANES, LANES), 1)
                   <= lax.broadcasted_iota(jnp.int32, (LANES, LANES), 0)).astype(jnp.float32))
        need_f = need.astype(jnp.float32)
        untied = n_ge <= topk

        def tie_body(kt, seen):
            for r in range(KT // LANES):
                k0 = kt * KT + r * LANES
                sc = sc_ref[pl.ds(k0, LANES), :]
                eq = sc == thr
                eq_f = eq.astype(jnp.float32)
                rank = _dot(tri, _bf(eq_f)) + seen
                sel = (sc > thr) | (eq & ((rank <= need_f) | untied))
                bias_ref[pl.ds(k0, LANES), :] = jnp.where(sel, 0.0, NEG_BIG)
                seen = seen + jnp.sum(eq_f, axis=0, keepdims=True)
            return seen

        _unrolled(nkt, tie_body, jnp.zeros((1, QB), jnp.float32))

    gw = grp * QB
    m8_init = tuple(jnp.full((SUBLANES, QB), NEG_BIG, jnp.float32) for _ in range(grp))
    l8_init = jnp.zeros((SUBLANES, gw), jnp.float32)
    q_groups = [jnp.concatenate(
        [q_ref[0, :, (g * grp + hh) * HEAD_DIM:(g * grp + hh + 1) * HEAD_DIM]
         for hh in range(grp)], axis=0) for g in range(N_KV)]

    def qk_tile(g, kt, m8):
        k0 = kt * KT
        kb = k_ref[0, pl.ds(k0, KT), g * HEAD_DIM:(g + 1) * HEAD_DIM]
        bias = bias_ref[pl.ds(k0, KT), :]
        s = _dot_nt(kb, q_groups[g])
        out = []
        for hh in range(grp):
            sh = s[:, hh * QB:(hh + 1) * QB] + bias
            s_ref[pl.ds(k0, KT), g * gw + hh * QB:g * gw + (hh + 1) * QB] = sh
            out.append(jnp.maximum(
                m8[hh], _fold_rows(sh, jnp.max)))
        return tuple(out)

    def pv_tile(g, kt, l8, m_all):
        k0 = kt * KT
        p = jnp.exp2(s_ref[pl.ds(k0, KT), g * gw:(g + 1) * gw] - m_all)
        pb = _bf(p)
        acc_ref[...] += sum(
            _dot(vt_ref[0, kt * vt_per_kt + r, g * HEAD_DIM:(g + 1) * HEAD_DIM, :],
                 pb[r * vt_w:(r + 1) * vt_w, :])
            for r in range(vt_per_kt))
        return l8 + _fold_rows(p, jnp.sum)

    def col_max(m8):
        return jnp.concatenate([jnp.max(m, axis=0, keepdims=True) for m in m8], axis=1)

    def finish_group(g, l8):
        o_t = acc_ref[...] * (1.0 / jnp.sum(l8, axis=0, keepdims=True))
        cols = slice(g * grp * HEAD_DIM, (g + 1) * grp * HEAD_DIM)
        o_g = jnp.concatenate([o_t[:, hh * QB:(hh + 1) * QB].T for hh in range(grp)], axis=1)
        return _dot(_bf(o_g * sza_ref[0, :, cols]), wpa_ref[cols, :])

    m8 = _unrolled(nkt, functools.partial(qk_tile, 0), m8_init)
    for g in range(N_KV):
        m_all = col_max(m8)
        acc_ref[...] = jnp.zeros((HEAD_DIM, gw), jnp.float32)
        if g + 1 < N_KV:
            def both(kt, carry, g=g, m_all=m_all):
                return pv_tile(g, kt, carry[0], m_all), qk_tile(g + 1, kt, carry[1])
            l8, m8 = _unrolled(nkt, both, (l8_init, m8_init))
        else:
            l8 = _unrolled(nkt, lambda kt, l8, g=g, m_all=m_all: pv_tile(g, kt, l8, m_all),
                           l8_init)
        part = finish_group(g, l8)
        a_out = part if g == 0 else a_out + part

    merged = sga_ref[0] * a_out + gbb_ref[0]
    y_ref[0] = x_ref[0] + gate_ref[0] * _dot(_bf(merged), wout_ref[...])


def _attn_call(q, qi, wi, k_all, vt_all, ki2_all, sza, sga, gbb, x, gate, w_pa, w_out,
               *, pos0, l_real, kt_size, topk, qb):
    b, t, d = x.shape
    lp = k_all.shape[1]
    nkt_values = sorted({_key_tiles(pos0, jj, l_real=l_real, kt_size=kt_size, qb=qb, minimum=min)
                         for jj in range(t // qb)})
    tok = lambda w: pl.BlockSpec((1, qb, w), lambda i, j: (i, j, 0))
    per_b = lambda r, w: pl.BlockSpec((1, r, w), lambda i, j: (i, 0, 0))
    whole = pl.BlockSpec(memory_space=pltpu.VMEM)
    in_specs = [
        tok(ATT_W), tok(IDX_HEADS * IDX_DIM), tok(LANES),
        per_b(lp, KV_W),
        pl.BlockSpec((1,) + vt_all.shape[1:], lambda i, j: (i, 0, 0, 0)),
        per_b(lp, LANES),
        tok(ATT_W), tok(D_MODEL), tok(D_MODEL), tok(d), per_b(1, d),
        whole, whole,
    ]
    return pl.pallas_call(
        functools.partial(_attn_kernel, pos0=pos0, l_real=l_real, kt_size=kt_size, topk=topk,
                          qb=qb, nkt_values=nkt_values),
        grid=(b, t // qb),
        in_specs=in_specs,
        out_specs=tok(d),
        out_shape=jax.ShapeDtypeStruct((b, t, d), jnp.float32),
        scratch_shapes=[
            pltpu.VMEM((lp, qb), jnp.float32),
            pltpu.VMEM((lp, qb), jnp.float32),
            pltpu.VMEM((lp, N_HEADS * qb), jnp.float32),
            pltpu.VMEM((HEAD_DIM, N_HEADS // N_KV * qb), jnp.float32),
        ],
        compiler_params=pltpu.CompilerParams(
            dimension_semantics=("arbitrary", "arbitrary"), vmem_limit_bytes=VMEM_LIMIT),
        name="dsa_attn",
    )(q, qi, wi, k_all, vt_all, ki2_all, sza, sga, gbb, x, gate, w_pa, w_out)


def _rope_tables(pos, width, period):
    rot = period // ROT_FRAC
    half = rot // 2
    inv = ROPE_THETA ** (-2.0 * jnp.arange(half, dtype=jnp.float32) / rot)
    ang = pos.astype(jnp.float32)[:, None] * inv[None, :]
    cos, sin = jnp.cos(ang), jnp.sin(ang)
    n = pos.shape[0]
    rest = period - rot
    one = jnp.ones((n, rest), jnp.float32)
    zero_h = jnp.zeros((n, half), jnp.float32)
    zero_r = jnp.zeros((n, rest), jnp.float32)
    c = jnp.concatenate([cos, cos, one], axis=1)
    s_up = jnp.concatenate([-sin, zero_h, zero_r], axis=1)
    s_dn = jnp.concatenate([zero_h, sin, zero_r], axis=1)
    rep = width // period
    return tuple(jnp.tile(a, (1, rep)) for a in (c, s_up, s_dn))


def _prep_w_in(w_in):
    o_ki = ATT_W + 2 * KV_W + ATT_W + IDX_HEADS * IDX_DIM
    o_wi = o_ki + IDX_DIM
    o_u = o_wi + IDX_HEADS
    ki = w_in[:, o_ki:o_wi]
    pad = jnp.zeros((w_in.shape[0], C_END - C_WI - IDX_HEADS), w_in.dtype)
    w_c = jnp.concatenate([ki, ki, w_in[:, o_wi:o_u], pad], axis=1)
    return _bf(w_in[:, :o_ki]), _bf(w_in[:, o_u:]), _bf(w_c)


def _layer(x, shift, scale, gate, pos, conv_state, past, weights, *, tm, t_real, kt_size, qb):
    norm_g, qg, kg, w_r, w_pb, conv_w, w_pa, w_out = weights
    b, t, _ = x.shape
    tabs = _rope_tables(pos, LANES, HEAD_DIM) + _rope_tables(pos, LANES, IDX_DIM)
    conv_row = t_real - (t // tm - 1) * tm
    (q, k, v, kb, vt, sza, qi, ki, ki2, wi, sga, gbb, conv_new) = _proj_call(
        x, shift, scale, norm_g, qg, kg, tabs, w_r, w_pb, conv_w, conv_state,
        tm=tm, conv_row=conv_row)
    if past is None:
        p_len = 0
        k_all, vt_all, ki2_all = kb, vt, ki2
    else:
        past_k, past_v, past_ki = past
        p_len = past_k.shape[1]
        lp = -(-(p_len + t) // kt_size) * kt_size
        extra = lp - p_len - t
        k_all = jnp.concatenate(
            [_bf(past_k), kb, jnp.zeros((b, extra, KV_W), jnp.bfloat16)], axis=1)
        vt_all = jnp.concatenate(
            [_bf(jnp.swapaxes(past_v, 1, 2)), vt[:, 0], jnp.zeros((b, KV_W, extra), jnp.bfloat16)],
            axis=2)
        vt_all = jnp.swapaxes(vt_all.reshape(b, KV_W, lp // kt_size, kt_size), 1, 2)
        ki2_all = jnp.concatenate(
            [jnp.concatenate([past_ki, past_ki], axis=-1), ki2,
             jnp.zeros((b, extra, LANES), jnp.float32)], axis=1)
    l_real = p_len + t_real
    topk = min(TOPK_MAX, l_real // 4)
    y = _attn_call(q, qi, wi, k_all, vt_all, ki2_all, sza, sga, gbb, x, gate, w_pa, w_out,
                   pos0=p_len, l_real=l_real, kt_size=kt_size, topk=topk, qb=qb)
    return y, k, v, ki, conv_new


def kernel(x_prompt, x_sample, cache_k, cache_v, cache_idx_k, state_conv, c_prompt, c_sample,
           w_ada, b_ada, norm_g, w_in, q_norm_g, k_norm_g, conv_w, w_pa, w_pb, w_out):
    bp, seq, d = x_prompt.shape
    bs, dec_seq, _ = x_sample.shape
    past_len = cache_k.shape[1]

    mod = _mod_call(jnp.concatenate([c_prompt, c_sample], axis=0), w_ada, b_ada)
    shift, scale, gate = (m.reshape(bp + bs, 1, d) for m in jnp.split(mod, 3, axis=-1))

    weights = (norm_g.reshape(1, d), q_norm_g.reshape(1, HEAD_DIM), k_norm_g.reshape(1, HEAD_DIM),
               _prep_w_in(w_in), _bf(w_pb), conv_w, _bf(w_pa), _bf(w_out))

    yp, k_p, v_p, ki_p, conv_p = _layer(
        x_prompt, shift[:bp], scale[:bp], gate[:bp], jnp.arange(seq, dtype=jnp.int32),
        jnp.zeros((bp, CONV_W - 1, d), x_prompt.dtype), None, weights,
        tm=256, t_real=seq, kt_size=512, qb=QB_PROMPT)

    x_pad = jnp.pad(x_sample, ((0, 0), (0, QB_SAMPLE - dec_seq), (0, 0)))
    past = (cache_k.reshape(bs, past_len, KV_W), cache_v.reshape(bs, past_len, KV_W), cache_idx_k)
    ys, k_s, v_s, ki_s, conv_s = _layer(
        x_pad, shift[bp:], scale[bp:], gate[bp:],
        past_len + jnp.arange(QB_SAMPLE, dtype=jnp.int32), state_conv, past, weights,
        tm=QB_SAMPLE, t_real=dec_seq, kt_size=384, qb=QB_SAMPLE)

    return (yp, ys[:, :dec_seq], k_p, v_p, ki_p, conv_p,
            k_s[:, :dec_seq], v_s[:, :dec_seq], ki_s[:, :dec_seq], conv_s)
```

```python
import functools
import math

import jax
import jax.numpy as jnp
from jax import lax
from jax.experimental import pallas as pl
from jax.experimental.pallas import tpu as pltpu

D_MODEL = 1024
CHUNK = 64
CHUNK_SHIFT = 6
N_HEADS = 8
N_KV = 2
HEAD_DIM = 128
ATT_W = N_HEADS * HEAD_DIM
KV_W = N_KV * HEAD_DIM
ROT_FRAC = 4
ROPE_THETA = 500000.0
IDX_HEADS = 8
IDX_DIM = 64
TOPK_MAX = 256
CONV_W = 3
EPS = 1e-6

LANES = 128
SUBLANES = 8
QB_PROMPT = 256
QB_SAMPLE = 128
Q_SCALE = HEAD_DIM ** -0.5 * math.log2(math.e)
VMEM_LIMIT = 56 * 1024 * 1024

C_Q, C_K, C_V, C_ZA, C_QI = 0, 1024, 1280, 1536, 2560
C_U, C_BG, C_CG, C_ZB, C_GA, C_GB = 3072, 4096, 5120, 6144, 7168, 8192
C_KI, C_WI, C_END = 9216, 9344, 9472

NEG_BIG = -1e30
TINY = 1.1754943508222875e-38
FOLD_CHAINS = 4
STAT_STRIDE = 4
STAT_MARGIN = 0.25
SEARCH_VALUE_IT = 18
SEARCH_FIRST_IT = 18
SEARCH_MAX_IT = SEARCH_VALUE_IT + 36

_NT = (((1,), (1,)), ((), ()))


def _bf(x):
    return x.astype(jnp.bfloat16)


def _dot(a, b):
    return jnp.dot(a, b, preferred_element_type=jnp.float32)


def _dot_nt(a, b):
    return lax.dot_general(a, b, _NT, preferred_element_type=jnp.float32)


def _silu(x):
    return x * jax.nn.sigmoid(x)


def _mod_kernel(c_ref, w_ref, b_ref, o_ref):
    c = c_ref[...]
    s = _silu(c)
    w = w_ref[...]
    s_hi = _bf(s)
    s_lo = _bf(s - s_hi.astype(jnp.float32))
    w_hi = _bf(w)
    w_lo = _bf(w - w_hi.astype(jnp.float32))
    acc = _dot(s_hi, w_hi) + (_dot(s_lo, w_hi) + _dot(s_hi, w_lo))
    o_ref[...] = acc + b_ref[...]


def _mod_call(c_all, w_ada, b_ada):
    nb, d = c_all.shape
    n = w_ada.shape[1]
    bn = 1024
    return pl.pallas_call(
        _mod_kernel,
        grid=(n // bn,),
        in_specs=[
            pl.BlockSpec((nb, d), lambda i: (0, 0)),
            pl.BlockSpec((d, bn), lambda i: (0, i)),
            pl.BlockSpec((1, bn), lambda i: (0, i)),
        ],
        out_specs=pl.BlockSpec((nb, bn), lambda i: (0, i)),
        out_shape=jax.ShapeDtypeStruct((nb, n), jnp.float32),
        compiler_params=pltpu.CompilerParams(dimension_semantics=("arbitrary",)),
        name="adaln_mod",
    )(c_all, w_ada, b_ada.reshape(1, n))


def _rope(x, cos, sin_up, sin_dn, half):
    up = pltpu.roll(x, LANES - half, 1)
    dn = pltpu.roll(x, half, 1)
    return x * cos + up * sin_up + dn * sin_dn


def _proj_kernel(x_ref, shift_ref, scale_ref, ng_ref, qg_ref, kg_ref,
                 rc_ref, rs1_ref, rs2_ref, ic_ref, is1_ref, is2_ref,
                 wa_ref, wb_ref, wc_ref, wpb_ref, cw_ref, cs_ref,
                 q_ref, k_ref, v_ref, kb_ref, vt_ref, sza_ref, qi_ref, ki_ref, ki2_ref, wi_ref,
                 sga_ref, gbb_ref, conv_ref, carry_ref, *, tm, conv_row):
    @pl.when(pl.program_id(1) == 0)
    def _():
        carry_ref[0:2, :] = cs_ref[0]

    x = x_ref[0]
    ms = jnp.mean(x * x, axis=-1, keepdims=True)
    gain = ng_ref[...] * (1.0 + scale_ref[0])
    h = x * lax.rsqrt(ms + EPS) * gain + shift_ref[0]
    hb = _bf(h)

    def proj(c0, c1):
        for w_ref, base in ((wc_ref, C_KI), (wb_ref, C_U), (wa_ref, C_Q)):
            if c0 >= base:
                return _dot(hb, w_ref[:, c0 - base:c1 - base])

    rc, rs1, rs2 = rc_ref[...], rs1_ref[...], rs2_ref[...]
    ic, is1, is2 = ic_ref[...], is1_ref[...], is2_ref[...]
    rot_half = HEAD_DIM // ROT_FRAC // 2
    idx_half = IDX_DIM // ROT_FRAC // 2

    def head_norm_rope(xh, g):
        r = lax.rsqrt(jnp.mean(xh * xh, axis=-1, keepdims=True) + EPS)
        return _rope(xh * r * g, rc, rs1, rs2, rot_half)

    q = proj(C_Q, C_K)
    qg = qg_ref[...]
    for hh in range(N_HEADS):
        sl = slice(hh * HEAD_DIM, (hh + 1) * HEAD_DIM)
        q_ref[0, :, sl] = _bf(head_norm_rope(q[:, sl], qg) * Q_SCALE)

    k = proj(C_K, C_V)
    kg = kg_ref[...]
    v = proj(C_V, C_ZA)
    for hh in range(N_KV):
        sl = slice(hh * HEAD_DIM, (hh + 1) * HEAD_DIM)
        kh = head_norm_rope(k[:, sl], kg)
        k_ref[0, :, hh, :] = kh
        kb_ref[0, :, sl] = _bf(kh)
        v_ref[0, :, hh, :] = v[:, sl]
    vt_ref[0, 0] = _bf(v.T)

    sza_ref[0] = _silu(proj(C_ZA, C_QI))

    qi = proj(C_QI, C_U)
    for p in range(IDX_HEADS * IDX_DIM // LANES):
        sl = slice(p * LANES, (p + 1) * LANES)
        qi_ref[0, :, sl] = _bf(_rope(qi[:, sl], ic, is1, is2, idx_half) * (IDX_DIM ** -0.5))

    ki2 = _rope(proj(C_KI, C_WI), ic, is1, is2, idx_half)
    ki2_ref[0] = ki2
    ki_ref[0] = ki2[:, :IDX_DIM]
    wi_ref[0] = proj(C_WI, C_END) * (IDX_HEADS ** -0.5)

    cv = proj(C_CG, C_ZB) * proj(C_U, C_BG)
    c0 = carry_ref[0:1, :]
    c1 = carry_ref[1:2, :]
    row = lax.broadcasted_iota(jnp.int32, (tm, 1), 0)
    r1 = jnp.where(row == 0, c1, pltpu.roll(cv, 1, 0))
    r2 = jnp.where(row == 0, c0, jnp.where(row == 1, c1, pltpu.roll(cv, 2, 0)))
    cw = cw_ref[...]
    y_conv = cw[0:1, :] * r2 + cw[1:2, :] * r1 + cw[2:3, :] * cv
    carry_ref[0:2, :] = cv[tm - 2:tm, :]
    conv_ref[0] = cv[conv_row - 2:conv_row, :]

    tb = proj(C_BG, C_CG) * y_conv * _silu(proj(C_ZB, C_GA))
    b_out = _dot(_bf(tb), wpb_ref[...])
    gbb_ref[0] = jax.nn.sigmoid(proj(C_GB, C_KI)) * b_out
    sga_ref[0] = jax.nn.sigmoid(proj(C_GA, C_GB))


def _proj_call(x, shift, scale, norm_g, qg, kg, rope_tabs, w_r, w_pb, conv_w, conv_state,
               *, tm, conv_row):
    b, t, d = x.shape
    nt = t // tm
    tok = lambda w: pl.BlockSpec((1, tm, w), lambda i, j: (i, j, 0))
    per_b = lambda r, w: pl.BlockSpec((1, r, w), lambda i, j: (i, 0, 0))
    const = lambda r, w: pl.BlockSpec((r, w), lambda i, j: (0, 0))
    tab = pl.BlockSpec((tm, LANES), lambda i, j: (j, 0))
    kv4 = pl.BlockSpec((1, tm, N_KV, HEAD_DIM), lambda i, j: (i, j, 0, 0))
    whole = pl.BlockSpec(memory_space=pltpu.VMEM)
    f32, bf16 = jnp.float32, jnp.bfloat16
    out_shape = (
        jax.ShapeDtypeStruct((b, t, ATT_W), bf16),
        jax.ShapeDtypeStruct((b, t, N_KV, HEAD_DIM), f32),
        jax.ShapeDtypeStruct((b, t, N_KV, HEAD_DIM), f32),
        jax.ShapeDtypeStruct((b, t, KV_W), bf16),
        jax.ShapeDtypeStruct((b, nt, KV_W, tm), bf16),
        jax.ShapeDtypeStruct((b, t, ATT_W), f32),
        jax.ShapeDtypeStruct((b, t, IDX_HEADS * IDX_DIM), bf16),
        jax.ShapeDtypeStruct((b, t, IDX_DIM), f32),
        jax.ShapeDtypeStruct((b, t, LANES), f32),
        jax.ShapeDtypeStruct((b, t, LANES), f32),
        jax.ShapeDtypeStruct((b, t, D_MODEL), f32),
        jax.ShapeDtypeStruct((b, t, D_MODEL), f32),
        jax.ShapeDtypeStruct((b, CONV_W - 1, D_MODEL), f32),
    )
    out_specs = (
        tok(ATT_W), kv4, kv4, tok(KV_W),
        pl.BlockSpec((1, 1, KV_W, tm), lambda i, j: (i, j, 0, 0)),
        tok(ATT_W), tok(IDX_HEADS * IDX_DIM), tok(IDX_DIM), tok(LANES), tok(LANES),
        tok(D_MODEL), tok(D_MODEL), per_b(CONV_W - 1, D_MODEL),
    )
    in_specs = [
        tok(d), per_b(1, d), per_b(1, d), const(1, d), const(1, LANES), const(1, LANES),
        tab, tab, tab, tab, tab, tab,
        whole, whole, whole, whole, const(CONV_W, d), per_b(CONV_W - 1, d),
    ]
    return pl.pallas_call(
        functools.partial(_proj_kernel, tm=tm, conv_row=conv_row),
        grid=(b, nt),
        in_specs=in_specs,
        out_specs=out_specs,
        out_shape=out_shape,
        scratch_shapes=[pltpu.VMEM((SUBLANES, d), jnp.float32)],
        compiler_params=pltpu.CompilerParams(
            dimension_semantics=("arbitrary", "arbitrary"), vmem_limit_bytes=VMEM_LIMIT),
        name="in_proj",
    )(x, shift, scale, norm_g, qg, kg, *rope_tabs, *w_r, w_pb, conv_w, conv_state)


def _key_tiles(pos0, j, *, l_real, kt_size, qb, minimum=jnp.minimum):
    last_q = pos0 + j * qb + qb - 1
    kmax = minimum(l_real, ((last_q >> CHUNK_SHIFT) + 1) * CHUNK)
    return (kmax + kt_size - 1) // kt_size


def _fold_rows(x, op):
    rows, w = x.shape
    part = op(x.reshape(FOLD_CHAINS, rows // (FOLD_CHAINS * SUBLANES), SUBLANES, w), axis=1)
    return op(part, axis=0)


def _unrolled(n, body, init):
    carry = init
    for i in range(n):
        carry = body(i, carry)
    return carry


def _attn_kernel(*refs, pos0, l_real, kt_size, topk, qb, nkt_values):
    nkt = _key_tiles(pos0, pl.program_id(1), l_real=l_real, kt_size=kt_size, qb=qb)
    for c in nkt_values:
        run = functools.partial(_attn_block, *refs, pos0=pos0, l_real=l_real, kt_size=kt_size,
                                topk=topk, qb=qb, nkt=c)
        if len(nkt_values) == 1:
            run()
        else:
            pl.when(nkt == c)(run)


def _attn_block(q_ref, qi_ref, wi_ref, k_ref, vt_ref, ki2_ref,
                sza_ref, sga_ref, gbb_ref, x_ref, gate_ref, wpa_ref, wout_ref,
                y_ref, sc_ref, bias_ref, s_ref, acc_ref, o_ref,
                *, pos0, l_real, kt_size, topk, qb, nkt):
    KT = kt_size
    QB = qb
    grp = N_HEADS // N_KV
    vt_w = vt_ref.shape[3]
    vt_per_kt = KT // vt_w
    q0 = pos0 + pl.program_id(1) * QB

    lane = lax.broadcasted_iota(jnp.int32, (1, QB), 1)
    q_chunk = (q0 + lane) >> CHUNK_SHIFT
    w_t = wi_ref[0].T
    qi = qi_ref[0]
    lane_kt = lax.broadcasted_iota(jnp.int32, (KT, LANES), 1)
    sub_kt = lax.broadcasted_iota(jnp.int32, (KT, 1), 0)

    def score_body(kt, carry):
        smax8, smin8, sum8, sq8 = carry
        k0 = kt * KT
        ki2 = ki2_ref[0, pl.ds(k0, KT), :]
        ki_lo = _bf(jnp.where(lane_kt < IDX_DIM, ki2, 0.0))
        ki_hi = _bf(jnp.where(lane_kt >= IDX_DIM, ki2, 0.0))
        acc = jnp.zeros((KT, QB), jnp.float32)
        for p in range(IDX_HEADS // 2):
            slab = qi[:, p * LANES:(p + 1) * LANES]
            acc = acc + w_t[2 * p:2 * p + 1, :] * jnp.maximum(_dot_nt(ki_lo, slab), 0.0)
            acc = acc + w_t[2 * p + 1:2 * p + 2, :] * jnp.maximum(_dot_nt(ki_hi, slab), 0.0)
        kpos = k0 + sub_kt
        adm = ((kpos >> CHUNK_SHIFT) <= q_chunk) & (kpos < l_real)
        sc_ref[pl.ds(k0, KT), :] = jnp.where(adm, acc, -jnp.inf)
        sub = acc.reshape(KT // (STAT_STRIDE * SUBLANES), STAT_STRIDE, SUBLANES, QB)[:, 0]
        return (jnp.maximum(smax8, _fold_rows(acc, jnp.max)),
                jnp.minimum(smin8, _fold_rows(acc, jnp.min)),
                sum8 + jnp.sum(sub, axis=0), sq8 + jnp.sum(sub * sub, axis=0))

    smax8, smin8, sum8, sq8 = _unrolled(
        nkt, score_body,
        (jnp.full((SUBLANES, QB), -jnp.inf, jnp.float32),
         jnp.full((SUBLANES, QB), jnp.inf, jnp.float32),
         jnp.zeros((SUBLANES, QB), jnp.float32), jnp.zeros((SUBLANES, QB), jnp.float32)))

    def count_ge(cand):
        def body(kt, acc):
            k0 = kt * KT
            m = (sc_ref[pl.ds(k0, KT), :] >= cand).astype(jnp.int32)
            return acc + _fold_rows(m, jnp.sum)
        acc = _unrolled(nkt, body, jnp.zeros((SUBLANES, QB), jnp.int32))
        return jnp.sum(acc, axis=0, keepdims=True)

    def inside(x, lo, hi):
        return (x > lo) & (x < hi)

    def active_rows(lo, hi, clo):
        return (clo > topk) & inside(lo * 0.5 + hi * 0.5, lo, hi)

    def any_active(st):
        lo, hi, clo, _ = st
        return jnp.max(active_rows(lo, hi, clo).astype(jnp.float32)) > 0.0

    def search_step(it, st):
        lo, hi, clo, chi = st
        act = active_rows(lo, hi, clo)
        mid = lo * 0.5 + hi * 0.5
        near_zero = jnp.where(lo >= 0.0, TINY, -TINY)
        geo = jnp.sqrt(jnp.abs(lo)) * jnp.sqrt(jnp.abs(hi)) * jnp.where(lo >= 0.0, 1.0, -1.0)
        guess = jnp.where(it == 0, q_below, jnp.where(it == 1, q_above,
                          jnp.where(it == 2, 0.0, jnp.where(it == 3, near_zero,
                          jnp.where(it < SEARCH_VALUE_IT, mid, geo)))))
        cand = jnp.where(inside(guess, lo, hi), guess, mid)
        cnt = count_ge(cand)
        ge = cnt >= topk
        up_lo = act & ge
        up_hi = act & jnp.logical_not(ge)
        return (jnp.where(up_lo, cand, lo), jnp.where(up_hi, cand, hi),
                jnp.where(up_lo, cnt, clo), jnp.where(up_hi, cnt, chi))

    n_adm = jnp.minimum(l_real, (q_chunk + 1) * CHUNK)
    lo0 = jnp.min(smin8, axis=0, keepdims=True)
    smax = jnp.max(smax8, axis=0, keepdims=True)
    hi0 = smax + jnp.abs(smax) * 2.0 ** -10 + 1e-30
    n_stat = nkt * KT // STAT_STRIDE
    mean = jnp.sum(sum8, axis=0, keepdims=True) * (1.0 / n_stat)
    std = jnp.sqrt(jnp.maximum(jnp.sum(sq8, axis=0, keepdims=True) * (1.0 / n_stat) - mean * mean, 0.0))
    z_q = jnp.log(jnp.maximum(n_adm - topk, 1).astype(jnp.float32) * (1.0 / topk)) * (1.0 / 1.7)
    q_below = mean + (z_q - STAT_MARGIN) * std
    q_above = mean + (z_q + STAT_MARGIN) * std
    st = (lo0, hi0, n_adm, jnp.zeros((1, QB), jnp.int32))
    n_first = jnp.where(any_active(st), SEARCH_FIRST_IT, 0)
    st = lax.fori_loop(0, n_first, search_step, st)

    def search_cond(c):
        return (c[0] < SEARCH_MAX_IT) & any_active(c[1])

    def search_body(c):
        it, st = c
        return it + 2, search_step(it + 1, search_step(it, st))

    _, (thr, _, n_ge, n_gt) = lax.while_loop(search_cond, search_body, (n_first, st))
    need = topk - n_gt
    any_tie = jnp.max((n_ge > topk).astype(jnp.float32)) > 0.0

    def bias_body(kt, carry):
        k0 = kt * KT
        bias_ref[pl.ds(k0, KT), :] = jnp.where(sc_ref[pl.ds(k0, KT), :] >= thr, 0.0, NEG_BIG)
        return carry

    _unrolled(nkt, bias_body, 0)

    @pl.when(any_tie)
    def _():
        tri = _bf((lax.broadcasted_iota(jnp.int32, (LANES, LANES), 1)
                   <= lax.broadcasted_iota(jnp.int32, (LANES, LANES), 0)).astype(jnp.float32))
        need_f = jnp.where(n_ge <= topk, jnp.inf, need.astype(jnp.float32))

        def tie_body(kt, seen):
            for r in range(KT // LANES):
                k0 = kt * KT + r * LANES
                sc = sc_ref[pl.ds(k0, LANES), :]
                eq = sc == thr
                rank = _dot(tri, jnp.where(eq, 1.0, 0.0).astype(jnp.bfloat16)) + seen
                sel = (sc > thr) | (eq & (rank <= need_f))
                bias_ref[pl.ds(k0, LANES), :] = jnp.where(sel, 0.0, NEG_BIG)
                seen = rank[LANES - 1:LANES, :]
            return seen

        _unrolled(nkt, tie_body, jnp.zeros((1, QB), jnp.float32))

    gw = grp * QB
    m8_init = tuple(jnp.full((SUBLANES, QB), NEG_BIG, jnp.float32) for _ in range(grp))
    l8_init = jnp.zeros((SUBLANES, gw), jnp.float32)
    q_groups = [jnp.concatenate(
        [q_ref[0, :, (g * grp + hh) * HEAD_DIM:(g * grp + hh + 1) * HEAD_DIM]
         for hh in range(grp)], axis=0) for g in range(N_KV)]

    def qk_tile(g, kt, m8):
        k0 = kt * KT
        kb = k_ref[0, pl.ds(k0, KT), g * HEAD_DIM:(g + 1) * HEAD_DIM]
        bias = bias_ref[pl.ds(k0, KT), :]
        s = _dot_nt(kb, q_groups[g])
        out = []
        for hh in range(grp):
            sh = s[:, hh * QB:(hh + 1) * QB] + bias
            s_ref[pl.ds(k0, KT), g * gw + hh * QB:g * gw + (hh + 1) * QB] = sh
            out.append(jnp.maximum(
                m8[hh], _fold_rows(sh, jnp.max)))
        return tuple(out)

    def pv_tile(g, kt, l8, m_all):
        k0 = kt * KT
        p = jnp.exp2(s_ref[pl.ds(k0, KT), g * gw:(g + 1) * gw] - m_all)
        pb = _bf(p)
        acc_ref[...] += sum(
            _dot(vt_ref[0, kt * vt_per_kt + r, g * HEAD_DIM:(g + 1) * HEAD_DIM, :],
                 pb[r * vt_w:(r + 1) * vt_w, :])
            for r in range(vt_per_kt))
        return l8 + _fold_rows(p, jnp.sum)

    def col_max(m8):
        return jnp.concatenate([jnp.max(m, axis=0, keepdims=True) for m in m8], axis=1)

    def finish_group(g, l8):
        o_t = acc_ref[...] * (1.0 / jnp.sum(l8, axis=0, keepdims=True))
        for hh in range(grp):
            h_abs = g * grp + hh
            o_ref[:, h_abs * HEAD_DIM:(h_abs + 1) * HEAD_DIM] = o_t[:, hh * QB:(hh + 1) * QB].T

    m8 = _unrolled(nkt, functools.partial(qk_tile, 0), m8_init)
    for g in range(N_KV):
        m_all = col_max(m8)
        acc_ref[...] = jnp.zeros((HEAD_DIM, gw), jnp.float32)
        if g + 1 < N_KV:
            def both(kt, carry, g=g, m_all=m_all):
                return pv_tile(g, kt, carry[0], m_all), qk_tile(g + 1, kt, carry[1])
            l8, m8 = _unrolled(nkt, both, (l8_init, m8_init))
        else:
            l8 = _unrolled(nkt, lambda kt, l8, g=g, m_all=m_all: pv_tile(g, kt, l8, m_all),
                           l8_init)
        finish_group(g, l8)

    a_out = _dot(_bf(o_ref[...] * sza_ref[0]), wpa_ref[...])
    merged = sga_ref[0] * a_out + gbb_ref[0]
    y_ref[0] = x_ref[0] + gate_ref[0] * _dot(_bf(merged), wout_ref[...])


def _attn_call(q, qi, wi, k_all, vt_all, ki2_all, sza, sga, gbb, x, gate, w_pa, w_out,
               *, pos0, l_real, kt_size, topk, qb):
    b, t, d = x.shape
    lp = k_all.shape[1]
    nkt_values = sorted({_key_tiles(pos0, jj, l_real=l_real, kt_size=kt_size, qb=qb, minimum=min)
                         for jj in range(t // qb)})
    tok = lambda w: pl.BlockSpec((1, qb, w), lambda i, j: (i, j, 0))
    per_b = lambda r, w: pl.BlockSpec((1, r, w), lambda i, j: (i, 0, 0))
    whole = pl.BlockSpec(memory_space=pltpu.VMEM)
    in_specs = [
        tok(ATT_W), tok(IDX_HEADS * IDX_DIM), tok(LANES),
        per_b(lp, KV_W),
        pl.BlockSpec((1,) + vt_all.shape[1:], lambda i, j: (i, 0, 0, 0)),
        per_b(lp, LANES),
        tok(ATT_W), tok(D_MODEL), tok(D_MODEL), tok(d), per_b(1, d),
        whole, whole,
    ]
    return pl.pallas_call(
        functools.partial(_attn_kernel, pos0=pos0, l_real=l_real, kt_size=kt_size, topk=topk,
                          qb=qb, nkt_values=nkt_values),
        grid=(b, t // qb),
        in_specs=in_specs,
        out_specs=tok(d),
        out_shape=jax.ShapeDtypeStruct((b, t, d), jnp.float32),
        scratch_shapes=[
            pltpu.VMEM((lp, qb), jnp.float32),
            pltpu.VMEM((lp, qb), jnp.float32),
            pltpu.VMEM((lp, N_HEADS * qb), jnp.float32),
            pltpu.VMEM((HEAD_DIM, N_HEADS // N_KV * qb), jnp.float32),
            pltpu.VMEM((qb, ATT_W), jnp.float32),
        ],
        compiler_params=pltpu.CompilerParams(
            dimension_semantics=("arbitrary", "arbitrary"), vmem_limit_bytes=VMEM_LIMIT),
        name="dsa_attn",
    )(q, qi, wi, k_all, vt_all, ki2_all, sza, sga, gbb, x, gate, w_pa, w_out)


def _rope_tables(pos, width, period):
    rot = period // ROT_FRAC
    half = rot // 2
    inv = ROPE_THETA ** (-2.0 * jnp.arange(half, dtype=jnp.float32) / rot)
    ang = pos.astype(jnp.float32)[:, None] * inv[None, :]
    cos, sin = jnp.cos(ang), jnp.sin(ang)
    n = pos.shape[0]
    rest = period - rot
    one = jnp.ones((n, rest), jnp.float32)
    zero_h = jnp.zeros((n, half), jnp.float32)
    zero_r = jnp.zeros((n, rest), jnp.float32)
    c = jnp.concatenate([cos, cos, one], axis=1)
    s_up = jnp.concatenate([-sin, zero_h, zero_r], axis=1)
    s_dn = jnp.concatenate([zero_h, sin, zero_r], axis=1)
    rep = width // period
    return tuple(jnp.tile(a, (1, rep)) for a in (c, s_up, s_dn))


def _prep_w_in(w_in):
    o_ki = ATT_W + 2 * KV_W + ATT_W + IDX_HEADS * IDX_DIM
    o_wi = o_ki + IDX_DIM
    o_u = o_wi + IDX_HEADS
    ki = w_in[:, o_ki:o_wi]
    pad = jnp.zeros((w_in.shape[0], C_END - C_WI - IDX_HEADS), w_in.dtype)
    w_c = jnp.concatenate([ki, ki, w_in[:, o_wi:o_u], pad], axis=1)
    return _bf(w_in[:, :o_ki]), _bf(w_in[:, o_u:]), _bf(w_c)


def _layer(x, shift, scale, gate, pos, conv_state, past, weights, *, tm, t_real, kt_size, qb):
    norm_g, qg, kg, w_r, w_pb, conv_w, w_pa, w_out = weights
    b, t, _ = x.shape
    tabs = _rope_tables(pos, LANES, HEAD_DIM) + _rope_tables(pos, LANES, IDX_DIM)
    conv_row = t_real - (t // tm - 1) * tm
    (q, k, v, kb, vt, sza, qi, ki, ki2, wi, sga, gbb, conv_new) = _proj_call(
        x, shift, scale, norm_g, qg, kg, tabs, w_r, w_pb, conv_w, conv_state,
        tm=tm, conv_row=conv_row)
    if past is None:
        p_len = 0
        k_all, vt_all, ki2_all = kb, vt, ki2
    else:
        past_k, past_v, past_ki = past
        p_len = past_k.shape[1]
        lp = -(-(p_len + t) // kt_size) * kt_size
        extra = lp - p_len - t
        k_all = jnp.concatenate(
            [_bf(past_k), kb, jnp.zeros((b, extra, KV_W), jnp.bfloat16)], axis=1)
        vt_all = jnp.concatenate(
            [_bf(jnp.swapaxes(past_v, 1, 2)), vt[:, 0], jnp.zeros((b, KV_W, extra), jnp.bfloat16)],
            axis=2)
        vt_all = jnp.swapaxes(vt_all.reshape(b, KV_W, lp // kt_size, kt_size), 1, 2)
        ki2_all = jnp.concatenate(
            [jnp.concatenate([past_ki, past_ki], axis=-1), ki2,
             jnp.zeros((b, extra, LANES), jnp.float32)], axis=1)
    l_real = p_len + t_real
    topk = min(TOPK_MAX, l_real // 4)
    y = _attn_call(q, qi, wi, k_all, vt_all, ki2_all, sza, sga, gbb, x, gate, w_pa, w_out,
                   pos0=p_len, l_real=l_real, kt_size=kt_size, topk=topk, qb=qb)
    return y, k, v, ki, conv_new


def kernel(x_prompt, x_sample, cache_k, cache_v, cache_idx_k, state_conv, c_prompt, c_sample,
           w_ada, b_ada, norm_g, w_in, q_norm_g, k_norm_g, conv_w, w_pa, w_pb, w_out):
    bp, seq, d = x_prompt.shape
    bs, dec_seq, _ = x_sample.shape
    past_len = cache_k.shape[1]

    mod = _mod_call(jnp.concatenate([c_prompt, c_sample], axis=0), w_ada, b_ada)
    shift, scale, gate = (m.reshape(bp + bs, 1, d) for m in jnp.split(mod, 3, axis=-1))

    weights = (norm_g.reshape(1, d), q_norm_g.reshape(1, HEAD_DIM), k_norm_g.reshape(1, HEAD_DIM),
               _prep_w_in(w_in), _bf(w_pb), conv_w, _bf(w_pa), _bf(w_out))

    yp, k_p, v_p, ki_p, conv_p = _layer(
        x_prompt, shift[:bp], scale[:bp], gate[:bp], jnp.arange(seq, dtype=jnp.int32),
        jnp.zeros((bp, CONV_W - 1, d), x_prompt.dtype), None, weights,
        tm=256, t_real=seq, kt_size=512, qb=QB_PROMPT)

    x_pad = jnp.pad(x_sample, ((0, 0), (0, QB_SAMPLE - dec_seq), (0, 0)))
    past = (cache_k.reshape(bs, past_len, KV_W), cache_v.reshape(bs, past_len, KV_W), cache_idx_k)
    ys, k_s, v_s, ki_s, conv_s = _layer(
        x_pad, shift[bp:], scale[bp:], gate[bp:],
        past_len + jnp.arange(QB_SAMPLE, dtype=jnp.int32), state_conv, past, weights,
        tm=QB_SAMPLE, t_real=dec_seq, kt_size=384, qb=QB_SAMPLE)

    return (yp, ys[:, :dec_seq], k_p, v_p, ki_p, conv_p,
            k_s[:, :dec_seq], v_s[:, :dec_seq], ki_s[:, :dec_seq], conv_s)
```

```python
import functools
import math

import jax
import jax.numpy as jnp
from jax import lax
from jax.experimental import pallas as pl
from jax.experimental.pallas import tpu as pltpu

D_MODEL = 1024
CHUNK = 64
CHUNK_SHIFT = 6
N_HEADS = 8
N_KV = 2
HEAD_DIM = 128
ATT_W = N_HEADS * HEAD_DIM
KV_W = N_KV * HEAD_DIM
ROT_FRAC = 4
ROPE_THETA = 500000.0
IDX_HEADS = 8
IDX_DIM = 64
TOPK_MAX = 256
CONV_W = 3
EPS = 1e-6

LANES = 128
SUBLANES = 8
QB_PROMPT = 256
QB_SAMPLE = 128
Q_SCALE = HEAD_DIM ** -0.5 * math.log2(math.e)
VMEM_LIMIT = 56 * 1024 * 1024

C_Q, C_K, C_V, C_ZA, C_QI = 0, 1024, 1280, 1536, 2560
C_U, C_BG, C_CG, C_ZB, C_GA, C_GB = 3072, 4096, 5120, 6144, 7168, 8192
C_KI, C_WI, C_END = 9216, 9344, 9472

NEG_BIG = -1e30
TINY = 1.1754943508222875e-38
FOLD_CHAINS = 8
STAT_STRIDE = 4
STAT_MARGIN = 0.25
SEARCH_VALUE_IT = 18
SEARCH_FIRST_IT = 18
SEARCH_MAX_IT = SEARCH_VALUE_IT + 36

_NT = (((1,), (1,)), ((), ()))


def _bf(x):
    return x.astype(jnp.bfloat16)


def _dot(a, b):
    return jnp.dot(a, b, preferred_element_type=jnp.float32)


def _dot_nt(a, b):
    return lax.dot_general(a, b, _NT, preferred_element_type=jnp.float32)


def _silu(x):
    return x * jax.nn.sigmoid(x)


def _mod_kernel(c_ref, w_ref, b_ref, o_ref):
    c = c_ref[...]
    s = _silu(c)
    w = w_ref[...]
    s_hi = _bf(s)
    s_lo = _bf(s - s_hi.astype(jnp.float32))
    w_hi = _bf(w)
    w_lo = _bf(w - w_hi.astype(jnp.float32))
    acc = _dot(s_hi, w_hi) + (_dot(s_lo, w_hi) + _dot(s_hi, w_lo))
    o_ref[...] = acc + b_ref[...]


def _mod_call(c_all, w_ada, b_ada):
    nb, d = c_all.shape
    n = w_ada.shape[1]
    bn = 1024
    return pl.pallas_call(
        _mod_kernel,
        grid=(n // bn,),
        in_specs=[
            pl.BlockSpec((nb, d), lambda i: (0, 0)),
            pl.BlockSpec((d, bn), lambda i: (0, i)),
            pl.BlockSpec((1, bn), lambda i: (0, i)),
        ],
        out_specs=pl.BlockSpec((nb, bn), lambda i: (0, i)),
        out_shape=jax.ShapeDtypeStruct((nb, n), jnp.float32),
        compiler_params=pltpu.CompilerParams(dimension_semantics=("arbitrary",)),
        name="adaln_mod",
    )(c_all, w_ada, b_ada.reshape(1, n))


def _rope(x, cos, sin_up, sin_dn, half):
    up = pltpu.roll(x, LANES - half, 1)
    dn = pltpu.roll(x, half, 1)
    return x * cos + up * sin_up + dn * sin_dn


def _proj_kernel(x_ref, shift_ref, scale_ref, ng_ref, qg_ref, kg_ref,
                 rc_ref, rs1_ref, rs2_ref, ic_ref, is1_ref, is2_ref,
                 wa_ref, wb_ref, wc_ref, wpb_ref, cw_ref, cs_ref,
                 q_ref, k_ref, v_ref, kb_ref, vt_ref, sza_ref, qi_ref, ki_ref, ki2_ref, wi_ref,
                 sga_ref, gbb_ref, conv_ref, carry_ref, *, tm, conv_row):
    @pl.when(pl.program_id(1) == 0)
    def _():
        carry_ref[0:2, :] = cs_ref[0]

    x = x_ref[0]
    ms = jnp.mean(x * x, axis=-1, keepdims=True)
    gain = ng_ref[...] * (1.0 + scale_ref[0])
    h = x * lax.rsqrt(ms + EPS) * gain + shift_ref[0]
    hb = _bf(h)

    def proj(c0, c1):
        for w_ref, base in ((wc_ref, C_KI), (wb_ref, C_U), (wa_ref, C_Q)):
            if c0 >= base:
                return _dot(hb, w_ref[:, c0 - base:c1 - base])

    rc, rs1, rs2 = rc_ref[...], rs1_ref[...], rs2_ref[...]
    ic, is1, is2 = ic_ref[...], is1_ref[...], is2_ref[...]
    rot_half = HEAD_DIM // ROT_FRAC // 2
    idx_half = IDX_DIM // ROT_FRAC // 2

    def head_norm_rope(xh, g):
        r = lax.rsqrt(jnp.mean(xh * xh, axis=-1, keepdims=True) + EPS)
        return _rope(xh * r * g, rc, rs1, rs2, rot_half)

    q = proj(C_Q, C_K)
    qg = qg_ref[...]
    for hh in range(N_HEADS):
        sl = slice(hh * HEAD_DIM, (hh + 1) * HEAD_DIM)
        q_ref[0, :, sl] = _bf(head_norm_rope(q[:, sl], qg) * Q_SCALE)

    k = proj(C_K, C_V)
    kg = kg_ref[...]
    v = proj(C_V, C_ZA)
    for hh in range(N_KV):
        sl = slice(hh * HEAD_DIM, (hh + 1) * HEAD_DIM)
        kh = head_norm_rope(k[:, sl], kg)
        k_ref[0, :, hh, :] = kh
        kb_ref[0, :, sl] = _bf(kh)
        v_ref[0, :, hh, :] = v[:, sl]
    vt_ref[0, 0] = _bf(v.T)

    sza_ref[0] = _silu(proj(C_ZA, C_QI))

    qi = proj(C_QI, C_U)
    for p in range(IDX_HEADS * IDX_DIM // LANES):
        sl = slice(p * LANES, (p + 1) * LANES)
        qi_ref[0, :, sl] = _bf(_rope(qi[:, sl], ic, is1, is2, idx_half) * (IDX_DIM ** -0.5))

    ki2 = _rope(proj(C_KI, C_WI), ic, is1, is2, idx_half)
    ki2_ref[0] = ki2
    ki_ref[0] = ki2[:, :IDX_DIM]
    wi_ref[0] = proj(C_WI, C_END) * (IDX_HEADS ** -0.5)

    cv = proj(C_CG, C_ZB) * proj(C_U, C_BG)
    c0 = carry_ref[0:1, :]
    c1 = carry_ref[1:2, :]
    row = lax.broadcasted_iota(jnp.int32, (tm, 1), 0)
    r1 = jnp.where(row == 0, c1, pltpu.roll(cv, 1, 0))
    r2 = jnp.where(row == 0, c0, jnp.where(row == 1, c1, pltpu.roll(cv, 2, 0)))
    cw = cw_ref[...]
    y_conv = cw[0:1, :] * r2 + cw[1:2, :] * r1 + cw[2:3, :] * cv
    carry_ref[0:2, :] = cv[tm - 2:tm, :]
    conv_ref[0] = cv[conv_row - 2:conv_row, :]

    tb = proj(C_BG, C_CG) * y_conv * _silu(proj(C_ZB, C_GA))
    b_out = _dot(_bf(tb), wpb_ref[...])
    gbb_ref[0] = jax.nn.sigmoid(proj(C_GB, C_KI)) * b_out
    sga_ref[0] = jax.nn.sigmoid(proj(C_GA, C_GB))


def _proj_call(x, shift, scale, norm_g, qg, kg, rope_tabs, w_r, w_pb, conv_w, conv_state,
               *, tm, conv_row):
    b, t, d = x.shape
    nt = t // tm
    tok = lambda w: pl.BlockSpec((1, tm, w), lambda i, j: (i, j, 0))
    per_b = lambda r, w: pl.BlockSpec((1, r, w), lambda i, j: (i, 0, 0))
    const = lambda r, w: pl.BlockSpec((r, w), lambda i, j: (0, 0))
    tab = pl.BlockSpec((tm, LANES), lambda i, j: (j, 0))
    kv4 = pl.BlockSpec((1, tm, N_KV, HEAD_DIM), lambda i, j: (i, j, 0, 0))
    whole = pl.BlockSpec(memory_space=pltpu.VMEM)
    f32, bf16 = jnp.float32, jnp.bfloat16
    out_shape = (
        jax.ShapeDtypeStruct((b, t, ATT_W), bf16),
        jax.ShapeDtypeStruct((b, t, N_KV, HEAD_DIM), f32),
        jax.ShapeDtypeStruct((b, t, N_KV, HEAD_DIM), f32),
        jax.ShapeDtypeStruct((b, t, KV_W), bf16),
        jax.ShapeDtypeStruct((b, nt, KV_W, tm), bf16),
        jax.ShapeDtypeStruct((b, t, ATT_W), f32),
        jax.ShapeDtypeStruct((b, t, IDX_HEADS * IDX_DIM), bf16),
        jax.ShapeDtypeStruct((b, t, IDX_DIM), f32),
        jax.ShapeDtypeStruct((b, t, LANES), f32),
        jax.ShapeDtypeStruct((b, t, LANES), f32),
        jax.ShapeDtypeStruct((b, t, D_MODEL), f32),
        jax.ShapeDtypeStruct((b, t, D_MODEL), f32),
        jax.ShapeDtypeStruct((b, CONV_W - 1, D_MODEL), f32),
    )
    out_specs = (
        tok(ATT_W), kv4, kv4, tok(KV_W),
        pl.BlockSpec((1, 1, KV_W, tm), lambda i, j: (i, j, 0, 0)),
        tok(ATT_W), tok(IDX_HEADS * IDX_DIM), tok(IDX_DIM), tok(LANES), tok(LANES),
        tok(D_MODEL), tok(D_MODEL), per_b(CONV_W - 1, D_MODEL),
    )
    in_specs = [
        tok(d), per_b(1, d), per_b(1, d), const(1, d), const(1, LANES), const(1, LANES),
        tab, tab, tab, tab, tab, tab,
        whole, whole, whole, whole, const(CONV_W, d), per_b(CONV_W - 1, d),
    ]
    return pl.pallas_call(
        functools.partial(_proj_kernel, tm=tm, conv_row=conv_row),
        grid=(b, nt),
        in_specs=in_specs,
        out_specs=out_specs,
        out_shape=out_shape,
        scratch_shapes=[pltpu.VMEM((SUBLANES, d), jnp.float32)],
        compiler_params=pltpu.CompilerParams(
            dimension_semantics=("arbitrary", "arbitrary"), vmem_limit_bytes=VMEM_LIMIT),
        name="in_proj",
    )(x, shift, scale, norm_g, qg, kg, *rope_tabs, *w_r, w_pb, conv_w, conv_state)


def _key_tiles(pos0, j, *, l_real, kt_size, qb, minimum=jnp.minimum):
    last_q = pos0 + j * qb + qb - 1
    kmax = minimum(l_real, ((last_q >> CHUNK_SHIFT) + 1) * CHUNK)
    return (kmax + kt_size - 1) // kt_size


def _fold_rows(x, op):
    rows, w = x.shape
    part = op(x.reshape(FOLD_CHAINS, rows // (FOLD_CHAINS * SUBLANES), SUBLANES, w), axis=1)
    return op(part, axis=0)


def _unrolled(n, body, init):
    carry = init
    for i in range(n):
        carry = body(i, carry)
    return carry


def _attn_kernel(*refs, pos0, l_real, kt_size, topk, qb, nkt_values):
    nkt = _key_tiles(pos0, pl.program_id(1), l_real=l_real, kt_size=kt_size, qb=qb)
    for c in nkt_values:
        run = functools.partial(_attn_block, *refs, pos0=pos0, l_real=l_real, kt_size=kt_size,
                                topk=topk, qb=qb, nkt=c)
        if len(nkt_values) == 1:
            run()
        else:
            pl.when(nkt == c)(run)


def _attn_block(q_ref, qi_ref, wi_ref, k_ref, vt_ref, ki2_ref,
                sza_ref, sga_ref, gbb_ref, x_ref, gate_ref, wpa_ref, wout_ref,
                y_ref, sc_ref, bias_ref, s_ref, acc_ref, o_ref,
                *, pos0, l_real, kt_size, topk, qb, nkt):
    KT = kt_size
    QB = qb
    grp = N_HEADS // N_KV
    vt_w = vt_ref.shape[3]
    vt_per_kt = KT // vt_w
    q0 = pos0 + pl.program_id(1) * QB

    lane = lax.broadcasted_iota(jnp.int32, (1, QB), 1)
    q_chunk = (q0 + lane) >> CHUNK_SHIFT
    w_t = wi_ref[0].T
    qi = qi_ref[0]
    lane_kt = lax.broadcasted_iota(jnp.int32, (KT, LANES), 1)
    sub_kt = lax.broadcasted_iota(jnp.int32, (KT, 1), 0)

    def score_body(kt, carry):
        smax8, smin8, sum8, sq8 = carry
        k0 = kt * KT
        ki2 = ki2_ref[0, pl.ds(k0, KT), :]
        ki_lo = _bf(jnp.where(lane_kt < IDX_DIM, ki2, 0.0))
        ki_hi = _bf(jnp.where(lane_kt >= IDX_DIM, ki2, 0.0))
        acc = jnp.zeros((KT, QB), jnp.float32)
        for p in range(IDX_HEADS // 2):
            slab = qi[:, p * LANES:(p + 1) * LANES]
            acc = acc + w_t[2 * p:2 * p + 1, :] * jnp.maximum(_dot_nt(ki_lo, slab), 0.0)
            acc = acc + w_t[2 * p + 1:2 * p + 2, :] * jnp.maximum(_dot_nt(ki_hi, slab), 0.0)
        kpos = k0 + sub_kt
        adm = ((kpos >> CHUNK_SHIFT) <= q_chunk) & (kpos < l_real)
        sc_ref[pl.ds(k0, KT), :] = jnp.where(adm, acc, -jnp.inf)
        sub = acc.reshape(KT // (STAT_STRIDE * SUBLANES), STAT_STRIDE, SUBLANES, QB)[:, 0]
        return (jnp.maximum(smax8, _fold_rows(acc, jnp.max)),
                jnp.minimum(smin8, _fold_rows(acc, jnp.min)),
                sum8 + jnp.sum(sub, axis=0), sq8 + jnp.sum(sub * sub, axis=0))

    smax8, smin8, sum8, sq8 = _unrolled(
        nkt, score_body,
        (jnp.full((SUBLANES, QB), -jnp.inf, jnp.float32),
         jnp.full((SUBLANES, QB), jnp.inf, jnp.float32),
         jnp.zeros((SUBLANES, QB), jnp.float32), jnp.zeros((SUBLANES, QB), jnp.float32)))

    def count_ge(cand):
        def body(kt, acc):
            k0 = kt * KT
            m = (sc_ref[pl.ds(k0, KT), :] >= cand).astype(jnp.int32)
            return acc + _fold_rows(m, jnp.sum)
        acc = _unrolled(nkt, body, jnp.zeros((SUBLANES, QB), jnp.int32))
        return jnp.sum(acc, axis=0, keepdims=True)

    def inside(x, lo, hi):
        return (x > lo) & (x < hi)

    def active_rows(lo, hi, clo):
        return (clo > topk) & inside(lo * 0.5 + hi * 0.5, lo, hi)

    def any_active(st):
        lo, hi, clo, _ = st
        return jnp.max(active_rows(lo, hi, clo).astype(jnp.float32)) > 0.0

    def search_step(it, st, may_be_late=True):
        lo, hi, clo, chi = st
        act = active_rows(lo, hi, clo)
        mid = lo * 0.5 + hi * 0.5
        near_zero = jnp.where(lo >= 0.0, TINY, -TINY)
        late = mid
        if may_be_late:
            geo = jnp.sqrt(jnp.abs(lo)) * jnp.sqrt(jnp.abs(hi)) * jnp.where(lo >= 0.0, 1.0, -1.0)
            late = jnp.where(it < SEARCH_VALUE_IT, mid, geo)
        guess = jnp.where(it == 0, q_below, jnp.where(it == 1, q_above,
                          jnp.where(it == 2, 0.0, jnp.where(it == 3, near_zero, late))))
        cand = jnp.where(inside(guess, lo, hi), guess, mid)
        cnt = count_ge(cand)
        ge = cnt >= topk
        up_lo = act & ge
        up_hi = act & jnp.logical_not(ge)
        return (jnp.where(up_lo, cand, lo), jnp.where(up_hi, cand, hi),
                jnp.where(up_lo, cnt, clo), jnp.where(up_hi, cnt, chi))

    n_adm = jnp.minimum(l_real, (q_chunk + 1) * CHUNK)
    lo0 = jnp.min(smin8, axis=0, keepdims=True)
    smax = jnp.max(smax8, axis=0, keepdims=True)
    hi0 = smax + jnp.abs(smax) * 2.0 ** -10 + 1e-30
    n_stat = nkt * KT // STAT_STRIDE
    mean = jnp.sum(sum8, axis=0, keepdims=True) * (1.0 / n_stat)
    std = jnp.sqrt(jnp.maximum(jnp.sum(sq8, axis=0, keepdims=True) * (1.0 / n_stat) - mean * mean, 0.0))
    z_q = jnp.log(jnp.maximum(n_adm - topk, 1).astype(jnp.float32) * (1.0 / topk)) * (1.0 / 1.7)
    q_below = mean + (z_q - STAT_MARGIN) * std
    q_above = mean + (z_q + STAT_MARGIN) * std
    st = (lo0, hi0, n_adm, jnp.zeros((1, QB), jnp.int32))
    n_first = jnp.where(any_active(st), SEARCH_FIRST_IT, 0)
    assert SEARCH_FIRST_IT <= SEARCH_VALUE_IT
    st = lax.fori_loop(0, n_first, functools.partial(search_step, may_be_late=False), st)

    def search_cond(c):
        return (c[0] < SEARCH_MAX_IT) & any_active(c[1])

    def search_body(c):
        it, st = c
        return it + 2, search_step(it + 1, search_step(it, st))

    _, (thr, _, n_ge, n_gt) = lax.while_loop(search_cond, search_body, (n_first, st))
    need = topk - n_gt
    any_tie = jnp.max((n_ge > topk).astype(jnp.float32)) > 0.0

    def bias_body(kt, carry):
        k0 = kt * KT
        bias_ref[pl.ds(k0, KT), :] = jnp.where(sc_ref[pl.ds(k0, KT), :] >= thr, 0.0, NEG_BIG)
        return carry

    _unrolled(nkt, bias_body, 0)

    @pl.when(any_tie)
    def _():
        tri = _bf((lax.broadcasted_iota(jnp.int32, (LANES, LANES), 1)
                   <= lax.broadcasted_iota(jnp.int32, (LANES, LANES), 0)).astype(jnp.float32))
        need_f = jnp.where(n_ge <= topk, jnp.inf, need.astype(jnp.float32))

        def tie_body(kt, seen):
            for r in range(KT // LANES):
                k0 = kt * KT + r * LANES
                sc = sc_ref[pl.ds(k0, LANES), :]
                eq = sc == thr
                rank = _dot(tri, jnp.where(eq, 1.0, 0.0).astype(jnp.bfloat16)) + seen
                sel = (sc > thr) | (eq & (rank <= need_f))
                bias_ref[pl.ds(k0, LANES), :] = jnp.where(sel, 0.0, NEG_BIG)
                seen = rank[LANES - 1:LANES, :]
            return seen

        _unrolled(nkt, tie_body, jnp.zeros((1, QB), jnp.float32))

    gw = grp * QB
    m8_init = tuple(jnp.full((SUBLANES, QB), NEG_BIG, jnp.float32) for _ in range(grp))
    l8_init = jnp.zeros((SUBLANES, gw), jnp.float32)
    q_groups = [jnp.concatenate(
        [q_ref[0, :, (g * grp + hh) * HEAD_DIM:(g * grp + hh + 1) * HEAD_DIM]
         for hh in range(grp)], axis=0) for g in range(N_KV)]

    def qk_tile(g, kt, m8):
        k0 = kt * KT
        kb = k_ref[0, pl.ds(k0, KT), g * HEAD_DIM:(g + 1) * HEAD_DIM]
        bias = bias_ref[pl.ds(k0, KT), :]
        s = _dot_nt(kb, q_groups[g])
        out = []
        for hh in range(grp):
            sh = s[:, hh * QB:(hh + 1) * QB] + bias
            s_ref[pl.ds(k0, KT), g * gw + hh * QB:g * gw + (hh + 1) * QB] = sh
            out.append(jnp.maximum(
                m8[hh], _fold_rows(sh, jnp.max)))
        return tuple(out)

    def pv_tile(g, kt, l8, m_all):
        k0 = kt * KT
        p = jnp.exp2(s_ref[pl.ds(k0, KT), g * gw:(g + 1) * gw] - m_all)
        pb = _bf(p)
        acc_ref[...] += sum(
            _dot(vt_ref[0, kt * vt_per_kt + r, g * HEAD_DIM:(g + 1) * HEAD_DIM, :],
                 pb[r * vt_w:(r + 1) * vt_w, :])
            for r in range(vt_per_kt))
        return l8 + _fold_rows(p, jnp.sum)

    def col_max(m8):
        return jnp.concatenate([jnp.max(m, axis=0, keepdims=True) for m in m8], axis=1)

    def finish_group(g, l8):
        o_t = acc_ref[...] * (1.0 / jnp.sum(l8, axis=0, keepdims=True))
        for hh in range(grp):
            h_abs = g * grp + hh
            o_ref[:, h_abs * HEAD_DIM:(h_abs + 1) * HEAD_DIM] = o_t[:, hh * QB:(hh + 1) * QB].T

    m8 = _unrolled(nkt, functools.partial(qk_tile, 0), m8_init)
    for g in range(N_KV):
        m_all = col_max(m8)
        acc_ref[...] = jnp.zeros((HEAD_DIM, gw), jnp.float32)
        if g + 1 < N_KV:
            def both(kt, carry, g=g, m_all=m_all):
                return pv_tile(g, kt, carry[0], m_all), qk_tile(g + 1, kt, carry[1])
            l8, m8 = _unrolled(nkt, both, (l8_init, m8_init))
        else:
            l8 = _unrolled(nkt, lambda kt, l8, g=g, m_all=m_all: pv_tile(g, kt, l8, m_all),
                           l8_init)
        finish_group(g, l8)

    a_out = _dot(_bf(o_ref[...] * sza_ref[0]), wpa_ref[...])
    merged = sga_ref[0] * a_out + gbb_ref[0]
    y_ref[0] = x_ref[0] + gate_ref[0] * _dot(_bf(merged), wout_ref[...])


def _attn_call(q, qi, wi, k_all, vt_all, ki2_all, sza, sga, gbb, x, gate, w_pa, w_out,
               *, pos0, l_real, kt_size, topk, qb):
    b, t, d = x.shape
    lp = k_all.shape[1]
    nkt_values = sorted({_key_tiles(pos0, jj, l_real=l_real, kt_size=kt_size, qb=qb, minimum=min)
                         for jj in range(t // qb)})
    tok = lambda w: pl.BlockSpec((1, qb, w), lambda i, j: (i, j, 0))
    per_b = lambda r, w: pl.BlockSpec((1, r, w), lambda i, j: (i, 0, 0))
    whole = pl.BlockSpec(memory_space=pltpu.VMEM)
    in_specs = [
        tok(ATT_W), tok(IDX_HEADS * IDX_DIM), tok(LANES),
        per_b(lp, KV_W),
        pl.BlockSpec((1,) + vt_all.shape[1:], lambda i, j: (i, 0, 0, 0)),
        per_b(lp, LANES),
        tok(ATT_W), tok(D_MODEL), tok(D_MODEL), tok(d), per_b(1, d),
        whole, whole,
    ]
    return pl.pallas_call(
        functools.partial(_attn_kernel, pos0=pos0, l_real=l_real, kt_size=kt_size, topk=topk,
                          qb=qb, nkt_values=nkt_values),
        grid=(b, t // qb),
        in_specs=in_specs,
        out_specs=tok(d),
        out_shape=jax.ShapeDtypeStruct((b, t, d), jnp.float32),
        scratch_shapes=[
            pltpu.VMEM((lp, qb), jnp.float32),
            pltpu.VMEM((lp, qb), jnp.float32),
            pltpu.VMEM((lp, N_HEADS * qb), jnp.float32),
            pltpu.VMEM((HEAD_DIM, N_HEADS // N_KV * qb), jnp.float32),
            pltpu.VMEM((qb, ATT_W), jnp.float32),
        ],
        compiler_params=pltpu.CompilerParams(
            dimension_semantics=("arbitrary", "arbitrary"), vmem_limit_bytes=VMEM_LIMIT),
        name="dsa_attn",
    )(q, qi, wi, k_all, vt_all, ki2_all, sza, sga, gbb, x, gate, w_pa, w_out)


def _rope_tables(pos, width, period):
    rot = period // ROT_FRAC
    half = rot // 2
    inv = ROPE_THETA ** (-2.0 * jnp.arange(half, dtype=jnp.float32) / rot)
    ang = pos.astype(jnp.float32)[:, None] * inv[None, :]
    cos, sin = jnp.cos(ang), jnp.sin(ang)
    n = pos.shape[0]
    rest = period - rot
    one = jnp.ones((n, rest), jnp.float32)
    zero_h = jnp.zeros((n, half), jnp.float32)
    zero_r = jnp.zeros((n, rest), jnp.float32)
    c = jnp.concatenate([cos, cos, one], axis=1)
    s_up = jnp.concatenate([-sin, zero_h, zero_r], axis=1)
    s_dn = jnp.concatenate([zero_h, sin, zero_r], axis=1)
    rep = width // period
    return tuple(jnp.tile(a, (1, rep)) for a in (c, s_up, s_dn))


def _prep_w_in(w_in):
    o_ki = ATT_W + 2 * KV_W + ATT_W + IDX_HEADS * IDX_DIM
    o_wi = o_ki + IDX_DIM
    o_u = o_wi + IDX_HEADS
    ki = w_in[:, o_ki:o_wi]
    pad = jnp.zeros((w_in.shape[0], C_END - C_WI - IDX_HEADS), w_in.dtype)
    w_c = jnp.concatenate([ki, ki, w_in[:, o_wi:o_u], pad], axis=1)
    return _bf(w_in[:, :o_ki]), _bf(w_in[:, o_u:]), _bf(w_c)


def _layer(x, shift, scale, gate, pos, conv_state, past, weights, *, tm, t_real, kt_size, qb):
    norm_g, qg, kg, w_r, w_pb, conv_w, w_pa, w_out = weights
    b, t, _ = x.shape
    tabs = _rope_tables(pos, LANES, HEAD_DIM) + _rope_tables(pos, LANES, IDX_DIM)
    conv_row = t_real - (t // tm - 1) * tm
    (q, k, v, kb, vt, sza, qi, ki, ki2, wi, sga, gbb, conv_new) = _proj_call(
        x, shift, scale, norm_g, qg, kg, tabs, w_r, w_pb, conv_w, conv_state,
        tm=tm, conv_row=conv_row)
    if past is None:
        p_len = 0
        k_all, vt_all, ki2_all = kb, vt, ki2
    else:
        past_k, past_v, past_ki = past
        p_len = past_k.shape[1]
        lp = -(-(p_len + t) // kt_size) * kt_size
        extra = lp - p_len - t
        k_all = jnp.concatenate(
            [_bf(past_k), kb, jnp.zeros((b, extra, KV_W), jnp.bfloat16)], axis=1)
        vt_all = jnp.concatenate(
            [_bf(jnp.swapaxes(past_v, 1, 2)), vt[:, 0], jnp.zeros((b, KV_W, extra), jnp.bfloat16)],
            axis=2)
        vt_all = jnp.swapaxes(vt_all.reshape(b, KV_W, lp // kt_size, kt_size), 1, 2)
        ki2_all = jnp.concatenate(
            [jnp.concatenate([past_ki, past_ki], axis=-1), ki2,
             jnp.zeros((b, extra, LANES), jnp.float32)], axis=1)
    l_real = p_len + t_real
    topk = min(TOPK_MAX, l_real // 4)
    y = _attn_call(q, qi, wi, k_all, vt_all, ki2_all, sza, sga, gbb, x, gate, w_pa, w_out,
                   pos0=p_len, l_real=l_real, kt_size=kt_size, topk=topk, qb=qb)
    return y, k, v, ki, conv_new


def kernel(x_prompt, x_sample, cache_k, cache_v, cache_idx_k, state_conv, c_prompt, c_sample,
           w_ada, b_ada, norm_g, w_in, q_norm_g, k_norm_g, conv_w, w_pa, w_pb, w_out):
    bp, seq, d = x_prompt.shape
    bs, dec_seq, _ = x_sample.shape
    past_len = cache_k.shape[1]

    mod = _mod_call(jnp.concatenate([c_prompt, c_sample], axis=0), w_ada, b_ada)
    shift, scale, gate = (m.reshape(bp + bs, 1, d) for m in jnp.split(mod, 3, axis=-1))

    weights = (norm_g.reshape(1, d), q_norm_g.reshape(1, HEAD_DIM), k_norm_g.reshape(1, HEAD_DIM),
               _prep_w_in(w_in), _bf(w_pb), conv_w, _bf(w_pa), _bf(w_out))

    yp, k_p, v_p, ki_p, conv_p = _layer(
        x_prompt, shift[:bp], scale[:bp], gate[:bp], jnp.arange(seq, dtype=jnp.int32),
        jnp.zeros((bp, CONV_W - 1, d), x_prompt.dtype), None, weights,
        tm=256, t_real=seq, kt_size=512, qb=QB_PROMPT)

    x_pad = jnp.pad(x_sample, ((0, 0), (0, QB_SAMPLE - dec_seq), (0, 0)))
    past = (cache_k.reshape(bs, past_len, KV_W), cache_v.reshape(bs, past_len, KV_W), cache_idx_k)
    ys, k_s, v_s, ki_s, conv_s = _layer(
        x_pad, shift[bp:], scale[bp:], gate[bp:],
        past_len + jnp.arange(QB_SAMPLE, dtype=jnp.int32), state_conv, past, weights,
        tm=QB_SAMPLE, t_real=dec_seq, kt_size=384, qb=QB_SAMPLE)

    return (yp, ys[:, :dec_seq], k_p, v_p, ki_p, conv_p,
            k_s[:, :dec_seq], v_s[:, :dec_seq], ki_s[:, :dec_seq], conv_s)
```

```python
import functools
import math

import jax
import jax.numpy as jnp
from jax import lax
from jax.experimental import pallas as pl
from jax.experimental.pallas import tpu as pltpu

D_MODEL = 1024
CHUNK = 64
CHUNK_SHIFT = 6
N_HEADS = 8
N_KV = 2
HEAD_DIM = 128
ATT_W = N_HEADS * HEAD_DIM
KV_W = N_KV * HEAD_DIM
ROT_FRAC = 4
ROPE_THETA = 500000.0
IDX_HEADS = 8
IDX_DIM = 64
TOPK_MAX = 256
CONV_W = 3
EPS = 1e-6

LANES = 128
SUBLANES = 8
QB_PROMPT = 256
QB_SAMPLE = 128
Q_SCALE = HEAD_DIM ** -0.5 * math.log2(math.e)
VMEM_LIMIT = 56 * 1024 * 1024

C_Q, C_K, C_V, C_ZA, C_QI = 0, 1024, 1280, 1536, 2560
C_U, C_BG, C_CG, C_ZB, C_GA, C_GB = 3072, 4096, 5120, 6144, 7168, 8192
C_KI, C_WI, C_END = 9216, 9344, 9472

NEG_BIG = -1e30
TINY = 1.1754943508222875e-38
FOLD_CHAINS = 8
STAT_STRIDE = 4
STAT_MARGIN = 0.2
SEARCH_VALUE_IT = 18
SEARCH_FIRST_IT = 18
SEARCH_MAX_IT = SEARCH_VALUE_IT + 36

_NT = (((1,), (1,)), ((), ()))


def _bf(x):
    return x.astype(jnp.bfloat16)


def _dot(a, b):
    return jnp.dot(a, b, preferred_element_type=jnp.float32)


def _dot_nt(a, b):
    return lax.dot_general(a, b, _NT, preferred_element_type=jnp.float32)


def _silu(x):
    return x * jax.nn.sigmoid(x)


def _mod_kernel(c_ref, w_ref, b_ref, o_ref):
    c = c_ref[...]
    s = _silu(c)
    w = w_ref[...]
    s_hi = _bf(s)
    s_lo = _bf(s - s_hi.astype(jnp.float32))
    w_hi = _bf(w)
    w_lo = _bf(w - w_hi.astype(jnp.float32))
    acc = _dot(s_hi, w_hi) + (_dot(s_lo, w_hi) + _dot(s_hi, w_lo))
    o_ref[...] = acc + b_ref[...]


def _mod_call(c_all, w_ada, b_ada):
    nb, d = c_all.shape
    n = w_ada.shape[1]
    bn = 1024
    return pl.pallas_call(
        _mod_kernel,
        grid=(n // bn,),
        in_specs=[
            pl.BlockSpec((nb, d), lambda i: (0, 0)),
            pl.BlockSpec((d, bn), lambda i: (0, i)),
            pl.BlockSpec((1, bn), lambda i: (0, i)),
        ],
        out_specs=pl.BlockSpec((nb, bn), lambda i: (0, i)),
        out_shape=jax.ShapeDtypeStruct((nb, n), jnp.float32),
        compiler_params=pltpu.CompilerParams(dimension_semantics=("arbitrary",)),
        name="adaln_mod",
    )(c_all, w_ada, b_ada.reshape(1, n))


def _rope(x, cos, sin_up, sin_dn, half):
    up = pltpu.roll(x, LANES - half, 1)
    dn = pltpu.roll(x, half, 1)
    return x * cos + up * sin_up + dn * sin_dn


def _proj_kernel(x_ref, shift_ref, scale_ref, ng_ref, qg_ref, kg_ref,
                 rc_ref, rs1_ref, rs2_ref, ic_ref, is1_ref, is2_ref,
                 wa_ref, wb_ref, wc_ref, wpb_ref, cw_ref, cs_ref,
                 q_ref, k_ref, v_ref, kb_ref, vt_ref, sza_ref, qi_ref, ki_ref, ki2_ref, wi_ref,
                 sga_ref, gbb_ref, conv_ref, carry_ref, *, tm, conv_row):
    @pl.when(pl.program_id(1) == 0)
    def _():
        carry_ref[0:2, :] = cs_ref[0]

    x = x_ref[0]
    ms = jnp.mean(x * x, axis=-1, keepdims=True)
    gain = ng_ref[...] * (1.0 + scale_ref[0])
    h = x * lax.rsqrt(ms + EPS) * gain + shift_ref[0]
    hb = _bf(h)

    def proj(c0, c1):
        for w_ref, base in ((wc_ref, C_KI), (wb_ref, C_U), (wa_ref, C_Q)):
            if c0 >= base:
                return _dot(hb, w_ref[:, c0 - base:c1 - base])

    rc, rs1, rs2 = rc_ref[...], rs1_ref[...], rs2_ref[...]
    ic, is1, is2 = ic_ref[...], is1_ref[...], is2_ref[...]
    rot_half = HEAD_DIM // ROT_FRAC // 2
    idx_half = IDX_DIM // ROT_FRAC // 2

    def head_norm_rope(xh, g):
        r = lax.rsqrt(jnp.mean(xh * xh, axis=-1, keepdims=True) + EPS)
        return _rope(xh * r * g, rc, rs1, rs2, rot_half)

    q = proj(C_Q, C_K)
    qg = qg_ref[...]
    for hh in range(N_HEADS):
        sl = slice(hh * HEAD_DIM, (hh + 1) * HEAD_DIM)
        q_ref[0, :, sl] = _bf(head_norm_rope(q[:, sl], qg) * Q_SCALE)

    k = proj(C_K, C_V)
    kg = kg_ref[...]
    v = proj(C_V, C_ZA)
    for hh in range(N_KV):
        sl = slice(hh * HEAD_DIM, (hh + 1) * HEAD_DIM)
        kh = head_norm_rope(k[:, sl], kg)
        k_ref[0, :, hh, :] = kh
        kb_ref[0, :, sl] = _bf(kh)
        v_ref[0, :, hh, :] = v[:, sl]
    vt_ref[0, 0] = _bf(v.T)

    sza_ref[0] = _silu(proj(C_ZA, C_QI))

    qi = proj(C_QI, C_U)
    for p in range(IDX_HEADS * IDX_DIM // LANES):
        sl = slice(p * LANES, (p + 1) * LANES)
        qi_ref[0, :, sl] = _bf(_rope(qi[:, sl], ic, is1, is2, idx_half) * (IDX_DIM ** -0.5))

    ki2 = _rope(proj(C_KI, C_WI), ic, is1, is2, idx_half)
    ki2_ref[0] = ki2
    ki_ref[0] = ki2[:, :IDX_DIM]
    wi_ref[0] = proj(C_WI, C_END) * (IDX_HEADS ** -0.5)

    cv = proj(C_CG, C_ZB) * proj(C_U, C_BG)
    c0 = carry_ref[0:1, :]
    c1 = carry_ref[1:2, :]
    row = lax.broadcasted_iota(jnp.int32, (tm, 1), 0)
    r1 = jnp.where(row == 0, c1, pltpu.roll(cv, 1, 0))
    r2 = jnp.where(row == 0, c0, jnp.where(row == 1, c1, pltpu.roll(cv, 2, 0)))
    cw = cw_ref[...]
    y_conv = cw[0:1, :] * r2 + cw[1:2, :] * r1 + cw[2:3, :] * cv
    carry_ref[0:2, :] = cv[tm - 2:tm, :]
    conv_ref[0] = cv[conv_row - 2:conv_row, :]

    tb = proj(C_BG, C_CG) * y_conv * _silu(proj(C_ZB, C_GA))
    b_out = _dot(_bf(tb), wpb_ref[...])
    gbb_ref[0] = jax.nn.sigmoid(proj(C_GB, C_KI)) * b_out
    sga_ref[0] = jax.nn.sigmoid(proj(C_GA, C_GB))


def _proj_call(x, shift, scale, norm_g, qg, kg, rope_tabs, w_r, w_pb, conv_w, conv_state,
               *, tm, conv_row):
    b, t, d = x.shape
    nt = t // tm
    tok = lambda w: pl.BlockSpec((1, tm, w), lambda i, j: (i, j, 0))
    per_b = lambda r, w: pl.BlockSpec((1, r, w), lambda i, j: (i, 0, 0))
    const = lambda r, w: pl.BlockSpec((r, w), lambda i, j: (0, 0))
    tab = pl.BlockSpec((tm, LANES), lambda i, j: (j, 0))
    kv4 = pl.BlockSpec((1, tm, N_KV, HEAD_DIM), lambda i, j: (i, j, 0, 0))
    whole = pl.BlockSpec(memory_space=pltpu.VMEM)
    f32, bf16 = jnp.float32, jnp.bfloat16
    out_shape = (
        jax.ShapeDtypeStruct((b, t, ATT_W), bf16),
        jax.ShapeDtypeStruct((b, t, N_KV, HEAD_DIM), f32),
        jax.ShapeDtypeStruct((b, t, N_KV, HEAD_DIM), f32),
        jax.ShapeDtypeStruct((b, t, KV_W), bf16),
        jax.ShapeDtypeStruct((b, nt, KV_W, tm), bf16),
        jax.ShapeDtypeStruct((b, t, ATT_W), f32),
        jax.ShapeDtypeStruct((b, t, IDX_HEADS * IDX_DIM), bf16),
        jax.ShapeDtypeStruct((b, t, IDX_DIM), f32),
        jax.ShapeDtypeStruct((b, t, LANES), f32),
        jax.ShapeDtypeStruct((b, t, LANES), f32),
        jax.ShapeDtypeStruct((b, t, D_MODEL), f32),
        jax.ShapeDtypeStruct((b, t, D_MODEL), f32),
        jax.ShapeDtypeStruct((b, CONV_W - 1, D_MODEL), f32),
    )
    out_specs = (
        tok(ATT_W), kv4, kv4, tok(KV_W),
        pl.BlockSpec((1, 1, KV_W, tm), lambda i, j: (i, j, 0, 0)),
        tok(ATT_W), tok(IDX_HEADS * IDX_DIM), tok(IDX_DIM), tok(LANES), tok(LANES),
        tok(D_MODEL), tok(D_MODEL), per_b(CONV_W - 1, D_MODEL),
    )
    in_specs = [
        tok(d), per_b(1, d), per_b(1, d), const(1, d), const(1, LANES), const(1, LANES),
        tab, tab, tab, tab, tab, tab,
        whole, whole, whole, whole, const(CONV_W, d), per_b(CONV_W - 1, d),
    ]
    return pl.pallas_call(
        functools.partial(_proj_kernel, tm=tm, conv_row=conv_row),
        grid=(b, nt),
        in_specs=in_specs,
        out_specs=out_specs,
        out_shape=out_shape,
        scratch_shapes=[pltpu.VMEM((SUBLANES, d), jnp.float32)],
        compiler_params=pltpu.CompilerParams(
            dimension_semantics=("arbitrary", "arbitrary"), vmem_limit_bytes=VMEM_LIMIT),
        name="in_proj",
    )(x, shift, scale, norm_g, qg, kg, *rope_tabs, *w_r, w_pb, conv_w, conv_state)


def _key_tiles(pos0, j, *, l_real, kt_size, qb, minimum=jnp.minimum):
    last_q = pos0 + j * qb + qb - 1
    kmax = minimum(l_real, ((last_q >> CHUNK_SHIFT) + 1) * CHUNK)
    return (kmax + kt_size - 1) // kt_size


def _fold_rows(x, op):
    rows, w = x.shape
    part = op(x.reshape(FOLD_CHAINS, rows // (FOLD_CHAINS * SUBLANES), SUBLANES, w), axis=1)
    return op(part, axis=0)


def _unrolled(n, body, init):
    carry = init
    for i in range(n):
        carry = body(i, carry)
    return carry


def _attn_kernel(*refs, pos0, l_real, kt_size, topk, qb, nkt_values):
    nkt = _key_tiles(pos0, pl.program_id(1), l_real=l_real, kt_size=kt_size, qb=qb)
    for c in nkt_values:
        run = functools.partial(_attn_block, *refs, pos0=pos0, l_real=l_real, kt_size=kt_size,
                                topk=topk, qb=qb, nkt=c)
        if len(nkt_values) == 1:
            run()
        else:
            pl.when(nkt == c)(run)


def _attn_block(q_ref, qi_ref, wi_ref, k_ref, vt_ref, ki2_ref,
                sza_ref, sga_ref, gbb_ref, x_ref, gate_ref, wpa_ref, wout_ref,
                y_ref, sc_ref, bias_ref, s_ref, acc_ref, o_ref,
                *, pos0, l_real, kt_size, topk, qb, nkt):
    KT = kt_size
    QB = qb
    grp = N_HEADS // N_KV
    vt_w = vt_ref.shape[3]
    vt_per_kt = KT // vt_w
    q0 = pos0 + pl.program_id(1) * QB

    lane = lax.broadcasted_iota(jnp.int32, (1, QB), 1)
    q_chunk = (q0 + lane) >> CHUNK_SHIFT
    w_t = wi_ref[0].T
    qi = qi_ref[0]
    lane_kt = lax.broadcasted_iota(jnp.int32, (KT, LANES), 1)
    sub_kt = lax.broadcasted_iota(jnp.int32, (KT, 1), 0)

    def score_body(kt, carry):
        smax8, smin8, sum8, sq8 = carry
        k0 = kt * KT
        ki2 = ki2_ref[0, pl.ds(k0, KT), :]
        ki_lo = _bf(jnp.where(lane_kt < IDX_DIM, ki2, 0.0))
        ki_hi = _bf(jnp.where(lane_kt >= IDX_DIM, ki2, 0.0))
        acc = jnp.zeros((KT, QB), jnp.float32)
        for p in range(IDX_HEADS // 2):
            slab = qi[:, p * LANES:(p + 1) * LANES]
            acc = acc + w_t[2 * p:2 * p + 1, :] * jnp.maximum(_dot_nt(ki_lo, slab), 0.0)
            acc = acc + w_t[2 * p + 1:2 * p + 2, :] * jnp.maximum(_dot_nt(ki_hi, slab), 0.0)
        kpos = k0 + sub_kt
        adm = ((kpos >> CHUNK_SHIFT) <= q_chunk) & (kpos < l_real)
        sc_ref[pl.ds(k0, KT), :] = jnp.where(adm, acc, -jnp.inf)
        sub = acc.reshape(KT // (STAT_STRIDE * SUBLANES), STAT_STRIDE, SUBLANES, QB)[:, 0]
        return (jnp.maximum(smax8, _fold_rows(acc, jnp.max)),
                jnp.minimum(smin8, _fold_rows(acc, jnp.min)),
                sum8 + jnp.sum(sub, axis=0), sq8 + jnp.sum(sub * sub, axis=0))

    smax8, smin8, sum8, sq8 = _unrolled(
        nkt, score_body,
        (jnp.full((SUBLANES, QB), -jnp.inf, jnp.float32),
         jnp.full((SUBLANES, QB), jnp.inf, jnp.float32),
         jnp.zeros((SUBLANES, QB), jnp.float32), jnp.zeros((SUBLANES, QB), jnp.float32)))

    def count_ge(cand):
        def body(kt, acc):
            k0 = kt * KT
            m = (sc_ref[pl.ds(k0, KT), :] >= cand).astype(jnp.int32)
            return acc + _fold_rows(m, jnp.sum)
        acc = _unrolled(nkt, body, jnp.zeros((SUBLANES, QB), jnp.int32))
        return jnp.sum(acc, axis=0, keepdims=True)

    def inside(x, lo, hi):
        return (x > lo) & (x < hi)

    def active_rows(lo, hi, clo):
        return (clo > topk) & inside(lo * 0.5 + hi * 0.5, lo, hi)

    def any_active(st):
        lo, hi, clo, _ = st
        return jnp.max(active_rows(lo, hi, clo).astype(jnp.float32)) > 0.0

    def search_step(it, st, may_be_late=True):
        lo, hi, clo, chi = st
        act = active_rows(lo, hi, clo)
        mid = lo * 0.5 + hi * 0.5
        near_zero = jnp.where(lo >= 0.0, TINY, -TINY)
        late = mid
        if may_be_late:
            geo = jnp.sqrt(jnp.abs(lo)) * jnp.sqrt(jnp.abs(hi)) * jnp.where(lo >= 0.0, 1.0, -1.0)
            late = jnp.where(it < SEARCH_VALUE_IT, mid, geo)
        guess = jnp.where(it == 0, q_below, jnp.where(it == 1, q_above,
                          jnp.where(it == 2, 0.0, jnp.where(it == 3, near_zero, late))))
        cand = jnp.where(inside(guess, lo, hi), guess, mid)
        cnt = count_ge(cand)
        ge = cnt >= topk
        up_lo = act & ge
        up_hi = act & jnp.logical_not(ge)
        return (jnp.where(up_lo, cand, lo), jnp.where(up_hi, cand, hi),
                jnp.where(up_lo, cnt, clo), jnp.where(up_hi, cnt, chi))

    n_adm = jnp.minimum(l_real, (q_chunk + 1) * CHUNK)
    lo0 = jnp.min(smin8, axis=0, keepdims=True)
    smax = jnp.max(smax8, axis=0, keepdims=True)
    hi0 = smax + jnp.abs(smax) * 2.0 ** -10 + 1e-30
    n_stat = nkt * KT // STAT_STRIDE
    mean = jnp.sum(sum8, axis=0, keepdims=True) * (1.0 / n_stat)
    std = jnp.sqrt(jnp.maximum(jnp.sum(sq8, axis=0, keepdims=True) * (1.0 / n_stat) - mean * mean, 0.0))
    z_q = jnp.log(jnp.maximum(n_adm - topk, 1).astype(jnp.float32) * (1.0 / topk)) * (1.0 / 1.7)
    q_below = mean + (z_q - STAT_MARGIN) * std
    q_above = mean + (z_q + STAT_MARGIN) * std
    st = (lo0, hi0, n_adm, jnp.zeros((1, QB), jnp.int32))
    n_first = jnp.where(any_active(st), SEARCH_FIRST_IT, 0)
    assert SEARCH_FIRST_IT <= SEARCH_VALUE_IT
    st = lax.fori_loop(0, n_first, functools.partial(search_step, may_be_late=False), st)

    def search_cond(c):
        return (c[0] < SEARCH_MAX_IT) & any_active(c[1])

    def search_body(c):
        it, st = c
        return it + 2, search_step(it + 1, search_step(it, st))

    _, (thr, _, n_ge, n_gt) = lax.while_loop(search_cond, search_body, (n_first, st))
    need = topk - n_gt
    any_tie = jnp.max((n_ge > topk).astype(jnp.float32)) > 0.0

    def bias_body(kt, carry):
        k0 = kt * KT
        bias_ref[pl.ds(k0, KT), :] = jnp.where(sc_ref[pl.ds(k0, KT), :] >= thr, 0.0, NEG_BIG)
        return carry

    _unrolled(nkt, bias_body, 0)

    @pl.when(any_tie)
    def _():
        tri = _bf((lax.broadcasted_iota(jnp.int32, (LANES, LANES), 1)
                   <= lax.broadcasted_iota(jnp.int32, (LANES, LANES), 0)).astype(jnp.float32))
        need_f = jnp.where(n_ge <= topk, jnp.inf, need.astype(jnp.float32))

        def tie_body(kt, seen):
            for r in range(KT // LANES):
                k0 = kt * KT + r * LANES
                sc = sc_ref[pl.ds(k0, LANES), :]
                eq = sc == thr
                rank = _dot(tri, jnp.where(eq, 1.0, 0.0).astype(jnp.bfloat16)) + seen
                sel = (sc > thr) | (eq & (rank <= need_f))
                bias_ref[pl.ds(k0, LANES), :] = jnp.where(sel, 0.0, NEG_BIG)
                seen = rank[LANES - 1:LANES, :]
            return seen

        _unrolled(nkt, tie_body, jnp.zeros((1, QB), jnp.float32))

    gw = grp * QB
    m8_init = tuple(jnp.full((SUBLANES, QB), NEG_BIG, jnp.float32) for _ in range(grp))
    l8_init = jnp.zeros((SUBLANES, gw), jnp.float32)
    q_groups = [jnp.concatenate(
        [q_ref[0, :, (g * grp + hh) * HEAD_DIM:(g * grp + hh + 1) * HEAD_DIM]
         for hh in range(grp)], axis=0) for g in range(N_KV)]

    def qk_tile(g, kt, m8):
        k0 = kt * KT
        kb = k_ref[0, pl.ds(k0, KT), g * HEAD_DIM:(g + 1) * HEAD_DIM]
        bias = bias_ref[pl.ds(k0, KT), :]
        s = _dot_nt(kb, q_groups[g])
        out = []
        for hh in range(grp):
            sh = s[:, hh * QB:(hh + 1) * QB] + bias
            s_ref[pl.ds(k0, KT), g * gw + hh * QB:g * gw + (hh + 1) * QB] = sh
            out.append(jnp.maximum(
                m8[hh], _fold_rows(sh, jnp.max)))
        return tuple(out)

    def pv_tile(g, kt, l8, m_all):
        k0 = kt * KT
        p = jnp.exp2(s_ref[pl.ds(k0, KT), g * gw:(g + 1) * gw] - m_all)
        pb = _bf(p)
        acc_ref[...] += sum(
            _dot(vt_ref[0, kt * vt_per_kt + r, g * HEAD_DIM:(g + 1) * HEAD_DIM, :],
                 pb[r * vt_w:(r + 1) * vt_w, :])
            for r in range(vt_per_kt))
        return l8 + _fold_rows(p, jnp.sum)

    def col_max(m8):
        return jnp.concatenate([jnp.max(m, axis=0, keepdims=True) for m in m8], axis=1)

    def finish_group(g, l8):
        o_t = acc_ref[...] * (1.0 / jnp.sum(l8, axis=0, keepdims=True))
        for hh in range(grp):
            h_abs = g * grp + hh
            o_ref[:, h_abs * HEAD_DIM:(h_abs + 1) * HEAD_DIM] = o_t[:, hh * QB:(hh + 1) * QB].T

    m8 = _unrolled(nkt, functools.partial(qk_tile, 0), m8_init)
    for g in range(N_KV):
        m_all = col_max(m8)
        acc_ref[...] = jnp.zeros((HEAD_DIM, gw), jnp.float32)
        if g + 1 < N_KV:
            def both(kt, carry, g=g, m_all=m_all):
                return pv_tile(g, kt, carry[0], m_all), qk_tile(g + 1, kt, carry[1])
            l8, m8 = _unrolled(nkt, both, (l8_init, m8_init))
        else:
            l8 = _unrolled(nkt, lambda kt, l8, g=g, m_all=m_all: pv_tile(g, kt, l8, m_all),
                           l8_init)
        finish_group(g, l8)

    a_out = _dot(_bf(o_ref[...] * sza_ref[0]), wpa_ref[...])
    merged = sga_ref[0] * a_out + gbb_ref[0]
    y_ref[0] = x_ref[0] + gate_ref[0] * _dot(_bf(merged), wout_ref[...])


def _attn_call(q, qi, wi, k_all, vt_all, ki2_all, sza, sga, gbb, x, gate, w_pa, w_out,
               *, pos0, l_real, kt_size, topk, qb):
    b, t, d = x.shape
    lp = k_all.shape[1]
    nkt_values = sorted({_key_tiles(pos0, jj, l_real=l_real, kt_size=kt_size, qb=qb, minimum=min)
                         for jj in range(t // qb)})
    tok = lambda w: pl.BlockSpec((1, qb, w), lambda i, j: (i, j, 0))
    per_b = lambda r, w: pl.BlockSpec((1, r, w), lambda i, j: (i, 0, 0))
    whole = pl.BlockSpec(memory_space=pltpu.VMEM)
    in_specs = [
        tok(ATT_W), tok(IDX_HEADS * IDX_DIM), tok(LANES),
        per_b(lp, KV_W),
        pl.BlockSpec((1,) + vt_all.shape[1:], lambda i, j: (i, 0, 0, 0)),
        per_b(lp, LANES),
        tok(ATT_W), tok(D_MODEL), tok(D_MODEL), tok(d), per_b(1, d),
        whole, whole,
    ]
    return pl.pallas_call(
        functools.partial(_attn_kernel, pos0=pos0, l_real=l_real, kt_size=kt_size, topk=topk,
                          qb=qb, nkt_values=nkt_values),
        grid=(b, t // qb),
        in_specs=in_specs,
        out_specs=tok(d),
        out_shape=jax.ShapeDtypeStruct((b, t, d), jnp.float32),
        scratch_shapes=[
            pltpu.VMEM((lp, qb), jnp.float32),
            pltpu.VMEM((lp, qb), jnp.float32),
            pltpu.VMEM((lp, N_HEADS * qb), jnp.float32),
            pltpu.VMEM((HEAD_DIM, N_HEADS // N_KV * qb), jnp.float32),
            pltpu.VMEM((qb, ATT_W), jnp.float32),
        ],
        compiler_params=pltpu.CompilerParams(
            dimension_semantics=("arbitrary", "arbitrary"), vmem_limit_bytes=VMEM_LIMIT),
        name="dsa_attn",
    )(q, qi, wi, k_all, vt_all, ki2_all, sza, sga, gbb, x, gate, w_pa, w_out)


def _rope_tables(pos, width, period):
    rot = period // ROT_FRAC
    half = rot // 2
    inv = ROPE_THETA ** (-2.0 * jnp.arange(half, dtype=jnp.float32) / rot)
    ang = pos.astype(jnp.float32)[:, None] * inv[None, :]
    cos, sin = jnp.cos(ang), jnp.sin(ang)
    n = pos.shape[0]
    rest = period - rot
    one = jnp.ones((n, rest), jnp.float32)
    zero_h = jnp.zeros((n, half), jnp.float32)
    zero_r = jnp.zeros((n, rest), jnp.float32)
    c = jnp.concatenate([cos, cos, one], axis=1)
    s_up = jnp.concatenate([-sin, zero_h, zero_r], axis=1)
    s_dn = jnp.concatenate([zero_h, sin, zero_r], axis=1)
    rep = width // period
    return tuple(jnp.tile(a, (1, rep)) for a in (c, s_up, s_dn))


def _prep_w_in(w_in):
    o_ki = ATT_W + 2 * KV_W + ATT_W + IDX_HEADS * IDX_DIM
    o_wi = o_ki + IDX_DIM
    o_u = o_wi + IDX_HEADS
    ki = w_in[:, o_ki:o_wi]
    pad = jnp.zeros((w_in.shape[0], C_END - C_WI - IDX_HEADS), w_in.dtype)
    w_c = jnp.concatenate([ki, ki, w_in[:, o_wi:o_u], pad], axis=1)
    return _bf(w_in[:, :o_ki]), _bf(w_in[:, o_u:]), _bf(w_c)


def _layer(x, shift, scale, gate, pos, conv_state, past, weights, *, tm, t_real, kt_size, qb):
    norm_g, qg, kg, w_r, w_pb, conv_w, w_pa, w_out = weights
    b, t, _ = x.shape
    tabs = _rope_tables(pos, LANES, HEAD_DIM) + _rope_tables(pos, LANES, IDX_DIM)
    conv_row = t_real - (t // tm - 1) * tm
    (q, k, v, kb, vt, sza, qi, ki, ki2, wi, sga, gbb, conv_new) = _proj_call(
        x, shift, scale, norm_g, qg, kg, tabs, w_r, w_pb, conv_w, conv_state,
        tm=tm, conv_row=conv_row)
    if past is None:
        p_len = 0
        k_all, vt_all, ki2_all = kb, vt, ki2
    else:
        past_k, past_v, past_ki = past
        p_len = past_k.shape[1]
        lp = -(-(p_len + t) // kt_size) * kt_size
        extra = lp - p_len - t
        k_all = jnp.concatenate(
            [_bf(past_k), kb, jnp.zeros((b, extra, KV_W), jnp.bfloat16)], axis=1)
        vt_all = jnp.concatenate(
            [_bf(jnp.swapaxes(past_v, 1, 2)), vt[:, 0], jnp.zeros((b, KV_W, extra), jnp.bfloat16)],
            axis=2)
        vt_all = jnp.swapaxes(vt_all.reshape(b, KV_W, lp // kt_size, kt_size), 1, 2)
        ki2_all = jnp.concatenate(
            [jnp.concatenate([past_ki, past_ki], axis=-1), ki2,
             jnp.zeros((b, extra, LANES), jnp.float32)], axis=1)
    l_real = p_len + t_real
    topk = min(TOPK_MAX, l_real // 4)
    y = _attn_call(q, qi, wi, k_all, vt_all, ki2_all, sza, sga, gbb, x, gate, w_pa, w_out,
                   pos0=p_len, l_real=l_real, kt_size=kt_size, topk=topk, qb=qb)
    return y, k, v, ki, conv_new


def kernel(x_prompt, x_sample, cache_k, cache_v, cache_idx_k, state_conv, c_prompt, c_sample,
           w_ada, b_ada, norm_g, w_in, q_norm_g, k_norm_g, conv_w, w_pa, w_pb, w_out):
    bp, seq, d = x_prompt.shape
    bs, dec_seq, _ = x_sample.shape
    past_len = cache_k.shape[1]

    mod = _mod_call(jnp.concatenate([c_prompt, c_sample], axis=0), w_ada, b_ada)
    shift, scale, gate = (m.reshape(bp + bs, 1, d) for m in jnp.split(mod, 3, axis=-1))

    weights = (norm_g.reshape(1, d), q_norm_g.reshape(1, HEAD_DIM), k_norm_g.reshape(1, HEAD_DIM),
               _prep_w_in(w_in), _bf(w_pb), conv_w, _bf(w_pa), _bf(w_out))

    yp, k_p, v_p, ki_p, conv_p = _layer(
        x_prompt, shift[:bp], scale[:bp], gate[:bp], jnp.arange(seq, dtype=jnp.int32),
        jnp.zeros((bp, CONV_W - 1, d), x_prompt.dtype), None, weights,
        tm=256, t_real=seq, kt_size=512, qb=QB_PROMPT)

    x_pad = jnp.pad(x_sample, ((0, 0), (0, QB_SAMPLE - dec_seq), (0, 0)))
    past = (cache_k.reshape(bs, past_len, KV_W), cache_v.reshape(bs, past_len, KV_W), cache_idx_k)
    ys, k_s, v_s, ki_s, conv_s = _layer(
        x_pad, shift[bp:], scale[bp:], gate[bp:],
        past_len + jnp.arange(QB_SAMPLE, dtype=jnp.int32), state_conv, past, weights,
        tm=QB_SAMPLE, t_real=dec_seq, kt_size=384, qb=QB_SAMPLE)

    return (yp, ys[:, :dec_seq], k_p, v_p, ki_p, conv_p,
            k_s[:, :dec_seq], v_s[:, :dec_seq], ki_s[:, :dec_seq], conv_s)
```
